```python
import math
import jax, jax.numpy as jnp
from jax import lax
import numpy as np

D_MODEL = 2048
BATCH = 1
SEQ = 8192
DEPTH = 4

HEAD_DIM = 64
ROT_DIM = HEAD_DIM // 4
ROPE_THETA = 500000.0
Q_BLOCK = 128
MOBA_HEADS = 8
MOBA_BLOCK = 256
MOBA_TOPK = 3
SSM_HEADS = 8
SSM_HEAD_DIM = 64
SSM_INNER = SSM_HEADS * SSM_HEAD_DIM
SSM_STATE = 128
SSM_GROUPS = 2
SSM_CONV = 4
SSM_CONV_CH = SSM_INNER + 2 * SSM_GROUPS * SSM_STATE
SSM_CHUNK = 256
DSA_HEADS = 8
IDX_HEADS = 4
IDX_DIM = 64
DSA_TOPK = 256
IDX_SCALE = (IDX_HEADS * IDX_DIM) ** -0.5
MLA_HEADS = 8
MLA_Q_RANK = 384
MLA_KV_RANK = 256
MLA_NOPE = 64
MLA_ROPE = 32
MLA_V = 64
MLA_THETA = 10000.0
N_BRANCH = 4
BRANCH_W = 512
SPLIT_SIZES = (
    MOBA_HEADS * HEAD_DIM, MOBA_HEADS * HEAD_DIM, MOBA_HEADS * HEAD_DIM,
    SSM_INNER, SSM_CONV_CH, SSM_HEADS,
    DSA_HEADS * HEAD_DIM, DSA_HEADS * HEAD_DIM, DSA_HEADS * HEAD_DIM,
    IDX_HEADS * IDX_DIM, IDX_DIM, IDX_HEADS,
    MLA_Q_RANK, MLA_KV_RANK, MLA_ROPE,
    N_BRANCH * D_MODEL,
)
D_IN = sum(SPLIT_SIZES)
N_EXPERTS = 32
TOP_K = 4
D_FF = 768
SWIGLU_ALPHA = 1.702
SWIGLU_LIMIT = 7.0
MOE_BLOCK = 128
DN_ALPHA = (2 * DEPTH) ** 0.25
DN_BETA = (8 * DEPTH) ** -0.25
LN_EPS = 1e-5

kernel_name = 'hybrid_gated_moba_ssd_dsa_mla_moe'


def layer_norm(x, g, b):
    xf = x.astype(jnp.float32)
    mu = jnp.mean(xf, axis=-1, keepdims=True)
    var = jnp.mean(jnp.square(xf - mu), axis=-1, keepdims=True)
    return ((xf - mu) * lax.rsqrt(var + LN_EPS) * g.astype(jnp.float32) + b.astype(jnp.float32)).astype(x.dtype)


def rms_norm(x, w):
    xf = x.astype(jnp.float32)
    y = xf * lax.rsqrt(jnp.mean(jnp.square(xf), axis=-1, keepdims=True) + LN_EPS)
    return (y * w.astype(jnp.float32)).astype(x.dtype)


def rope_cos_sin(positions, dim, theta):
    inv = theta ** (-jnp.arange(0, dim, 2, dtype=jnp.float32) / dim)
    ang = positions.astype(jnp.float32)[..., None] * inv
    return jnp.cos(ang), jnp.sin(ang)


def rotate(x, cos, sin):
    half = x.shape[-1] // 2
    x1 = x[..., :half].astype(jnp.float32)
    x2 = x[..., half:].astype(jnp.float32)
    c = cos[:, :, None, :]
    s = sin[:, :, None, :]
    return jnp.concatenate([x1 * c - x2 * s, x2 * c + x1 * s], axis=-1).astype(x.dtype)


def partial_rope(x, cos, sin):
    return jnp.concatenate([rotate(x[..., :ROT_DIM], cos, sin), x[..., ROT_DIM:]], axis=-1)


def moba_attention(q, k, v):
    B, S, H, dh = q.shape
    nb = -(-S // MOBA_BLOCK)
    pad = nb * MOBA_BLOCK - S
    kb = jnp.pad(k, ((0, 0), (0, pad), (0, 0), (0, 0))).reshape(B, nb, MOBA_BLOCK, H, dh).transpose(0, 3, 1, 2, 4)
    vb = jnp.pad(v, ((0, 0), (0, pad), (0, 0), (0, 0))).reshape(B, nb, MOBA_BLOCK, H, dh).transpose(0, 3, 1, 2, 4)
    k_mean = jnp.mean(kb, axis=3)
    topk = min(MOBA_TOPK, nb)
    nq = S // Q_BLOCK
    qb = q.reshape(B, nq, Q_BLOCK, H, dh).transpose(1, 0, 3, 2, 4)
    scale = dh ** -0.5
    b_ix = jnp.arange(B)[:, None, None, None]
    h_ix = jnp.arange(H)[None, :, None, None]
    blk_ids = jnp.arange(nb)
    key_off = jnp.arange(MOBA_BLOCK)

    def one_block(args):
        q_blk, qi = args
        q_pos = qi * Q_BLOCK + jnp.arange(Q_BLOCK)
        own = (qi * Q_BLOCK) // MOBA_BLOCK
        gate = jnp.einsum('bhqd,bhnd->bhqn', q_blk, k_mean).astype(jnp.float32)
        gate = jnp.where(blk_ids < own, gate, -jnp.inf)
        _, sel = lax.top_k(gate, topk)
        sel_ok = sel < own
        k_sel = kb[b_ix, h_ix, sel]
        v_sel = vb[b_ix, h_ix, sel]
        s_sel = jnp.einsum('bhqd,bhqjkd->bhqjk', q_blk, k_sel).astype(jnp.float32) * scale
        s_sel = jnp.where(sel_ok[..., None], s_sel, -jnp.inf).reshape(B, H, Q_BLOCK, topk * MOBA_BLOCK)
        k_own = lax.dynamic_index_in_dim(kb, own, axis=2, keepdims=False)
        v_own = lax.dynamic_index_in_dim(vb, own, axis=2, keepdims=False)
        s_own = jnp.einsum('bhqd,bhkd->bhqk', q_blk, k_own).astype(jnp.float32) * scale
        causal = (own * MOBA_BLOCK + key_off)[None, :] <= q_pos[:, None]
        s_own = jnp.where(causal, s_own, -jnp.inf)
        p = jax.nn.softmax(jnp.concatenate([s_sel, s_own], axis=-1), axis=-1).astype(v.dtype)
        p_sel = p[..., :topk * MOBA_BLOCK].reshape(B, H, Q_BLOCK, topk, MOBA_BLOCK)
        p_own = p[..., topk * MOBA_BLOCK:]
        return (jnp.einsum('bhqjk,bhqjkd->bhqd', p_sel, v_sel)
                + jnp.einsum('bhqk,bhkd->bhqd', p_own, v_own))

    out = lax.map(one_block, (qb, jnp.arange(nq)))
    return out.transpose(1, 0, 3, 2, 4).reshape(B, S, H * dh)


def dsa_attention(q, k, v, q_idx, k_idx, w_idx):
    B, S, H, dh = q.shape
    topk = min(DSA_TOPK, S // 4)
    nq = S // Q_BLOCK
    qb = q.reshape(B, nq, Q_BLOCK, H, dh).transpose(1, 0, 2, 3, 4)
    qib = q_idx.reshape(B, nq, Q_BLOCK, IDX_HEADS, IDX_DIM).transpose(1, 0, 2, 3, 4)
    wb = w_idx.reshape(B, nq, Q_BLOCK, IDX_HEADS).transpose(1, 0, 2, 3)
    key_pos = jnp.arange(S)
    b_ix = jnp.arange(B)[:, None, None]
    scale = dh ** -0.5

    def one_block(args):
        q_blk, qi_blk, w_blk, qi = args
        q_pos = qi * Q_BLOCK + jnp.arange(Q_BLOCK)
        rel = jax.nn.relu(jnp.einsum('bqhd,bsd->bqhs', qi_blk, k_idx).astype(jnp.float32))
        score = jnp.einsum('bqh,bqhs->bqs', w_blk.astype(jnp.float32), rel)
        score = jnp.where(key_pos[None, :] <= q_pos[:, None], score, -jnp.inf)
        _, sel = lax.top_k(score, topk)
        sel_ok = sel <= q_pos[None, :, None]
        k_sel = k[b_ix, sel]
        v_sel = v[b_ix, sel]
        s = jnp.einsum('bqhd,bqkhd->bhqk', q_blk, k_sel).astype(jnp.float32) * scale
        s = jnp.where(sel_ok[:, None], s, -jnp.inf)
        p = jax.nn.softmax(s, axis=-1).astype(v.dtype)
        return jnp.einsum('bhqk,bqkhd->bqhd', p, v_sel)

    out = lax.map(one_block, (qb, qib, wb, jnp.arange(nq)))
    return out.transpose(1, 0, 2, 3, 4).reshape(B, S, H * dh)


def causal_attention_blocked(q, k, v):
    B, S, H, dqk = q.shape
    dv = v.shape[-1]
    nq = S // Q_BLOCK
    qb = q.reshape(B, nq, Q_BLOCK, H, dqk).transpose(1, 0, 2, 3, 4)
    key_pos = jnp.arange(S)
    scale = dqk ** -0.5

    def one_block(args):
        q_blk, qi = args
        q_pos = qi * Q_BLOCK + jnp.arange(Q_BLOCK)
        s = jnp.einsum('bqhd,bkhd->bhqk', q_blk, k).astype(jnp.float32) * scale
        s = jnp.where(key_pos[None, :] <= q_pos[:, None], s, -jnp.inf)
        p = jax.nn.softmax(s, axis=-1).astype(v.dtype)
        return jnp.einsum('bhqk,bkhd->bqhd', p, v)

    out = lax.map(one_block, (qb, jnp.arange(nq)))
    return out.transpose(1, 0, 2, 3, 4).reshape(B, S, H * dv)


def mla_attention(c_q, c_kv, k_pe_raw, cos_m, sin_m, q_norm_w, w_uq, kv_norm_w, w_ukv):
    B, S, _ = c_q.shape
    q = jnp.einsum('bsr,re->bse', rms_norm(c_q, q_norm_w), w_uq).reshape(B, S, MLA_HEADS, MLA_NOPE + MLA_ROPE)
    q = jnp.concatenate([q[..., :MLA_NOPE], rotate(q[..., MLA_NOPE:], cos_m, sin_m)], axis=-1)
    kv = jnp.einsum('bsr,re->bse', rms_norm(c_kv, kv_norm_w), w_ukv).reshape(B, S, MLA_HEADS, MLA_NOPE + MLA_V)
    k_pe = rotate(k_pe_raw[:, :, None, :], cos_m, sin_m)
    k = jnp.concatenate([kv[..., :MLA_NOPE], jnp.broadcast_to(k_pe, (B, S, MLA_HEADS, MLA_ROPE))], axis=-1)
    v = kv[..., MLA_NOPE:]
    return causal_attention_blocked(q, k, v)


def ssd_chunked(xdt, da, bm, cm):
    B, S, H, P = xdt.shape
    N = bm.shape[-1]
    nc = -(-S // SSM_CHUNK)
    pad = nc * SSM_CHUNK - S

    def chunked(t):
        t = jnp.pad(t, [(0, 0), (0, pad)] + [(0, 0)] * (t.ndim - 2))
        return t.reshape((B, nc, SSM_CHUNK) + t.shape[2:])

    xc = chunked(xdt)
    bc = chunked(bm)
    cc = chunked(cm)
    ac = chunked(da).transpose(0, 1, 3, 2)
    a_cum = jnp.cumsum(ac, axis=-1)
    tri = jnp.tril(jnp.ones((SSM_CHUNK, SSM_CHUNK), dtype=bool))
    seg = jnp.exp(jnp.where(tri, a_cum[..., :, None] - a_cum[..., None, :], -jnp.inf))
    scores = jnp.einsum('bclhn,bcshn->bchls', cc, bc) * seg
    y_diag = jnp.einsum('bchls,bcshp->bclhp', scores, xc)
    decay_to_end = jnp.exp(a_cum[..., -1:] - a_cum).transpose(0, 1, 3, 2)
    states = jnp.einsum('bclhn,bclhp->bchpn', bc * decay_to_end[..., None], xc)
    chunk_decay = jnp.exp(a_cum[..., -1])

    def carry_state(h, inp):
        st, dec = inp
        return h * dec[..., None, None] + st, h

    h0 = jnp.zeros((B, H, P, N), xdt.dtype)
    _, h_in = lax.scan(carry_state, h0, (jnp.moveaxis(states, 1, 0), jnp.moveaxis(chunk_decay, 1, 0)))
    h_in = jnp.moveaxis(h_in, 0, 1)
    decay_from_start = jnp.exp(a_cum).transpose(0, 1, 3, 2)
    y_off = jnp.einsum('bclhn,bchpn->bclhp', cc, h_in) * decay_from_start[..., None]
    return (y_diag + y_off).reshape(B, nc * SSM_CHUNK, H, P)[:, :S]


def mamba2_ssd(z, xbc, dt_raw, conv_w, conv_b, dt_bias, a_log, d_skip, norm_w):
    B, S, _ = xbc.shape
    conv = lax.conv_general_dilated(xbc, conv_w[:, None, :], window_strides=(1,),
                                    padding=[(SSM_CONV - 1, 0)],
                                    dimension_numbers=('NWC', 'WIO', 'NWC'),
                                    feature_group_count=SSM_CONV_CH) + conv_b
    xbc = jax.nn.silu(conv)
    xs, bm, cm = jnp.split(xbc, [SSM_INNER, SSM_INNER + SSM_GROUPS * SSM_STATE], axis=-1)
    xs = xs.reshape(B, S, SSM_HEADS, SSM_HEAD_DIM).astype(jnp.float32)
    rep = SSM_HEADS // SSM_GROUPS
    bm = jnp.repeat(bm.reshape(B, S, SSM_GROUPS, SSM_STATE), rep, axis=2).astype(jnp.float32)
    cm = jnp.repeat(cm.reshape(B, S, SSM_GROUPS, SSM_STATE), rep, axis=2).astype(jnp.float32)
    dt = jax.nn.softplus((dt_raw + dt_bias).astype(jnp.float32))
    a = -jnp.exp(a_log.astype(jnp.float32))
    y = ssd_chunked(xs * dt[..., None], dt * a, bm, cm)
    y = y + xs * d_skip.astype(jnp.float32)[:, None]
    y = y.reshape(B, S, SSM_INNER) * jax.nn.silu(z.astype(jnp.float32))
    return rms_norm(y, norm_w).astype(z.dtype)


def token_mixer(x, cos_p, sin_p, cos_m, sin_m, w_in, b_gate, conv_w, conv_b, dt_bias, a_log,
                d_skip, ssm_norm_w, q_norm_w, w_uq, kv_norm_w, w_ukv, w_branch, w_out):
    B, S, _ = x.shape
    proj = jnp.einsum('bsd,de->bse', x, w_in)
    points = np.cumsum(SPLIT_SIZES)[:-1].tolist()
    (a_q, a_k, a_v, m_z, m_xbc, m_dt, c_q, c_k, c_v, c_qi, c_ki, c_wi,
     d_cq, d_ckv, d_kpe, g) = jnp.split(proj, points, axis=-1)

    def heads(t, h):
        return t.reshape(B, S, h, -1)

    o_a = moba_attention(partial_rope(heads(a_q, MOBA_HEADS), cos_p, sin_p),
                         partial_rope(heads(a_k, MOBA_HEADS), cos_p, sin_p),
                         heads(a_v, MOBA_HEADS))
    o_b = mamba2_ssd(m_z, m_xbc, m_dt, conv_w, conv_b, dt_bias, a_log, d_skip, ssm_norm_w)
    o_c = dsa_attention(partial_rope(heads(c_q, DSA_HEADS), cos_p, sin_p),
                        partial_rope(heads(c_k, DSA_HEADS), cos_p, sin_p),
                        heads(c_v, DSA_HEADS),
                        partial_rope(heads(c_qi, IDX_HEADS), cos_p, sin_p),
                        partial_rope(c_ki[:, :, None, :], cos_p, sin_p)[:, :, 0],
                        c_wi * IDX_SCALE)
    o_d = mla_attention(d_cq, d_ckv, d_kpe, cos_m, sin_m, q_norm_w, w_uq, kv_norm_w, w_ukv)

    branches = jnp.stack([o_a, o_b, o_c, o_d], axis=2)
    y = jnp.einsum('bsgw,gwd->bsgd', branches, w_branch)
    gates = jax.nn.sigmoid((g.reshape(B, S, N_BRANCH, D_MODEL) + b_gate).astype(jnp.float32)).astype(y.dtype)
    merged = jnp.sum(gates * y, axis=2)
    return jnp.einsum('bsd,de->bse', merged, w_out)


def moe_ffn(h, router_w, router_b, w_gate_up, b_gate_up, w_down, b_down):
    B, S, D = h.shape
    T = B * S
    TK = T * TOP_K
    xt = h.reshape(T, D)
    logits = (jnp.einsum('td,de->te', xt, router_w) + router_b).astype(jnp.float32)
    top_vals, top_idx = lax.top_k(logits, TOP_K)
    gate = jax.nn.softmax(top_vals, axis=-1)
    flat_e = top_idx.reshape(-1)
    counts = jnp.bincount(flat_e, length=N_EXPERTS)
    padded = ((counts + MOE_BLOCK - 1) // MOE_BLOCK) * MOE_BLOCK
    pad_end = jnp.cumsum(padded)
    pad_start = pad_end - padded
    start = jnp.cumsum(counts) - counts
    order = jnp.argsort(flat_e)
    sorted_e = flat_e[order]
    rank = jnp.arange(TK) - start[sorted_e]
    dest = pad_start[sorted_e] + rank
    tok = order // TOP_K
    gate_sorted = gate.reshape(-1)[order]
    n_blocks = -(-TK // MOE_BLOCK) + N_EXPERTS
    n_rows = n_blocks * MOE_BLOCK
    buf_x = jnp.zeros((n_rows, D), h.dtype).at[dest].set(xt[tok])
    block_start = jnp.arange(n_blocks) * MOE_BLOCK
    block_e = jnp.minimum(jnp.sum(pad_end[None, :] <= block_start[:, None], axis=1), N_EXPERTS - 1)

    def expert_block(args):
        x_blk, e = args
        gu = x_blk @ w_gate_up[e] + b_gate_up[e]
        glu = jnp.minimum(gu[:, :D_FF], SWIGLU_LIMIT)
        lin = jnp.clip(gu[:, D_FF:], -SWIGLU_LIMIT, SWIGLU_LIMIT)
        act = glu * jax.nn.sigmoid(SWIGLU_ALPHA * glu) * (lin + 1)
        return act @ w_down[e] + b_down[e]

    out = lax.map(expert_block, (buf_x.reshape(n_blocks, MOE_BLOCK, D), block_e)).reshape(n_rows, D)
    contrib = out[dest] * gate_sorted[:, None].astype(out.dtype)
    y = jax.ops.segment_sum(contrib, tok, num_segments=T)
    return y.reshape(B, S, D).astype(h.dtype)


def setup_inputs(seed: int = 0) -> dict:
    key = jax.random.key(seed)
    ks = jax.random.split(key, 32)
    f32 = jnp.float32
    L = DEPTH

    def nrm(k, shape, scale):
        return jax.random.normal(k, shape, f32) * scale

    x = jax.random.normal(ks[0], (BATCH, SEQ, D_MODEL), f32)
    positions = jnp.broadcast_to(jnp.arange(SEQ, dtype=jnp.int32), (BATCH, SEQ))
    w_in = nrm(ks[1], (L, D_MODEL, D_IN), D_MODEL ** -0.5)
    b_gate = nrm(ks[2], (L, N_BRANCH, D_MODEL), 0.02)
    conv_w = nrm(ks[3], (L, SSM_CONV, SSM_CONV_CH), SSM_CONV ** -0.5)
    conv_b = nrm(ks[4], (L, SSM_CONV_CH), 0.02)
    dt_init = jnp.exp(jax.random.uniform(ks[5], (L, SSM_HEADS), f32, math.log(1e-3), math.log(1e-1)))
    dt_bias = dt_init + jnp.log(-jnp.expm1(-dt_init))
    a_log = jnp.log(jax.random.uniform(ks[6], (L, SSM_HEADS), f32, 1.0, 16.0))
    d_skip = 1.0 + nrm(ks[7], (L, SSM_HEADS), 0.1)
    ssm_norm_w = 1.0 + nrm(ks[8], (L, SSM_INNER), 0.02)
    q_norm_w = 1.0 + nrm(ks[9], (L, MLA_Q_RANK), 0.02)
    w_uq = nrm(ks[10], (L, MLA_Q_RANK, MLA_HEADS * (MLA_NOPE + MLA_ROPE)), MLA_Q_RANK ** -0.5)
    kv_norm_w = 1.0 + nrm(ks[11], (L, MLA_KV_RANK), 0.02)
    w_ukv = nrm(ks[12], (L, MLA_KV_RANK, MLA_HEADS * (MLA_NOPE + MLA_V)), MLA_KV_RANK ** -0.5)
    w_branch = nrm(ks[13], (L, N_BRANCH, BRANCH_W, D_MODEL), BRANCH_W ** -0.5)
    w_out = nrm(ks[14], (L, D_MODEL, D_MODEL), D_MODEL ** -0.5 * DN_BETA)
    ln1_g = 1.0 + nrm(ks[15], (L, D_MODEL), 0.02)
    ln1_b = nrm(ks[16], (L, D_MODEL), 0.02)
    router_w = nrm(ks[17], (L, D_MODEL, N_EXPERTS), D_MODEL ** -0.5)
    router_b = nrm(ks[18], (L, N_EXPERTS), 0.01)
    w_gate_up = nrm(ks[19], (L, N_EXPERTS, D_MODEL, 2 * D_FF), D_MODEL ** -0.5)
    b_gate_up = nrm(ks[20], (L, N_EXPERTS, 2 * D_FF), 0.02)
    w_down = nrm(ks[21], (L, N_EXPERTS, D_FF, D_MODEL), D_FF ** -0.5 * DN_BETA)
    b_down = nrm(ks[22], (L, N_EXPERTS, D_MODEL), 0.02)
    ln2_g = 1.0 + nrm(ks[23], (L, D_MODEL), 0.02)
    ln2_b = nrm(ks[24], (L, D_MODEL), 0.02)
    return {'x': x, 'positions': positions, 'w_in': w_in, 'b_gate': b_gate,
            'conv_w': conv_w, 'conv_b': conv_b, 'dt_bias': dt_bias, 'a_log': a_log,
            'd_skip': d_skip, 'ssm_norm_w': ssm_norm_w, 'q_norm_w': q_norm_w, 'w_uq': w_uq,
            'kv_norm_w': kv_norm_w, 'w_ukv': w_ukv, 'w_branch': w_branch, 'w_out': w_out,
            'ln1_g': ln1_g, 'ln1_b': ln1_b, 'router_w': router_w, 'router_b': router_b,
            'w_gate_up': w_gate_up, 'b_gate_up': b_gate_up, 'w_down': w_down, 'b_down': b_down,
            'ln2_g': ln2_g, 'ln2_b': ln2_b}


def reference(x, positions, w_in, b_gate, conv_w, conv_b, dt_bias, a_log, d_skip, ssm_norm_w,
              q_norm_w, w_uq, kv_norm_w, w_ukv, w_branch, w_out, ln1_g, ln1_b, router_w, router_b,
              w_gate_up, b_gate_up, w_down, b_down, ln2_g, ln2_b):
    cos_p, sin_p = rope_cos_sin(positions, ROT_DIM, ROPE_THETA)
    cos_m, sin_m = rope_cos_sin(positions, MLA_ROPE, MLA_THETA)
    for l in range(DEPTH):
        mixed = token_mixer(x, cos_p, sin_p, cos_m, sin_m, w_in[l], b_gate[l], conv_w[l], conv_b[l],
                            dt_bias[l], a_log[l], d_skip[l], ssm_norm_w[l], q_norm_w[l], w_uq[l],
                            kv_norm_w[l], w_ukv[l], w_branch[l], w_out[l])
        x = layer_norm(DN_ALPHA * x + mixed, ln1_g[l], ln1_b[l])
        ffn = moe_ffn(x, router_w[l], router_b[l], w_gate_up[l], b_gate_up[l], w_down[l], b_down[l])
        x = layer_norm(DN_ALPHA * x + ffn, ln2_g[l], ln2_b[l])
    return x
```

```python
import functools
import math

import jax
import jax.numpy as jnp
import numpy as np
from jax import lax
from jax.experimental import pallas as pl
from jax.experimental.pallas import tpu as pltpu

F32 = jnp.float32
BF16 = jnp.bfloat16
I32 = jnp.int32

D_MODEL = 2048
DEPTH = 4
HEAD_DIM = 64
ROT_DIM = HEAD_DIM // 4
ROPE_THETA = 500000.0
N_HEADS = 8
ATT_W = N_HEADS * HEAD_DIM
MOBA_BLOCK = 256
MOBA_TOPK = 3
SSM_HEADS = 8
SSM_HEAD_DIM = 64
SSM_INNER = 512
SSM_STATE = 128
SSM_GROUPS = 2
SSM_CONV = 4
SSM_CONV_CH = 1024
SSM_CHUNK = 256
IDX_HEADS = 4
IDX_DIM = 64
DSA_TOPK = 256
IDX_SCALE = (IDX_HEADS * IDX_DIM) ** -0.5
MLA_Q_RANK = 384
MLA_KV_RANK = 256
MLA_NOPE = 64
MLA_ROPE = 32
MLA_V = 64
MLA_THETA = 10000.0
N_BRANCH = 4
BRANCH_W = 512
SPLIT_SIZES = (512, 512, 512, 512, 1024, 8, 512, 512, 512, 256, 64, 4, 384, 256, 32, N_BRANCH * D_MODEL)
N_EXPERTS = 32
TOP_K = 4
D_FF = 768
SWIGLU_ALPHA = 1.702
SWIGLU_LIMIT = 7.0
DN_ALPHA = (2 * DEPTH) ** 0.25
LN_EPS = 1e-5

LANES = 128
NEG = -1e30
VMEM_LIMIT = 56 * 1024 * 1024

COL_AQ, COL_AK, COL_AV = 0, 512, 1024
COL_CQ, COL_CK, COL_CV = 1536, 2048, 2560
COL_CQI, COL_KI4 = 3072, 3328
COL_MZ = 3584
COL_XBC = 4096
COL_CKV = 5120
COL_CQL = 5376
COL_MISC = 5760
N_SMALL = 6144
N_CAST = 3584
MISC_DT = 32
MISC_WI = 40
MOE_ROWS = 256


def _cparams(sem, vmem=None):
    return pltpu.CompilerParams(dimension_semantics=sem, vmem_limit_bytes=vmem)


def _dot(a, b):
    return jnp.dot(a, b, preferred_element_type=F32)


def _dot_nt(a, b):
    return lax.dot_general(a, b, (((1,), (1,)), ((), ())), preferred_element_type=F32)


def _dot_hi(a, b):
    return jnp.dot(a, b, preferred_element_type=F32, precision=lax.Precision.HIGHEST)


def _mm_kernel(x_ref, w_ref, o_ref):
    o_ref[...] = _dot(x_ref[...], w_ref[...])


def matmul_bf16(x, w, tm=512, tn=512):
    m, k = x.shape
    n = w.shape[1]
    return pl.pallas_call(
        _mm_kernel,
        grid=(n // tn, m // tm),
        in_specs=[pl.BlockSpec((tm, k), lambda j, i: (i, 0)),
                  pl.BlockSpec((k, tn), lambda j, i: (0, j))],
        out_specs=pl.BlockSpec((tm, tn), lambda j, i: (i, j)),
        out_shape=jax.ShapeDtypeStruct((m, n), F32),
        compiler_params=_cparams(("arbitrary", "arbitrary"), VMEM_LIMIT),
        name="in_proj",
    )(x, w)


def _rot_apply(x, c, s1, s2, shift):
    return x * c + pltpu.roll(x, LANES - shift, 1) * s1 + pltpu.roll(x, shift, 1) * s2


def _rope_cast_kernel(flag_ref, x_ref, c_ref, s1_ref, s2_ref, o_ref):
    j = pl.program_id(0)
    width = x_ref.shape[1]

    @pl.when(flag_ref[j] == 1)
    def _():
        c, s1, s2 = c_ref[...], s1_ref[...], s2_ref[...]
        for g in range(width // LANES):
            sl = slice(g * LANES, (g + 1) * LANES)
            o_ref[:, sl] = _rot_apply(x_ref[:, sl], c, s1, s2, ROT_DIM // 2).astype(BF16)

    @pl.when(flag_ref[j] == 0)
    def _():
        o_ref[...] = x_ref[...].astype(BF16)


def rope_cast(proj, flags, tabs, tm=512, tw=512):
    t = proj.shape[0]
    grid_spec = pltpu.PrefetchScalarGridSpec(
        num_scalar_prefetch=1,
        grid=(N_CAST // tw, t // tm),
        in_specs=[pl.BlockSpec((tm, tw), lambda j, i, f: (i, j)),
                  pl.BlockSpec((tm, LANES), lambda j, i, f: (i, 0)),
                  pl.BlockSpec((tm, LANES), lambda j, i, f: (i, 0)),
                  pl.BlockSpec((tm, LANES), lambda j, i, f: (i, 0))],
        out_specs=pl.BlockSpec((tm, tw), lambda j, i, f: (i, j)),
    )
    return pl.pallas_call(
        _rope_cast_kernel,
        grid_spec=grid_spec,
        out_shape=jax.ShapeDtypeStruct((t, N_CAST), BF16),
        compiler_params=_cparams(("arbitrary", "arbitrary")),
        name="rope_cast",
    )(flags, proj, *tabs)


def _head_masks():
    lane = lax.broadcasted_iota(I32, (1, LANES), 1)
    return lane < HEAD_DIM, lane >= HEAD_DIM


def _softmax_step(s, vb, state):
    m, l, acc = state
    m_new = jnp.maximum(m, jnp.max(s, axis=1, keepdims=True))
    alpha = jnp.exp(m - m_new)
    p = jnp.exp(s - m_new)
    l_new = alpha * l + jnp.sum(p, axis=1, keepdims=True)
    acc_new = alpha * acc + _dot(p.astype(BF16), vb)
    return m_new, l_new, acc_new


def _init_state(tq):
    return (jnp.full((tq, 1), NEG, F32), jnp.zeros((tq, 1), F32), jnp.zeros((tq, LANES), F32))


def _finish_pair(st0, st1):
    lo, _ = _head_masks()
    o0 = st0[2] / st0[1]
    o1 = st1[2] / st1[1]
    return jnp.where(lo, o0, o1)


def _top_n_mask(g, valid, n):
    lane = lax.broadcasted_iota(I32, g.shape, 1)
    g = jnp.where(valid, g, -jnp.inf)
    sel = jnp.zeros(g.shape, jnp.bool_)
    for _ in range(n):
        m = jnp.max(g, axis=1, keepdims=True)
        idx = jnp.min(jnp.where(g == m, lane, g.shape[1]), axis=1, keepdims=True)
        pick = lane == idx
        sel = jnp.logical_or(sel, pick)
        g = jnp.where(pick, -jnp.inf, g)
    return jnp.logical_and(sel, valid)


def _moba_kernel(q_ref, k_ref, v_ref, o_ref, kmean_ref):
    i = pl.program_id(1)
    tq = q_ref.shape[0]
    nb = k_ref.shape[0] // MOBA_BLOCK

    @pl.when(i == 0)
    def _():
        kmean_ref[...] = jnp.zeros(kmean_ref.shape, F32)

        def mean_body(n, c):
            kb = k_ref[pl.ds(pl.multiple_of(n * MOBA_BLOCK, MOBA_BLOCK), MOBA_BLOCK), :].astype(F32)
            kmean_ref[pl.ds(n, 1), :] = jnp.sum(kb, axis=0, keepdims=True) * (1.0 / MOBA_BLOCK)
            return c

        lax.fori_loop(0, nb, mean_body, 0)

    lo, hi = _head_masks()
    q2 = q_ref[...]
    zero = jnp.zeros_like(q2)
    qh = (jnp.where(lo, q2, zero), jnp.where(hi, q2, zero))
    kmean = kmean_ref[...].astype(BF16)
    lane_n = lax.broadcasted_iota(I32, (tq, LANES), 1)
    valid = lane_n < i
    selbias = []
    for h in range(2):
        gate = _dot_nt(qh[h], kmean)
        sel = _top_n_mask(gate, valid, MOBA_TOPK)
        selbias.append(jnp.where(sel, 0.0, NEG).astype(BF16))

    own0 = pl.multiple_of(i * MOBA_BLOCK, MOBA_BLOCK)
    kb = k_ref[pl.ds(own0, MOBA_BLOCK), :]
    vb = v_ref[pl.ds(own0, MOBA_BLOCK), :]
    row = lax.broadcasted_iota(I32, (tq, MOBA_BLOCK), 0)
    col = lax.broadcasted_iota(I32, (tq, MOBA_BLOCK), 1)
    causal = col <= row
    states = []
    for h in range(2):
        s = jnp.where(causal, _dot_nt(qh[h], kb), NEG)
        states.append(_softmax_step(s, vb, _init_state(tq)))

    oh_row = lax.broadcasted_iota(I32, (LANES, LANES), 0)

    def past_body(n, carry):
        st = (carry[0:3], carry[3:6])
        off = pl.multiple_of(n * MOBA_BLOCK, MOBA_BLOCK)
        kb = k_ref[pl.ds(off, MOBA_BLOCK), :]
        vb = v_ref[pl.ds(off, MOBA_BLOCK), :]
        onehot = jnp.where(oh_row == n, 1.0, 0.0).astype(BF16)
        out = []
        for h in range(2):
            bias = _dot(selbias[h], onehot)
            s = _dot_nt(qh[h], kb) + jnp.concatenate([bias, bias], axis=1)
            out.extend(_softmax_step(s, vb, st[h]))
        return tuple(out)

    fin = lax.fori_loop(0, i, past_body, tuple(states[0]) + tuple(states[1]))
    o_ref[...] = _finish_pair(fin[0:3], fin[3:6]).astype(BF16)


def moba_attention(cast, tq=MOBA_BLOCK):
    t = cast.shape[0]
    nhp = N_HEADS // 2
    return pl.pallas_call(
        _moba_kernel,
        grid=(nhp, t // tq),
        in_specs=[pl.BlockSpec((tq, LANES), lambda hp, i: (i, COL_AQ // LANES + hp)),
                  pl.BlockSpec((t, LANES), lambda hp, i: (0, COL_AK // LANES + hp)),
                  pl.BlockSpec((t, LANES), lambda hp, i: (0, COL_AV // LANES + hp))],
        out_specs=pl.BlockSpec((tq, LANES), lambda hp, i: (i, hp)),
        out_shape=jax.ShapeDtypeStruct((t, ATT_W), BF16),
        scratch_shapes=[pltpu.VMEM((LANES, LANES), F32)],
        compiler_params=_cparams(("arbitrary", "arbitrary"), VMEM_LIMIT),
        name="moba_attention",
    )(cast, cast, cast)


DSA_TQ = 128
DSA_TK = 512


def _float_key(x):
    b = lax.bitcast_convert_type(x, I32)
    return jnp.where(b < 0, b ^ jnp.int32(0x7FFFFFFF), b)


def _dsa_select_kernel(qi_ref, ki_ref, misc_ref, bias_ref, key_ref):
    i = pl.program_id(0)
    tq = qi_ref.shape[0]
    t = ki_ref.shape[0]
    n_chunks = t // DSA_TK
    n_live = (i * tq + tq + DSA_TK - 1) // DSA_TK

    qi = qi_ref[...]
    lane = lax.broadcasted_iota(I32, (1, IDX_HEADS * IDX_DIM), 1)
    zero = jnp.zeros_like(qi)
    qh = [jnp.where((lane >= h * IDX_DIM) & (lane < (h + 1) * IDX_DIM), qi, zero) for h in range(IDX_HEADS)]
    misc = misc_ref[...]
    wcol = [misc[:, MISC_WI + h:MISC_WI + h + 1] for h in range(IDX_HEADS)]
    qpos = i * tq + lax.broadcasted_iota(I32, (tq, DSA_TK), 0)
    kcol = lax.broadcasted_iota(I32, (tq, DSA_TK), 1)

    def score_body(j, c):
        off = pl.multiple_of(j * DSA_TK, DSA_TK)
        kic = ki_ref[pl.ds(off, DSA_TK), :]
        acc = jnp.zeros((tq, DSA_TK), F32)
        for h in range(IDX_HEADS):
            acc = acc + wcol[h] * jnp.maximum(_dot_nt(qh[h], kic), 0.0)
        acc = jnp.where(off + kcol <= qpos, acc, -jnp.inf)
        key_ref[j] = _float_key(acc)
        return c

    lax.fori_loop(0, n_live, score_body, 0)

    def count_ge(cand):
        def body(j, cnt):
            for g in range(DSA_TK // LANES):
                kk = key_ref[j, :, g * LANES:(g + 1) * LANES]
                cnt = cnt + jnp.where(kk >= cand, 1.0, 0.0)
            return cnt
        cnt = lax.fori_loop(0, n_live, body, jnp.zeros((tq, LANES), F32))
        return jnp.sum(cnt, axis=1, keepdims=True)

    topk = float(DSA_TOPK)
    int_min = jnp.int32(-2 ** 31)
    nonneg = count_ge(jnp.zeros((tq, 1), I32)) >= topk
    prefix0 = jnp.where(nonneg, jnp.int32(0), int_min)

    def bit_body(b, prefix):
        cand = prefix | (jnp.int32(1) << (30 - b))
        return jnp.where(count_ge(cand) >= topk, cand, prefix)

    thr = lax.fori_loop(0, 31, bit_body, prefix0)

    need = topk - count_ge(thr + 1)
    ur = lax.broadcasted_iota(I32, (DSA_TK, DSA_TK), 0)
    uc = lax.broadcasted_iota(I32, (DSA_TK, DSA_TK), 1)
    upper = jnp.where(ur <= uc, 1.0, 0.0).astype(BF16)

    def bias_body(j, seen):
        off = j * DSA_TK
        kk = key_ref[j]
        eq = kk == thr
        rank = _dot(jnp.where(eq, 1.0, 0.0).astype(BF16), upper) + seen
        keep = ((kk > thr) | (eq & (rank <= need))) & (off + kcol <= qpos)
        bias_ref[j] = jnp.where(keep, 0.0, NEG).astype(BF16)
        return rank[:, DSA_TK - 1:DSA_TK]

    lax.fori_loop(0, n_live, bias_body, jnp.zeros((tq, 1), F32))

    def fill_body(j, c):
        bias_ref[j] = jnp.full((tq, DSA_TK), NEG, BF16)
        return c

    lax.fori_loop(n_live, n_chunks, fill_body, 0)


def dsa_select(cast, proj):
    t = cast.shape[0]
    w = IDX_HEADS * IDX_DIM
    return pl.pallas_call(
        _dsa_select_kernel,
        grid=(t // DSA_TQ,),
        in_specs=[pl.BlockSpec((DSA_TQ, w), lambda i: (i, COL_CQI // w)),
                  pl.BlockSpec((t, w), lambda i: (0, COL_KI4 // w)),
                  pl.BlockSpec((DSA_TQ, LANES), lambda i: (i, COL_MISC // LANES))],
        out_specs=pl.BlockSpec((t // DSA_TK, DSA_TQ, DSA_TK), lambda i: (0, i, 0)),
        out_shape=jax.ShapeDtypeStruct((t // DSA_TK, t, DSA_TK), BF16),
        scratch_shapes=[pltpu.VMEM((t // DSA_TK, DSA_TQ, DSA_TK), I32)],
        compiler_params=_cparams(("arbitrary",), VMEM_LIMIT),
        name="dsa_select",
    )(cast, cast, proj)


ATT_TQ = 256
ATT_TK = 512


def _dsa_attn_kernel(q_ref, k_ref, v_ref, bias_ref, o_ref):
    i = pl.program_id(1)
    tq = q_ref.shape[0]
    n_live = (i * tq + tq + ATT_TK - 1) // ATT_TK
    lo, hi = _head_masks()
    q2 = q_ref[...]
    zero = jnp.zeros_like(q2)
    qh = (jnp.where(lo, q2, zero), jnp.where(hi, q2, zero))

    def body(j, carry):
        st = (carry[0:3], carry[3:6])
        off = pl.multiple_of(j * ATT_TK, ATT_TK)
        kb = k_ref[pl.ds(off, ATT_TK), :]
        vb = v_ref[pl.ds(off, ATT_TK), :]
        bias = bias_ref[j].astype(F32)
        out = []
        for h in range(2):
            out.extend(_softmax_step(_dot_nt(qh[h], kb) + bias, vb, st[h]))
        return tuple(out)

    init = _init_state(tq)
    fin = lax.fori_loop(0, n_live, body, tuple(init) + tuple(init))
    o_ref[...] = _finish_pair(fin[0:3], fin[3:6]).astype(BF16)


def dsa_attention(cast, bias):
    t = cast.shape[0]
    nhp = N_HEADS // 2
    return pl.pallas_call(
        _dsa_attn_kernel,
        grid=(nhp, t // ATT_TQ),
        in_specs=[pl.BlockSpec((ATT_TQ, LANES), lambda hp, i: (i, COL_CQ // LANES + hp)),
                  pl.BlockSpec((t, LANES), lambda hp, i: (0, COL_CK // LANES + hp)),
                  pl.BlockSpec((t, LANES), lambda hp, i: (0, COL_CV // LANES + hp)),
                  pl.BlockSpec((t // ATT_TK, ATT_TQ, ATT_TK), lambda hp, i: (0, i, 0))],
        out_specs=pl.BlockSpec((ATT_TQ, LANES), lambda hp, i: (i, hp)),
        out_shape=jax.ShapeDtypeStruct((t, ATT_W), BF16),
        compiler_params=_cparams(("arbitrary", "arbitrary"), VMEM_LIMIT),
        name="dsa_attention",
    )(cast, cast, cast, bias)


def _rms(x, w):
    return x * lax.rsqrt(jnp.mean(x * x, axis=-1, keepdims=True) + LN_EPS) * w


def _mla_prep_kernel(cq_ref, ckv_ref, misc_ref, qn_ref, kvn_ref, wq_ref, wk_ref, wv_ref,
                     c_ref, s1_ref, s2_ref, q_ref, k_ref, v_ref):
    c, s1, s2 = c_ref[...], s1_ref[...], s2_ref[...]
    cq = _rms(cq_ref[...], qn_ref[...]).astype(BF16)
    ckv = _rms(ckv_ref[...], kvn_ref[...]).astype(BF16)
    lane = lax.broadcasted_iota(I32, (1, LANES), 1)
    kpe = pltpu.roll(jnp.where(lane < MLA_ROPE, misc_ref[...], 0.0), MLA_NOPE, 1)
    kpe = _rot_apply(kpe, c, s1, s2, MLA_ROPE // 2)
    q = _dot(cq, wq_ref[...])
    k = _dot(ckv, wk_ref[...])
    for h in range(N_HEADS):
        sl = slice(h * LANES, (h + 1) * LANES)
        q_ref[:, sl] = _rot_apply(q[:, sl], c, s1, s2, MLA_ROPE // 2).astype(BF16)
        k_ref[:, sl] = (k[:, sl] + kpe).astype(BF16)
    v_ref[...] = _dot(ckv, wv_ref[...]).astype(BF16)


def mla_prep(proj, qn, kvn, wq, wk, wv, tabs, tm=512):
    t = proj.shape[0]
    full = lambda a: pl.BlockSpec(a.shape, lambda i: (0,) * a.ndim)
    row = lambda w, col: pl.BlockSpec((tm, w), lambda i: (i, col // w))
    return pl.pallas_call(
        _mla_prep_kernel,
        grid=(t // tm,),
        in_specs=[row(MLA_Q_RANK, COL_CQL), row(MLA_KV_RANK, COL_CKV), row(LANES, COL_MISC),
                  full(qn), full(kvn), full(wq), full(wk), full(wv),
                  row(LANES, 0), row(LANES, 0), row(LANES, 0)],
        out_specs=[pl.BlockSpec((tm, N_HEADS * LANES), lambda i: (i, 0)),
                   pl.BlockSpec((tm, N_HEADS * LANES), lambda i: (i, 0)),
                   pl.BlockSpec((tm, ATT_W), lambda i: (i, 0))],
        out_shape=[jax.ShapeDtypeStruct((t, N_HEADS * LANES), BF16),
                   jax.ShapeDtypeStruct((t, N_HEADS * LANES), BF16),
                   jax.ShapeDtypeStruct((t, ATT_W), BF16)],
        compiler_params=_cparams(("arbitrary",), VMEM_LIMIT),
        name="mla_prep",
    )(proj, proj, proj, qn, kvn, wq, wk, wv, *tabs)


def _mla_attn_kernel(q_ref, k_ref, v_ref, o_ref):
    i = pl.program_id(1)
    tq = q_ref.shape[0]
    qh = (q_ref[:, 0:LANES], q_ref[:, LANES:2 * LANES])

    def body(j, carry):
        st = (carry[0:3], carry[3:6])
        off = pl.multiple_of(j * tq, tq)
        vb = v_ref[pl.ds(off, tq), :]
        out = []
        for h in range(2):
            kb = k_ref[pl.ds(off, tq), h * LANES:(h + 1) * LANES]
            out.extend(_softmax_step(_dot_nt(qh[h], kb), vb, st[h]))
        return tuple(out)

    init = _init_state(tq)
    carry = lax.fori_loop(0, i, body, tuple(init) + tuple(init))
    off = pl.multiple_of(i * tq, tq)
    vb = v_ref[pl.ds(off, tq), :]
    row = lax.broadcasted_iota(I32, (tq, tq), 0)
    col = lax.broadcasted_iota(I32, (tq, tq), 1)
    causal = col <= row
    fin = []
    for h in range(2):
        kb = k_ref[pl.ds(off, tq), h * LANES:(h + 1) * LANES]
        s = jnp.where(causal, _dot_nt(qh[h], kb), NEG)
        fin.append(_softmax_step(s, vb, carry[3 * h:3 * h + 3]))
    o_ref[...] = _finish_pair(fin[0], fin[1]).astype(BF16)


def mla_attention(q, k, v, tq=ATT_TQ):
    t = q.shape[0]
    nhp = N_HEADS // 2
    return pl.pallas_call(
        _mla_attn_kernel,
        grid=(nhp, t // tq),
        in_specs=[pl.BlockSpec((tq, 2 * LANES), lambda hp, i: (i, hp)),
                  pl.BlockSpec((t, 2 * LANES), lambda hp, i: (0, hp)),
                  pl.BlockSpec((t, LANES), lambda hp, i: (0, hp))],
        out_specs=pl.BlockSpec((tq, LANES), lambda hp, i: (i, hp)),
        out_shape=jax.ShapeDtypeStruct((t, ATT_W), BF16),
        compiler_params=_cparams(("arbitrary", "arbitrary"), VMEM_LIMIT),
        name="mla_attention",
    )(q, k, v)


def _silu(x):
    return x / (1.0 + jnp.exp(-x))


def _softplus(x):
    return jnp.maximum(x, 0.0) + jnp.log1p(jnp.exp(-jnp.abs(x)))


def _ssd_kernel(z_ref, xbc_ref, misc_ref, cw_ref, cb_ref, dtb_ref, a_ref, dsk_ref, nw_ref, exp_ref,
                o_ref, ext_ref, state_ref):
    c = pl.program_id(0)
    L = SSM_CHUNK
    half = SSM_INNER // SSM_GROUPS

    @pl.when(c == 0)
    def _():
        ext_ref[0:8, :] = jnp.zeros((8, SSM_CONV_CH), F32)
        state_ref[...] = jnp.zeros(state_ref.shape, F32)

    ext_ref[8:8 + L, :] = xbc_ref[...]
    conv = jnp.zeros((L, SSM_CONV_CH), F32) + cb_ref[...]
    for kk in range(SSM_CONV):
        conv = conv + cw_ref[kk:kk + 1, :] * ext_ref[pl.ds(8 - (SSM_CONV - 1) + kk, L), :]
    tail = ext_ref[L:L + 8, :]
    ext_ref[0:8, :] = tail
    xbc = _silu(conv)
    xs = xbc[:, 0:SSM_INNER]
    bm = xbc[:, SSM_INNER:SSM_INNER + SSM_GROUPS * SSM_STATE]
    cm = xbc[:, SSM_INNER + SSM_GROUPS * SSM_STATE:]

    dt = _softplus(misc_ref[...] + dtb_ref[...])
    da = dt * a_ref[...]
    r = lax.broadcasted_iota(I32, (L, L), 0)
    cc = lax.broadcasted_iota(I32, (L, L), 1)
    tri = r >= cc
    a_cum = _dot_hi(jnp.where(tri, 1.0, 0.0), da)
    a_cum_t = a_cum.T
    expand = exp_ref[...]
    dt_full = _dot_hi(dt, expand)
    acum_full = _dot_hi(a_cum, expand)
    a_last = acum_full[L - 1:L, :]
    xdt = xs * dt_full
    xdt_b = xdt.astype(BF16)

    ys = []
    for g in range(SSM_GROUPS):
        bg = bm[:, g * SSM_STATE:(g + 1) * SSM_STATE].astype(BF16)
        cg = cm[:, g * SSM_STATE:(g + 1) * SSM_STATE].astype(BF16)
        cb = _dot_nt(cg, bg)
        hpg = SSM_HEADS // SSM_GROUPS
        cols = []
        for hh in range(hpg):
            h = g * hpg + hh
            diff = a_cum[:, MISC_DT + h:MISC_DT + h + 1] - a_cum_t[MISC_DT + h:MISC_DT + h + 1, :]
            seg = jnp.where(tri, jnp.exp(jnp.where(tri, diff, 0.0)), 0.0)
            sc = (cb * seg).astype(BF16)
            pair = (h // 2) * LANES
            yp = _dot(sc, xdt_b[:, pair:pair + LANES])
            cols.append(yp[:, (h % 2) * SSM_HEAD_DIM:(h % 2 + 1) * SSM_HEAD_DIM])
        y_diag = jnp.concatenate(cols, axis=1)
        gs = slice(g * half, (g + 1) * half)
        st = state_ref[:, gs]
        y_off = _dot(cg, st.astype(BF16)) * jnp.exp(acum_full[:, gs])
        ys.append(y_diag + y_off)
        xw = (xdt[:, gs] * jnp.exp(a_last[:, gs] - acum_full[:, gs])).astype(BF16)
        new = _dot(bm[:, g * SSM_STATE:(g + 1) * SSM_STATE].T.astype(BF16), xw)
        state_ref[:, gs] = st * jnp.exp(a_last[:, gs]) + new
    y = jnp.concatenate(ys, axis=1) + xs * dsk_ref[...]
    y = y * _silu(z_ref[...])
    o_ref[...] = _rms(y, nw_ref[...]).astype(BF16)


def ssd_mixer(proj, conv_w, conv_b, dtb_row, a_row, dsk_row, norm_w, expand):
    t = proj.shape[0]
    L = SSM_CHUNK
    full = lambda a: pl.BlockSpec(a.shape, lambda c: (0,) * a.ndim)
    return pl.pallas_call(
        _ssd_kernel,
        grid=(t // L,),
        in_specs=[pl.BlockSpec((L, SSM_INNER), lambda c: (c, COL_MZ // SSM_INNER)),
                  pl.BlockSpec((L, SSM_CONV_CH), lambda c: (c, COL_XBC // SSM_CONV_CH)),
                  pl.BlockSpec((L, LANES), lambda c: (c, COL_MISC // LANES)),
                  full(conv_w), full(conv_b), full(dtb_row), full(a_row), full(dsk_row), full(norm_w),
                  full(expand)],
        out_specs=pl.BlockSpec((L, SSM_INNER), lambda c: (c, 0)),
        out_shape=jax.ShapeDtypeStruct((t, SSM_INNER), BF16),
        scratch_shapes=[pltpu.VMEM((L + 8, SSM_CONV_CH), F32),
                        pltpu.VMEM((SSM_STATE, SSM_INNER), F32)],
        compiler_params=_cparams(("arbitrary",), VMEM_LIMIT),
        name="ssd_mixer",
    )(proj, proj, proj, conv_w, conv_b, dtb_row, a_row, dsk_row, norm_w, expand)


def _merge_kernel(x_ref, wg_ref, bg_ref, oa_ref, ob_ref, oc_ref, od_ref, wb_ref, o_ref):
    x = x_ref[...]
    acc = None
    for g, br in enumerate((oa_ref, ob_ref, oc_ref, od_ref)):
        logit = _dot(x, wg_ref[g]) + bg_ref[g:g + 1, :]
        gate = 1.0 / (1.0 + jnp.exp(-logit))
        term = gate * _dot(br[...], wb_ref[g])
        acc = term if acc is None else acc + term
    o_ref[...] = acc.astype(BF16)


def gated_merge(x_bf, wg, bg, branches, wb, tm=1024, tn=256):
    t = x_bf.shape[0]
    br_spec = pl.BlockSpec((tm, BRANCH_W), lambda j, i: (i, 0))
    return pl.pallas_call(
        _merge_kernel,
        grid=(D_MODEL // tn, t // tm),
        in_specs=[pl.BlockSpec((tm, D_MODEL), lambda j, i: (i, 0)),
                  pl.BlockSpec((N_BRANCH, D_MODEL, tn), lambda j, i: (0, 0, j)),
                  pl.BlockSpec((N_BRANCH, tn), lambda j, i: (0, j)),
                  br_spec, br_spec, br_spec, br_spec,
                  pl.BlockSpec((N_BRANCH, BRANCH_W, tn), lambda j, i: (0, 0, j))],
        out_specs=pl.BlockSpec((tm, tn), lambda j, i: (i, j)),
        out_shape=jax.ShapeDtypeStruct((t, D_MODEL), BF16),
        compiler_params=_cparams(("arbitrary", "arbitrary"), VMEM_LIMIT),
        name="gated_merge",
    )(x_bf, wg, bg, *branches, wb)


def _layer_norm(v, g, b):
    mu = jnp.mean(v, axis=-1, keepdims=True)
    d = v - mu
    var = jnp.mean(d * d, axis=-1, keepdims=True)
    return d * lax.rsqrt(var + LN_EPS) * g + b


def _outproj_kernel(m_ref, w_ref, x_ref, g_ref, b_ref, rw_ref, rb_ref, x1_ref, x1b_ref, idx_ref, gate_ref):
    mixed = _dot(m_ref[...], w_ref[...])
    x1 = _layer_norm(DN_ALPHA * x_ref[...] + mixed, g_ref[...], b_ref[...])
    x1_ref[...] = x1
    x1b = x1.astype(BF16)
    x1b_ref[...] = x1b
    logits = _dot(x1b, rw_ref[...]) + rb_ref[...]
    lane = lax.broadcasted_iota(I32, logits.shape, 1)
    g = logits
    idx_out = jnp.zeros(logits.shape, I32)
    val_out = jnp.zeros(logits.shape, F32)
    vals = []
    for kk in range(TOP_K):
        m = jnp.max(g, axis=1, keepdims=True)
        idx = jnp.min(jnp.where(g == m, lane, LANES), axis=1, keepdims=True)
        g = jnp.where(lane == idx, -jnp.inf, g)
        idx_out = jnp.where(lane == kk, idx, idx_out)
        vals.append(m)
    es = [jnp.exp(v - vals[0]) for v in vals]
    den = es[0] + es[1] + es[2] + es[3]
    for kk in range(TOP_K):
        val_out = jnp.where(lane == kk, es[kk] / den, val_out)
    idx_ref[...] = idx_out
    gate_ref[...] = val_out


def outproj_ln_router(merged, w_out, x, g, b, rw, rb, tm=256):
    t = x.shape[0]
    full = lambda a: pl.BlockSpec(a.shape, lambda i: (0,) * a.ndim)
    row = lambda w: pl.BlockSpec((tm, w), lambda i: (i, 0))
    return pl.pallas_call(
        _outproj_kernel,
        grid=(t // tm,),
        in_specs=[row(D_MODEL), full(w_out), row(D_MODEL), full(g), full(b), full(rw), full(rb)],
        out_specs=[row(D_MODEL), row(D_MODEL), row(LANES), row(LANES)],
        out_shape=[jax.ShapeDtypeStruct((t, D_MODEL), F32), jax.ShapeDtypeStruct((t, D_MODEL), BF16),
                   jax.ShapeDtypeStruct((t, LANES), I32), jax.ShapeDtypeStruct((t, LANES), F32)],
        compiler_params=_cparams(("arbitrary",), VMEM_LIMIT),
        name="outproj_ln_router",
    )(merged, w_out, x, g, b, rw, rb)


def _expert_kernel(be_ref, nu_ref, x_ref, wgu_ref, bgu_ref, wd_ref, bd_ref, o_ref):
    blk = pl.program_id(0)

    @pl.when(blk < nu_ref[0])
    def _():
        gu = _dot(x_ref[...], wgu_ref[0]) + bgu_ref[0]
        glu = jnp.minimum(gu[:, :D_FF], SWIGLU_LIMIT)
        lin = jnp.clip(gu[:, D_FF:], -SWIGLU_LIMIT, SWIGLU_LIMIT)
        act = glu / (1.0 + jnp.exp(-SWIGLU_ALPHA * glu)) * (lin + 1.0)
        o_ref[...] = _dot(act.astype(BF16), wd_ref[0]) + bd_ref[0]

    @pl.when(blk >= nu_ref[0])
    def _():
        o_ref[...] = jnp.zeros(o_ref.shape, F32)


def expert_ffn(block_e, n_used, buf_x, wgu, bgu, wd, bd):
    n_rows = buf_x.shape[0]
    grid_spec = pltpu.PrefetchScalarGridSpec(
        num_scalar_prefetch=2,
        grid=(n_rows // MOE_ROWS,),
        in_specs=[pl.BlockSpec((MOE_ROWS, D_MODEL), lambda b, be, nu: (b, 0)),
                  pl.BlockSpec((1, D_MODEL, 2 * D_FF), lambda b, be, nu: (be[b], 0, 0)),
                  pl.BlockSpec((1, 1, 2 * D_FF), lambda b, be, nu: (be[b], 0, 0)),
                  pl.BlockSpec((1, D_FF, D_MODEL), lambda b, be, nu: (be[b], 0, 0)),
                  pl.BlockSpec((1, 1, D_MODEL), lambda b, be, nu: (be[b], 0, 0))],
        out_specs=pl.BlockSpec((MOE_ROWS, D_MODEL), lambda b, be, nu: (b, 0)),
    )
    return pl.pallas_call(
        _expert_kernel,
        grid_spec=grid_spec,
        out_shape=jax.ShapeDtypeStruct((n_rows, D_MODEL), F32),
        compiler_params=_cparams(("arbitrary",), VMEM_LIMIT),
        name="expert_ffn",
    )(block_e, n_used, buf_x, wgu, bgu, wd, bd)


def _combine_kernel(r_ref, gate_ref, x_ref, g_ref, b_ref, o_ref, ob_ref):
    gate = gate_ref[...]
    y = jnp.zeros(x_ref.shape, F32)
    for kk in range(TOP_K):
        y = y + gate[:, kk:kk + 1] * r_ref[kk]
    x2 = _layer_norm(DN_ALPHA * x_ref[...] + y, g_ref[...], b_ref[...])
    o_ref[...] = x2
    ob_ref[...] = x2.astype(BF16)


def combine_ln(rows, gates, x1, g, b, tm=256):
    t = x1.shape[0]
    full = lambda a: pl.BlockSpec(a.shape, lambda i: (0,) * a.ndim)
    row = lambda w: pl.BlockSpec((tm, w), lambda i: (i, 0))
    return pl.pallas_call(
        _combine_kernel,
        grid=(t // tm,),
        in_specs=[pl.BlockSpec((TOP_K, tm, D_MODEL), lambda i: (0, i, 0)), row(LANES), row(D_MODEL),
                  full(g), full(b)],
        out_specs=[row(D_MODEL), row(D_MODEL)],
        out_shape=[jax.ShapeDtypeStruct((t, D_MODEL), F32), jax.ShapeDtypeStruct((t, D_MODEL), BF16)],
        compiler_params=_cparams(("arbitrary",), VMEM_LIMIT),
        name="combine_ln",
    )(rows, gates, x1, g, b)


def _rope_tables(positions):
    pos = positions.reshape(-1).astype(F32)

    def cs(dim, theta):
        inv = theta ** (-jnp.arange(0, dim, 2, dtype=F32) / dim)
        ang = pos[:, None] * inv
        return jnp.cos(ang), jnp.sin(ang)

    t = pos.shape[0]
    cos_p, sin_p = cs(ROT_DIM, ROPE_THETA)
    half = ROT_DIM // 2
    one = jnp.ones((t, HEAD_DIM - ROT_DIM), F32)
    zero = jnp.zeros((t, HEAD_DIM - ROT_DIM), F32)
    zh = jnp.zeros((t, half), F32)
    c64 = jnp.concatenate([cos_p, cos_p, one], axis=1)
    s1_64 = jnp.concatenate([-sin_p, zh, zero], axis=1)
    s2_64 = jnp.concatenate([zh, sin_p, zero], axis=1)
    tabs_p = tuple(jnp.tile(a, (1, 2)) for a in (c64, s1_64, s2_64))

    cos_m, sin_m = cs(MLA_ROPE, MLA_THETA)
    hm = MLA_ROPE // 2
    pre1 = jnp.ones((t, MLA_NOPE), F32)
    pre0 = jnp.zeros((t, MLA_NOPE), F32)
    post1 = jnp.ones((t, LANES - MLA_NOPE - MLA_ROPE), F32)
    post0 = jnp.zeros((t, LANES - MLA_NOPE - MLA_ROPE), F32)
    zm = jnp.zeros((t, hm), F32)
    c_m = jnp.concatenate([pre1, cos_m, cos_m, post1], axis=1)
    s1_m = jnp.concatenate([pre0, -sin_m, zm, post0], axis=1)
    s2_m = jnp.concatenate([pre0, zm, sin_m, post0], axis=1)
    return tabs_p, (c_m, s1_m, s2_m)


def _prep_in_weights(w_in):
    pts = np.cumsum((0,) + SPLIT_SIZES)
    seg = lambda k: w_in[:, pts[k]:pts[k + 1]]
    (a_q, a_k, a_v, m_z, m_xbc, m_dt, c_q, c_k, c_v, c_qi, c_ki, c_wi, d_cq, d_ckv, d_kpe) = [seg(k) for k in range(15)]
    scale = HEAD_DIM ** -0.5
    zeros = lambda n: jnp.zeros((D_MODEL, n), w_in.dtype)
    cols = [a_q * scale, a_k, a_v, c_q * scale, c_k, c_v, c_qi, c_ki, c_ki, c_ki, c_ki,
            m_z, m_xbc, d_ckv, d_cq, d_kpe, m_dt, c_wi * IDX_SCALE]
    small = jnp.concatenate(cols, axis=1)
    small = jnp.concatenate([small, zeros(N_SMALL - small.shape[1])], axis=1).astype(BF16)
    wg = w_in[:, pts[15]:].reshape(D_MODEL, N_BRANCH, D_MODEL).transpose(1, 0, 2).astype(BF16)
    return small, wg


def _prep_mla_weights(w_uq, w_ukv):
    dq = MLA_NOPE + MLA_ROPE
    wq = w_uq.reshape(MLA_Q_RANK, N_HEADS, dq) * (dq ** -0.5)
    wq = jnp.pad(wq, ((0, 0), (0, 0), (0, LANES - dq))).reshape(MLA_Q_RANK, N_HEADS * LANES)
    wkv = w_ukv.reshape(MLA_KV_RANK, N_HEADS, MLA_NOPE + MLA_V)
    wk = jnp.pad(wkv[:, :, :MLA_NOPE], ((0, 0), (0, 0), (0, LANES - MLA_NOPE))).reshape(MLA_KV_RANK, N_HEADS * LANES)
    wv = wkv[:, :, MLA_NOPE:].reshape(MLA_KV_RANK, N_HEADS * MLA_V)
    return wq.astype(BF16), wk.astype(BF16), wv.astype(BF16)


def _misc_row(v):
    return jnp.zeros((1, LANES), F32).at[0, MISC_DT:MISC_DT + SSM_HEADS].set(v.astype(F32))


def _route(top_idx, t):
    tk = t * TOP_K
    flat_e = top_idx.reshape(-1)
    counts = jnp.bincount(flat_e, length=N_EXPERTS)
    padded = ((counts + MOE_ROWS - 1) // MOE_ROWS) * MOE_ROWS
    pad_end = jnp.cumsum(padded)
    pad_start = pad_end - padded
    start = jnp.cumsum(counts) - counts
    order = jnp.argsort(flat_e, stable=True)
    sorted_e = flat_e[order]
    dest_sorted = pad_start[sorted_e] + jnp.arange(tk) - start[sorted_e]
    n_blocks = tk // MOE_ROWS + N_EXPERTS
    n_rows = n_blocks * MOE_ROWS
    row_tok = jnp.zeros((n_rows,), I32).at[dest_sorted].set((order // TOP_K).astype(I32))
    pos = jnp.zeros((tk,), I32).at[order].set(dest_sorted.astype(I32))
    block_start = jnp.arange(n_blocks) * MOE_ROWS
    block_e = jnp.minimum(jnp.sum(pad_end[None, :] <= block_start[:, None], axis=1), N_EXPERTS - 1).astype(I32)
    n_used = (pad_end[-1] // MOE_ROWS).astype(I32).reshape(1)
    return row_tok, pos, block_e, n_used


def _layer(x, x_bf, tabs_p, tabs_m, flags, expand, p):
    t = x.shape[0]
    w_small, wg = _prep_in_weights(p["w_in"])
    proj = matmul_bf16(x_bf, w_small)
    cast = rope_cast(proj, flags, tabs_p)

    o_a = moba_attention(cast)
    o_b = ssd_mixer(proj, p["conv_w"], p["conv_b"].reshape(1, -1), _misc_row(p["dt_bias"]),
                    _misc_row(-jnp.exp(p["a_log"].astype(F32))),
                    jnp.repeat(p["d_skip"].astype(F32), SSM_HEAD_DIM).reshape(1, -1),
                    p["ssm_norm_w"].reshape(1, -1), expand)
    o_c = dsa_attention(cast, dsa_select(cast, proj))
    wq, wk, wv = _prep_mla_weights(p["w_uq"], p["w_ukv"])
    q_m, k_m, v_m = mla_prep(proj, p["q_norm_w"].reshape(1, -1), p["kv_norm_w"].reshape(1, -1), wq, wk, wv, tabs_m)
    o_d = mla_attention(q_m, k_m, v_m)

    merged = gated_merge(x_bf, wg, p["b_gate"], (o_a, o_b, o_c, o_d), p["w_branch"].astype(BF16))
    rw = jnp.pad(p["router_w"], ((0, 0), (0, LANES - N_EXPERTS))).astype(BF16)
    rb = jnp.concatenate([p["router_b"].astype(F32), jnp.full((LANES - N_EXPERTS,), -jnp.inf, F32)]).reshape(1, -1)
    x1, x1_bf, top_idx, gates = outproj_ln_router(merged, p["w_out"].astype(BF16), x, p["ln1_g"].reshape(1, -1),
                                                  p["ln1_b"].reshape(1, -1), rw, rb)

    row_tok, pos, block_e, n_used = _route(top_idx[:, :TOP_K], t)
    buf_x = x1_bf[row_tok]
    out = expert_ffn(block_e, n_used, buf_x, p["w_gate_up"].astype(BF16), p["b_gate_up"][:, None, :],
                     p["w_down"].astype(BF16), p["b_down"][:, None, :])
    rows = out[pos.reshape(t, TOP_K).T]
    return combine_ln(rows, gates, x1, p["ln2_g"].reshape(1, -1), p["ln2_b"].reshape(1, -1))


def kernel(x, positions, w_in, b_gate, conv_w, conv_b, dt_bias, a_log, d_skip, ssm_norm_w, q_norm_w, w_uq,
           kv_norm_w, w_ukv, w_branch, w_out, ln1_g, ln1_b, router_w, router_b, w_gate_up, b_gate_up, w_down,
           b_down, ln2_g, ln2_b):
    b, s, d = x.shape
    assert b == 1 and d == D_MODEL
    names = ("w_in", "b_gate", "conv_w", "conv_b", "dt_bias", "a_log", "d_skip", "ssm_norm_w", "q_norm_w", "w_uq",
             "kv_norm_w", "w_ukv", "w_branch", "w_out", "ln1_g", "ln1_b", "router_w", "router_b", "w_gate_up",
             "b_gate_up", "w_down", "b_down", "ln2_g", "ln2_b")
    stacked = (w_in, b_gate, conv_w, conv_b, dt_bias, a_log, d_skip, ssm_norm_w, q_norm_w, w_uq, kv_norm_w, w_ukv,
               w_branch, w_out, ln1_g, ln1_b, router_w, router_b, w_gate_up, b_gate_up, w_down, b_down, ln2_g, ln2_b)
    tabs_p, tabs_m = _rope_tables(positions)
    flags = jnp.array([1, 1, 0, 1, 1, 0, 1], I32)
    head_of_lane = jnp.arange(SSM_INNER) // SSM_HEAD_DIM
    expand = (jnp.arange(LANES)[:, None] == (MISC_DT + head_of_lane)[None, :]).astype(F32)

    xt = x.reshape(s, d)
    xt_bf = xt.astype(BF16)
    for l in range(DEPTH):
        p = {n: a[l] for n, a in zip(names, stacked)}
        xt, xt_bf = _layer(xt, xt_bf, tabs_p, tabs_m, flags, expand, p)
    return xt.reshape(b, s, d)
```

```python
import functools
import math

import jax
import jax.numpy as jnp
import numpy as np
from jax import lax
from jax.experimental import pallas as pl
from jax.experimental.pallas import tpu as pltpu

F32 = jnp.float32
BF16 = jnp.bfloat16
I32 = jnp.int32

D_MODEL = 2048
DEPTH = 4
HEAD_DIM = 64
ROT_DIM = HEAD_DIM // 4
ROPE_THETA = 500000.0
N_HEADS = 8
ATT_W = N_HEADS * HEAD_DIM
MOBA_BLOCK = 256
MOBA_TOPK = 3
SSM_HEADS = 8
SSM_HEAD_DIM = 64
SSM_INNER = 512
SSM_STATE = 128
SSM_GROUPS = 2
SSM_CONV = 4
SSM_CONV_CH = 1024
SSM_CHUNK = 256
IDX_HEADS = 4
IDX_DIM = 64
DSA_TOPK = 256
IDX_SCALE = (IDX_HEADS * IDX_DIM) ** -0.5
MLA_Q_RANK = 384
MLA_KV_RANK = 256
MLA_NOPE = 64
MLA_ROPE = 32
MLA_V = 64
MLA_THETA = 10000.0
N_BRANCH = 4
BRANCH_W = 512
SPLIT_SIZES = (512, 512, 512, 512, 1024, 8, 512, 512, 512, 256, 64, 4, 384, 256, 32, N_BRANCH * D_MODEL)
N_EXPERTS = 32
TOP_K = 4
D_FF = 768
SWIGLU_ALPHA = 1.702
SWIGLU_LIMIT = 7.0
DN_ALPHA = (2 * DEPTH) ** 0.25
LN_EPS = 1e-5

LANES = 128
NEG = -1e30
VMEM_LIMIT = 56 * 1024 * 1024

COL_AQ, COL_AK, COL_AV = 0, 512, 1024
COL_CQ, COL_CK, COL_CV = 1536, 2048, 2560
COL_CQI, COL_KI4 = 3072, 3328
COL_MZ = 3584
COL_XBC = 4096
COL_CKV = 5120
COL_CQL = 5376
COL_MISC = 5760
N_SMALL = 6144
N_CAST = 3584
MISC_DT = 32
MISC_WI = 40
MOE_ROWS = 256


def _cparams(sem, vmem=None):
    return pltpu.CompilerParams(dimension_semantics=sem, vmem_limit_bytes=vmem)


def _dot(a, b):
    return jnp.dot(a, b, preferred_element_type=F32)


def _dot_nt(a, b):
    return lax.dot_general(a, b, (((1,), (1,)), ((), ())), preferred_element_type=F32)


def _dot_hi(a, b):
    return jnp.dot(a, b, preferred_element_type=F32, precision=lax.Precision.HIGHEST)


def _mm_kernel(x_ref, w_ref, o_ref):
    o_ref[...] = _dot(x_ref[...], w_ref[...])


def matmul_bf16(x, w, tm=512, tn=512):
    m, k = x.shape
    n = w.shape[1]
    return pl.pallas_call(
        _mm_kernel,
        grid=(n // tn, m // tm),
        in_specs=[pl.BlockSpec((tm, k), lambda j, i: (i, 0)),
                  pl.BlockSpec((k, tn), lambda j, i: (0, j))],
        out_specs=pl.BlockSpec((tm, tn), lambda j, i: (i, j)),
        out_shape=jax.ShapeDtypeStruct((m, n), F32),
        compiler_params=_cparams(("arbitrary", "arbitrary"), VMEM_LIMIT),
        name="in_proj",
    )(x, w)


def _rot_apply(x, c, s1, s2, shift):
    return x * c + pltpu.roll(x, LANES - shift, 1) * s1 + pltpu.roll(x, shift, 1) * s2


def _rope_cast_kernel(flag_ref, x_ref, c_ref, s1_ref, s2_ref, o_ref, ot_ref):
    j = pl.program_id(0)
    width = x_ref.shape[1]

    def emit(g, y):
        sl = slice(g * LANES, (g + 1) * LANES)
        o_ref[:, sl] = y.astype(BF16)
        ot_ref[sl, :] = y.T.astype(BF16)

    @pl.when(flag_ref[j] == 1)
    def _():
        c, s1, s2 = c_ref[...], s1_ref[...], s2_ref[...]
        for g in range(width // LANES):
            emit(g, _rot_apply(x_ref[:, g * LANES:(g + 1) * LANES], c, s1, s2, ROT_DIM // 2))

    @pl.when(flag_ref[j] == 0)
    def _():
        for g in range(width // LANES):
            emit(g, x_ref[:, g * LANES:(g + 1) * LANES])


def rope_cast(proj, flags, tabs, tw=512):
    t = proj.shape[0]
    tm = ATT_T
    grid_spec = pltpu.PrefetchScalarGridSpec(
        num_scalar_prefetch=1,
        grid=(N_CAST // tw, t // tm),
        in_specs=[pl.BlockSpec((tm, tw), lambda j, i, f: (i, j)),
                  pl.BlockSpec((tm, LANES), lambda j, i, f: (i, 0)),
                  pl.BlockSpec((tm, LANES), lambda j, i, f: (i, 0)),
                  pl.BlockSpec((tm, LANES), lambda j, i, f: (i, 0))],
        out_specs=[pl.BlockSpec((tm, tw), lambda j, i, f: (i, j)),
                   pl.BlockSpec((None, tw, tm), lambda j, i, f: (i, j, 0))],
    )
    return pl.pallas_call(
        _rope_cast_kernel,
        grid_spec=grid_spec,
        out_shape=[jax.ShapeDtypeStruct((t, N_CAST), BF16),
                   jax.ShapeDtypeStruct((t // tm, N_CAST, tm), BF16)],
        compiler_params=_cparams(("arbitrary", "arbitrary")),
        name="rope_cast",
    )(flags, proj, *tabs)


ATT_T = 256
ATT_HEADS = 4


def _softmax_step(s, vt, state):
    m, l, acc = state
    m_new = jnp.maximum(m, jnp.max(s, axis=0, keepdims=True))
    alpha = jnp.exp(m - m_new)
    p = jnp.exp(s - m_new)
    l_new = alpha * l + jnp.sum(p, axis=0, keepdims=True)
    acc_new = alpha * acc + _dot(vt, p.astype(BF16))
    return m_new, l_new, acc_new


def _init_state(tq):
    return (jnp.full((1, tq), NEG, F32), jnp.zeros((1, tq), F32), jnp.zeros((HEAD_DIM, tq), F32))


def _causal_t(tk, tq):
    return lax.broadcasted_iota(I32, (tk, tq), 0) <= lax.broadcasted_iota(I32, (tk, tq), 1)


def _head_rows(qt):
    row = lax.broadcasted_iota(I32, (LANES, 1), 0)
    out = []
    for p in range(qt.shape[0] // LANES):
        blk = qt[p * LANES:(p + 1) * LANES, :]
        zero = jnp.zeros_like(blk)
        out += [jnp.where(row < HEAD_DIM, blk, zero), jnp.where(row >= HEAD_DIM, blk, zero)]
    return out


def _pair_cols(h):
    return slice((h // 2) * LANES, (h // 2 + 1) * LANES)


def _attention_tile(i, score_fn, vt_fn, s_ref, tq, last_mask):
    nh = s_ref.shape[0]

    def put(scores):
        for h in range(nh):
            s_ref[h] = scores[h]

    put(score_fn(0))

    def body(n, carry):
        nxt = score_fn(n + 1)
        out = []
        for h in range(nh):
            out.extend(_softmax_step(s_ref[h], vt_fn(n, h), carry[3 * h:3 * h + 3]))
        put(nxt)
        return tuple(out)

    carry = lax.fori_loop(0, i, body, tuple(_init_state(tq)) * nh)
    fin = []
    for h in range(nh):
        s = s_ref[h]
        if last_mask is not None:
            s = jnp.where(last_mask, s, NEG)
        st = _softmax_step(s, vt_fn(i, h), carry[3 * h:3 * h + 3])
        fin.append(st[2] / st[1])
    return jnp.concatenate(fin, axis=0).T


def _top_n_rows(g, valid, n):
    row = lax.broadcasted_iota(I32, g.shape, 0)
    g = jnp.where(valid, g, -jnp.inf)
    sel = jnp.zeros(g.shape, jnp.bool_)
    for _ in range(n):
        m = jnp.max(g, axis=0, keepdims=True)
        idx = jnp.min(jnp.where(g == m, row, g.shape[0]), axis=0, keepdims=True)
        pick = row == idx
        sel = jnp.logical_or(sel, pick)
        g = jnp.where(pick, -jnp.inf, g)
    return jnp.logical_and(sel, valid)


def _moba_kernel(qt_ref, k_ref, vt_ref, o_ref, kmean_ref, sel_ref, s_ref):
    i = pl.program_id(1)
    tq = qt_ref.shape[1]
    nb = k_ref.shape[0] // MOBA_BLOCK

    @pl.when(i == 0)
    def _():
        kmean_ref[...] = jnp.zeros(kmean_ref.shape, F32)

        def mean_body(n, c):
            kb = k_ref[pl.ds(pl.multiple_of(n * MOBA_BLOCK, MOBA_BLOCK), MOBA_BLOCK), :].astype(F32)
            kmean_ref[pl.ds(n, 1), :] = jnp.sum(kb, axis=0, keepdims=True) * (1.0 / MOBA_BLOCK)
            return c

        lax.fori_loop(0, nb, mean_body, 0)

    nh = s_ref.shape[0]
    qh = _head_rows(qt_ref[...])
    kmean = kmean_ref[...].astype(BF16)
    row_n = lax.broadcasted_iota(I32, (kmean.shape[0], tq), 0)
    for h in range(nh):
        gate = _dot(kmean[:, _pair_cols(h)], qh[h])
        sel = _top_n_rows(gate, row_n < i, MOBA_TOPK)
        sel_ref[h] = jnp.where(jnp.logical_or(sel, row_n == i), 0.0, NEG)

    def score_fn(n):
        kb = k_ref[pl.ds(pl.multiple_of(n * MOBA_BLOCK, MOBA_BLOCK), MOBA_BLOCK), :]
        return [_dot(kb[:, _pair_cols(h)], qh[h]) + sel_ref[h, pl.ds(n, 1), :] for h in range(nh)]

    def vt_fn(n, h):
        return vt_ref[n, h * HEAD_DIM:(h + 1) * HEAD_DIM, :]

    o_ref[...] = _attention_tile(i, score_fn, vt_fn, s_ref, tq, _causal_t(MOBA_BLOCK, tq)).astype(BF16)


def moba_attention(cast, cast_t):
    t = cast.shape[0]
    tq = MOBA_BLOCK
    w = ATT_HEADS * HEAD_DIM
    nbp = max(8, t // MOBA_BLOCK)
    return pl.pallas_call(
        _moba_kernel,
        grid=(N_HEADS // ATT_HEADS, t // tq),
        in_specs=[pl.BlockSpec((None, w, tq), lambda g, i: (i, COL_AQ // w + g, 0)),
                  pl.BlockSpec((t, w), lambda g, i: (0, COL_AK // w + g)),
                  pl.BlockSpec((t // tq, w, tq), lambda g, i: (0, COL_AV // w + g, 0))],
        out_specs=pl.BlockSpec((tq, w), lambda g, i: (i, g)),
        out_shape=jax.ShapeDtypeStruct((t, ATT_W), BF16),
        scratch_shapes=[pltpu.VMEM((nbp, w), F32), pltpu.VMEM((ATT_HEADS, nbp, tq), F32),
                        pltpu.VMEM((ATT_HEADS, MOBA_BLOCK, tq), F32)],
        compiler_params=_cparams(("arbitrary", "arbitrary"), VMEM_LIMIT),
        name="moba_attention",
    )(cast_t, cast, cast_t)


def _float_key(x):
    b = lax.bitcast_convert_type(x, I32)
    return jnp.where(b < 0, b ^ jnp.int32(0x7FFFFFFF), b)


def _dsa_select_kernel(qit_ref, ki_ref, misct_ref, bias_ref, key_ref):
    i = pl.program_id(0)
    tq = qit_ref.shape[1]
    tk = ATT_T
    n_chunks = ki_ref.shape[0] // tk
    n_live = i + 1

    qit = qit_ref[...]
    row = lax.broadcasted_iota(I32, (IDX_HEADS * IDX_DIM, 1), 0)
    zero = jnp.zeros_like(qit)
    qh = [jnp.where((row >= h * IDX_DIM) & (row < (h + 1) * IDX_DIM), qit, zero) for h in range(IDX_HEADS)]
    wrow = [misct_ref[MISC_WI + h:MISC_WI + h + 1, :] for h in range(IDX_HEADS)]
    visible = _causal_t(tk, tq)

    def score_body(j, c):
        kic = ki_ref[pl.ds(pl.multiple_of(j * tk, tk), tk), :]
        acc = jnp.zeros((tk, tq), F32)
        for h in range(IDX_HEADS):
            acc = acc + wrow[h] * jnp.maximum(_dot(kic, qh[h]), 0.0)
        acc = jnp.where(jnp.logical_or(j < i, visible), acc, -jnp.inf)
        key_ref[j] = _float_key(acc)
        return c

    lax.fori_loop(0, n_live, score_body, 0)

    def count_ge(cand):
        def body(j, cnt):
            for r in range(tk // 8):
                cnt = cnt + jnp.where(key_ref[j, r * 8:(r + 1) * 8, :] >= cand, 1.0, 0.0)
            return cnt
        cnt = lax.fori_loop(0, n_live, body, jnp.zeros((8, tq), F32))
        return jnp.sum(cnt, axis=0, keepdims=True)

    topk = float(DSA_TOPK)
    int_min = jnp.int32(-2 ** 31)
    nonneg = count_ge(jnp.zeros((1, tq), I32)) >= topk
    prefix0 = jnp.where(nonneg, jnp.int32(0), int_min)

    def bit_body(b, prefix):
        cand = prefix | (jnp.int32(1) << (30 - b))
        return jnp.where(count_ge(cand) >= topk, cand, prefix)

    thr = lax.fori_loop(0, 31, bit_body, prefix0)

    need = topk - count_ge(thr + 1)
    lower = jnp.where(lax.broadcasted_iota(I32, (tk, tk), 0) >= lax.broadcasted_iota(I32, (tk, tk), 1),
                      1.0, 0.0).astype(BF16)

    def bias_body(j, seen):
        kk = key_ref[j]
        eq = kk == thr
        rank = _dot(lower, jnp.where(eq, 1.0, 0.0).astype(BF16)) + seen
        keep = ((kk > thr) | (eq & (rank <= need))) & jnp.logical_or(j < i, visible)
        bias_ref[pl.ds(pl.multiple_of(j * tk, tk), tk), :] = jnp.where(keep, 0.0, NEG).astype(BF16)
        return rank[tk - 1:tk, :]

    lax.fori_loop(0, n_live, bias_body, jnp.zeros((1, tq), F32))

    def fill_body(j, c):
        bias_ref[pl.ds(pl.multiple_of(j * tk, tk), tk), :] = jnp.full((tk, tq), NEG, BF16)
        return c

    lax.fori_loop(n_live, n_chunks, fill_body, 0)


def dsa_select(cast, cast_t, misc_t):
    t = cast.shape[0]
    tq = ATT_T
    w = IDX_HEADS * IDX_DIM
    return pl.pallas_call(
        _dsa_select_kernel,
        grid=(t // tq,),
        in_specs=[pl.BlockSpec((None, w, tq), lambda i: (i, COL_CQI // w, 0)),
                  pl.BlockSpec((t, w), lambda i: (0, COL_KI4 // w)),
                  pl.BlockSpec((None, LANES, tq), lambda i: (i, 0, 0))],
        out_specs=pl.BlockSpec((None, t, tq), lambda i: (i, 0, 0)),
        out_shape=jax.ShapeDtypeStruct((t // tq, t, tq), BF16),
        scratch_shapes=[pltpu.VMEM((t // ATT_T, ATT_T, tq), I32)],
        compiler_params=_cparams(("arbitrary",), VMEM_LIMIT),
        name="dsa_select",
    )(cast_t, cast, misc_t)


def _dsa_attn_kernel(qt_ref, k_ref, vt_ref, bias_ref, o_ref, s_ref):
    i = pl.program_id(1)
    tq = qt_ref.shape[1]
    tk = ATT_T
    nh = s_ref.shape[0]
    qh = _head_rows(qt_ref[...])

    def score_fn(n):
        off = pl.multiple_of(n * tk, tk)
        kb = k_ref[pl.ds(off, tk), :]
        bias = bias_ref[pl.ds(off, tk), :].astype(F32)
        return [_dot(kb[:, _pair_cols(h)], qh[h]) + bias for h in range(nh)]

    def vt_fn(n, h):
        return vt_ref[n, h * HEAD_DIM:(h + 1) * HEAD_DIM, :]

    o_ref[...] = _attention_tile(i, score_fn, vt_fn, s_ref, tq, None).astype(BF16)


def dsa_attention(cast, cast_t, bias):
    t = cast.shape[0]
    tq = ATT_T
    w = ATT_HEADS * HEAD_DIM
    return pl.pallas_call(
        _dsa_attn_kernel,
        grid=(N_HEADS // ATT_HEADS, t // tq),
        in_specs=[pl.BlockSpec((None, w, tq), lambda g, i: (i, COL_CQ // w + g, 0)),
                  pl.BlockSpec((t, w), lambda g, i: (0, COL_CK // w + g)),
                  pl.BlockSpec((t // tq, w, tq), lambda g, i: (0, COL_CV // w + g, 0)),
                  pl.BlockSpec((None, t, tq), lambda g, i: (i, 0, 0))],
        out_specs=pl.BlockSpec((tq, w), lambda g, i: (i, g)),
        out_shape=jax.ShapeDtypeStruct((t, ATT_W), BF16),
        scratch_shapes=[pltpu.VMEM((ATT_HEADS, ATT_T, tq), F32)],
        compiler_params=_cparams(("arbitrary", "arbitrary"), VMEM_LIMIT),
        name="dsa_attention",
    )(cast_t, cast, cast_t, bias)


def _rms(x, w):
    return x * lax.rsqrt(jnp.mean(x * x, axis=-1, keepdims=True) + LN_EPS) * w


def _mla_prep_kernel(cq_ref, ckv_ref, misc_ref, qn_ref, kvn_ref, wq_ref, wk_ref, wv_ref,
                     c_ref, s1_ref, s2_ref, qt_ref, k_ref, vt_ref, misct_ref):
    c, s1, s2 = c_ref[...], s1_ref[...], s2_ref[...]
    cq = _rms(cq_ref[...], qn_ref[...]).astype(BF16)
    ckv = _rms(ckv_ref[...], kvn_ref[...]).astype(BF16)
    misc = misc_ref[...]
    misct_ref[...] = misc.T
    lane = lax.broadcasted_iota(I32, (1, LANES), 1)
    kpe = pltpu.roll(jnp.where(lane < MLA_ROPE, misc, 0.0), MLA_NOPE, 1)
    kpe = _rot_apply(kpe, c, s1, s2, MLA_ROPE // 2)
    q = _dot(cq, wq_ref[...])
    k = _dot(ckv, wk_ref[...])
    for h in range(N_HEADS):
        sl = slice(h * LANES, (h + 1) * LANES)
        qt_ref[sl, :] = _rot_apply(q[:, sl], c, s1, s2, MLA_ROPE // 2).T.astype(BF16)
        k_ref[:, sl] = (k[:, sl] + kpe).astype(BF16)
    v = _dot(ckv, wv_ref[...])
    for g in range(ATT_W // LANES):
        sl = slice(g * LANES, (g + 1) * LANES)
        vt_ref[sl, :] = v[:, sl].T.astype(BF16)


def mla_prep(proj, qn, kvn, wq, wk, wv, tabs):
    t = proj.shape[0]
    tm = ATT_T
    full = lambda a: pl.BlockSpec(a.shape, lambda i: (0,) * a.ndim)
    row = lambda w, col: pl.BlockSpec((tm, w), lambda i: (i, col // w))
    tile_t = lambda n: pl.BlockSpec((None, n, tm), lambda i: (i, 0, 0))
    return pl.pallas_call(
        _mla_prep_kernel,
        grid=(t // tm,),
        in_specs=[row(MLA_Q_RANK, COL_CQL), row(MLA_KV_RANK, COL_CKV), row(LANES, COL_MISC),
                  full(qn), full(kvn), full(wq), full(wk), full(wv),
                  row(LANES, 0), row(LANES, 0), row(LANES, 0)],
        out_specs=[tile_t(N_HEADS * LANES),
                   pl.BlockSpec((tm, N_HEADS * LANES), lambda i: (i, 0)),
                   tile_t(ATT_W), tile_t(LANES)],
        out_shape=[jax.ShapeDtypeStruct((t // tm, N_HEADS * LANES, tm), BF16),
                   jax.ShapeDtypeStruct((t, N_HEADS * LANES), BF16),
                   jax.ShapeDtypeStruct((t // tm, ATT_W, tm), BF16),
                   jax.ShapeDtypeStruct((t // tm, LANES, tm), F32)],
        compiler_params=_cparams(("arbitrary",), VMEM_LIMIT),
        name="mla_prep",
    )(proj, proj, proj, qn, kvn, wq, wk, wv, *tabs)


def _mla_attn_kernel(qt_ref, k_ref, vt_ref, o_ref, s_ref):
    i = pl.program_id(1)
    tq = qt_ref.shape[1]
    tk = ATT_T
    nh = s_ref.shape[0]
    qh = [qt_ref[h * LANES:(h + 1) * LANES, :] for h in range(nh)]

    def score_fn(n):
        off = pl.multiple_of(n * tk, tk)
        return [_dot(k_ref[pl.ds(off, tk), h * LANES:(h + 1) * LANES], qh[h]) for h in range(nh)]

    def vt_fn(n, h):
        return vt_ref[n, h * HEAD_DIM:(h + 1) * HEAD_DIM, :]

    o_ref[...] = _attention_tile(i, score_fn, vt_fn, s_ref, tq, _causal_t(tk, tq)).astype(BF16)


def mla_attention(qt, k, vt):
    t = k.shape[0]
    tq = ATT_T
    wqk = ATT_HEADS * LANES
    wv = ATT_HEADS * HEAD_DIM
    return pl.pallas_call(
        _mla_attn_kernel,
        grid=(N_HEADS // ATT_HEADS, t // tq),
        in_specs=[pl.BlockSpec((None, wqk, tq), lambda g, i: (i, g, 0)),
                  pl.BlockSpec((t, wqk), lambda g, i: (0, g)),
                  pl.BlockSpec((t // tq, wv, tq), lambda g, i: (0, g, 0))],
        out_specs=pl.BlockSpec((tq, wv), lambda g, i: (i, g)),
        out_shape=jax.ShapeDtypeStruct((t, ATT_W), BF16),
        scratch_shapes=[pltpu.VMEM((ATT_HEADS, ATT_T, tq), F32)],
        compiler_params=_cparams(("arbitrary", "arbitrary"), VMEM_LIMIT),
        name="mla_attention",
    )(qt, k, vt)


def _silu(x):
    return x / (1.0 + jnp.exp(-x))


def _softplus(x):
    return jnp.maximum(x, 0.0) + jnp.log1p(jnp.exp(-jnp.abs(x)))


def _ssd_kernel(z_ref, xbc_ref, misc_ref, cw_ref, cb_ref, dtb_ref, a_ref, dsk_ref, nw_ref, exp_ref,
                o_ref, ext_ref, state_ref):
    c = pl.program_id(0)
    L = SSM_CHUNK
    half = SSM_INNER // SSM_GROUPS

    @pl.when(c == 0)
    def _():
        ext_ref[0:8, :] = jnp.zeros((8, SSM_CONV_CH), F32)
        state_ref[...] = jnp.zeros(state_ref.shape, F32)

    ext_ref[8:8 + L, :] = xbc_ref[...]
    conv = jnp.zeros((L, SSM_CONV_CH), F32) + cb_ref[...]
    for kk in range(SSM_CONV):
        conv = conv + cw_ref[kk:kk + 1, :] * ext_ref[pl.ds(8 - (SSM_CONV - 1) + kk, L), :]
    tail = ext_ref[L:L + 8, :]
    ext_ref[0:8, :] = tail
    xbc = _silu(conv)
    xs = xbc[:, 0:SSM_INNER]
    bm = xbc[:, SSM_INNER:SSM_INNER + SSM_GROUPS * SSM_STATE]
    cm = xbc[:, SSM_INNER + SSM_GROUPS * SSM_STATE:]

    dt = _softplus(misc_ref[...] + dtb_ref[...])
    da = dt * a_ref[...]
    r = lax.broadcasted_iota(I32, (L, L), 0)
    cc = lax.broadcasted_iota(I32, (L, L), 1)
    tri = r >= cc
    a_cum = _dot_hi(jnp.where(tri, 1.0, 0.0), da)
    a_cum_t = a_cum.T
    expand = exp_ref[...]
    dt_full = _dot_hi(dt, expand)
    acum_full = _dot_hi(a_cum, expand)
    a_last = acum_full[L - 1:L, :]
    xdt = xs * dt_full
    xdt_b = xdt.astype(BF16)

    ys = []
    for g in range(SSM_GROUPS):
        bg = bm[:, g * SSM_STATE:(g + 1) * SSM_STATE].astype(BF16)
        cg = cm[:, g * SSM_STATE:(g + 1) * SSM_STATE].astype(BF16)
        cb = _dot_nt(cg, bg)
        hpg = SSM_HEADS // SSM_GROUPS
        cols = []
        for hh in range(hpg):
            h = g * hpg + hh
            diff = a_cum[:, MISC_DT + h:MISC_DT + h + 1] - a_cum_t[MISC_DT + h:MISC_DT + h + 1, :]
            seg = jnp.where(tri, jnp.exp(jnp.where(tri, diff, 0.0)), 0.0)
            sc = (cb * seg).astype(BF16)
            pair = (h // 2) * LANES
            yp = _dot(sc, xdt_b[:, pair:pair + LANES])
            cols.append(yp[:, (h % 2) * SSM_HEAD_DIM:(h % 2 + 1) * SSM_HEAD_DIM])
        y_diag = jnp.concatenate(cols, axis=1)
        gs = slice(g * half, (g + 1) * half)
        st = state_ref[:, gs]
        y_off = _dot(cg, st.astype(BF16)) * jnp.exp(acum_full[:, gs])
        ys.append(y_diag + y_off)
        xw = (xdt[:, gs] * jnp.exp(a_last[:, gs] - acum_full[:, gs])).astype(BF16)
        new = _dot(bm[:, g * SSM_STATE:(g + 1) * SSM_STATE].T.astype(BF16), xw)
        state_ref[:, gs] = st * jnp.exp(a_last[:, gs]) + new
    y = jnp.concatenate(ys, axis=1) + xs * dsk_ref[...]
    y = y * _silu(z_ref[...])
    o_ref[...] = _rms(y, nw_ref[...]).astype(BF16)


def ssd_mixer(proj, conv_w, conv_b, dtb_row, a_row, dsk_row, norm_w, expand):
    t = proj.shape[0]
    L = SSM_CHUNK
    full = lambda a: pl.BlockSpec(a.shape, lambda c: (0,) * a.ndim)
    return pl.pallas_call(
        _ssd_kernel,
        grid=(t // L,),
        in_specs=[pl.BlockSpec((L, SSM_INNER), lambda c: (c, COL_MZ // SSM_INNER)),
                  pl.BlockSpec((L, SSM_CONV_CH), lambda c: (c, COL_XBC // SSM_CONV_CH)),
                  pl.BlockSpec((L, LANES), lambda c: (c, COL_MISC // LANES)),
                  full(conv_w), full(conv_b), full(dtb_row), full(a_row), full(dsk_row), full(norm_w),
                  full(expand)],
        out_specs=pl.BlockSpec((L, SSM_INNER), lambda c: (c, 0)),
        out_shape=jax.ShapeDtypeStruct((t, SSM_INNER), BF16),
        scratch_shapes=[pltpu.VMEM((L + 8, SSM_CONV_CH), F32),
                        pltpu.VMEM((SSM_STATE, SSM_INNER), F32)],
        compiler_params=_cparams(("arbitrary",), VMEM_LIMIT),
        name="ssd_mixer",
    )(proj, proj, proj, conv_w, conv_b, dtb_row, a_row, dsk_row, norm_w, expand)


def _merge_kernel(x_ref, wg_ref, bg_ref, oa_ref, ob_ref, oc_ref, od_ref, wb_ref, o_ref):
    x = x_ref[...]
    acc = None
    for g, br in enumerate((oa_ref, ob_ref, oc_ref, od_ref)):
        logit = _dot(x, wg_ref[g]) + bg_ref[g:g + 1, :]
        gate = 1.0 / (1.0 + jnp.exp(-logit))
        term = gate * _dot(br[...], wb_ref[g])
        acc = term if acc is None else acc + term
    o_ref[...] = acc.astype(BF16)


def gated_merge(x_bf, wg, bg, branches, wb, tm=1024, tn=256):
    t = x_bf.shape[0]
    br_spec = pl.BlockSpec((tm, BRANCH_W), lambda j, i: (i, 0))
    return pl.pallas_call(
        _merge_kernel,
        grid=(D_MODEL // tn, t // tm),
        in_specs=[pl.BlockSpec((tm, D_MODEL), lambda j, i: (i, 0)),
                  pl.BlockSpec((N_BRANCH, D_MODEL, tn), lambda j, i: (0, 0, j)),
                  pl.BlockSpec((N_BRANCH, tn), lambda j, i: (0, j)),
                  br_spec, br_spec, br_spec, br_spec,
                  pl.BlockSpec((N_BRANCH, BRANCH_W, tn), lambda j, i: (0, 0, j))],
        out_specs=pl.BlockSpec((tm, tn), lambda j, i: (i, j)),
        out_shape=jax.ShapeDtypeStruct((t, D_MODEL), BF16),
        compiler_params=_cparams(("arbitrary", "arbitrary"), VMEM_LIMIT),
        name="gated_merge",
    )(x_bf, wg, bg, *branches, wb)


def _layer_norm(v, g, b):
    mu = jnp.mean(v, axis=-1, keepdims=True)
    d = v - mu
    var = jnp.mean(d * d, axis=-1, keepdims=True)
    return d * lax.rsqrt(var + LN_EPS) * g + b


def _outproj_kernel(m_ref, w_ref, x_ref, g_ref, b_ref, rw_ref, rb_ref, x1_ref, x1b_ref, idx_ref, gate_ref):
    mixed = _dot(m_ref[...], w_ref[...])
    x1 = _layer_norm(DN_ALPHA * x_ref[...] + mixed, g_ref[...], b_ref[...])
    x1_ref[...] = x1
    x1b = x1.astype(BF16)
    x1b_ref[...] = x1b
    logits = _dot(x1b, rw_ref[...]) + rb_ref[...]
    lane = lax.broadcasted_iota(I32, logits.shape, 1)
    g = logits
    idx_out = jnp.zeros(logits.shape, I32)
    val_out = jnp.zeros(logits.shape, F32)
    vals = []
    for kk in range(TOP_K):
        m = jnp.max(g, axis=1, keepdims=True)
        idx = jnp.min(jnp.where(g == m, lane, LANES), axis=1, keepdims=True)
        g = jnp.where(lane == idx, -jnp.inf, g)
        idx_out = jnp.where(lane == kk, idx, idx_out)
        vals.append(m)
    es = [jnp.exp(v - vals[0]) for v in vals]
    den = es[0] + es[1] + es[2] + es[3]
    for kk in range(TOP_K):
        val_out = jnp.where(lane == kk, es[kk] / den, val_out)
    idx_ref[...] = idx_out
    gate_ref[...] = val_out


def outproj_ln_router(merged, w_out, x, g, b, rw, rb, tm=256):
    t = x.shape[0]
    full = lambda a: pl.BlockSpec(a.shape, lambda i: (0,) * a.ndim)
    row = lambda w: pl.BlockSpec((tm, w), lambda i: (i, 0))
    return pl.pallas_call(
        _outproj_kernel,
        grid=(t // tm,),
        in_specs=[row(D_MODEL), full(w_out), row(D_MODEL), full(g), full(b), full(rw), full(rb)],
        out_specs=[row(D_MODEL), row(D_MODEL), row(LANES), row(LANES)],
        out_shape=[jax.ShapeDtypeStruct((t, D_MODEL), F32), jax.ShapeDtypeStruct((t, D_MODEL), BF16),
                   jax.ShapeDtypeStruct((t, LANES), I32), jax.ShapeDtypeStruct((t, LANES), F32)],
        compiler_params=_cparams(("arbitrary",), VMEM_LIMIT),
        name="outproj_ln_router",
    )(merged, w_out, x, g, b, rw, rb)


def _expert_kernel(be_ref, nu_ref, x_ref, wgu_ref, bgu_ref, wd_ref, bd_ref, o_ref):
    blk = pl.program_id(0)

    @pl.when(blk < nu_ref[0])
    def _():
        gu = _dot(x_ref[...], wgu_ref[0]) + bgu_ref[0]
        glu = jnp.minimum(gu[:, :D_FF], SWIGLU_LIMIT)
        lin = jnp.clip(gu[:, D_FF:], -SWIGLU_LIMIT, SWIGLU_LIMIT)
        act = glu / (1.0 + jnp.exp(-SWIGLU_ALPHA * glu)) * (lin + 1.0)
        o_ref[...] = _dot(act.astype(BF16), wd_ref[0]) + bd_ref[0]

    @pl.when(blk >= nu_ref[0])
    def _():
        o_ref[...] = jnp.zeros(o_ref.shape, F32)


def expert_ffn(block_e, n_used, buf_x, wgu, bgu, wd, bd):
    n_rows = buf_x.shape[0]
    grid_spec = pltpu.PrefetchScalarGridSpec(
        num_scalar_prefetch=2,
        grid=(n_rows // MOE_ROWS,),
        in_specs=[pl.BlockSpec((MOE_ROWS, D_MODEL), lambda b, be, nu: (b, 0)),
                  pl.BlockSpec((1, D_MODEL, 2 * D_FF), lambda b, be, nu: (be[b], 0, 0)),
                  pl.BlockSpec((1, 1, 2 * D_FF), lambda b, be, nu: (be[b], 0, 0)),
                  pl.BlockSpec((1, D_FF, D_MODEL), lambda b, be, nu: (be[b], 0, 0)),
                  pl.BlockSpec((1, 1, D_MODEL), lambda b, be, nu: (be[b], 0, 0))],
        out_specs=pl.BlockSpec((MOE_ROWS, D_MODEL), lambda b, be, nu: (b, 0)),
    )
    return pl.pallas_call(
        _expert_kernel,
        grid_spec=grid_spec,
        out_shape=jax.ShapeDtypeStruct((n_rows, D_MODEL), F32),
        compiler_params=_cparams(("arbitrary",), VMEM_LIMIT),
        name="expert_ffn",
    )(block_e, n_used, buf_x, wgu, bgu, wd, bd)


def _combine_kernel(r_ref, gate_ref, x_ref, g_ref, b_ref, o_ref, ob_ref):
    gate = gate_ref[...]
    y = jnp.zeros(x_ref.shape, F32)
    for kk in range(TOP_K):
        y = y + gate[:, kk:kk + 1] * r_ref[kk]
    x2 = _layer_norm(DN_ALPHA * x_ref[...] + y, g_ref[...], b_ref[...])
    o_ref[...] = x2
    ob_ref[...] = x2.astype(BF16)


def combine_ln(rows, gates, x1, g, b, tm=256):
    t = x1.shape[0]
    full = lambda a: pl.BlockSpec(a.shape, lambda i: (0,) * a.ndim)
    row = lambda w: pl.BlockSpec((tm, w), lambda i: (i, 0))
    return pl.pallas_call(
        _combine_kernel,
        grid=(t // tm,),
        in_specs=[pl.BlockSpec((TOP_K, tm, D_MODEL), lambda i: (0, i, 0)), row(LANES), row(D_MODEL),
                  full(g), full(b)],
        out_specs=[row(D_MODEL), row(D_MODEL)],
        out_shape=[jax.ShapeDtypeStruct((t, D_MODEL), F32), jax.ShapeDtypeStruct((t, D_MODEL), BF16)],
        compiler_params=_cparams(("arbitrary",), VMEM_LIMIT),
        name="combine_ln",
    )(rows, gates, x1, g, b)


def _rope_tables(positions):
    pos = positions.reshape(-1).astype(F32)

    def cs(dim, theta):
        inv = theta ** (-jnp.arange(0, dim, 2, dtype=F32) / dim)
        ang = pos[:, None] * inv
        return jnp.cos(ang), jnp.sin(ang)

    t = pos.shape[0]
    cos_p, sin_p = cs(ROT_DIM, ROPE_THETA)
    half = ROT_DIM // 2
    one = jnp.ones((t, HEAD_DIM - ROT_DIM), F32)
    zero = jnp.zeros((t, HEAD_DIM - ROT_DIM), F32)
    zh = jnp.zeros((t, half), F32)
    c64 = jnp.concatenate([cos_p, cos_p, one], axis=1)
    s1_64 = jnp.concatenate([-sin_p, zh, zero], axis=1)
    s2_64 = jnp.concatenate([zh, sin_p, zero], axis=1)
    tabs_p = tuple(jnp.tile(a, (1, 2)) for a in (c64, s1_64, s2_64))

    cos_m, sin_m = cs(MLA_ROPE, MLA_THETA)
    hm = MLA_ROPE // 2
    pre1 = jnp.ones((t, MLA_NOPE), F32)
    pre0 = jnp.zeros((t, MLA_NOPE), F32)
    post1 = jnp.ones((t, LANES - MLA_NOPE - MLA_ROPE), F32)
    post0 = jnp.zeros((t, LANES - MLA_NOPE - MLA_ROPE), F32)
    zm = jnp.zeros((t, hm), F32)
    c_m = jnp.concatenate([pre1, cos_m, cos_m, post1], axis=1)
    s1_m = jnp.concatenate([pre0, -sin_m, zm, post0], axis=1)
    s2_m = jnp.concatenate([pre0, zm, sin_m, post0], axis=1)
    return tabs_p, (c_m, s1_m, s2_m)


def _prep_in_weights(w_in):
    pts = np.cumsum((0,) + SPLIT_SIZES)
    seg = lambda k: w_in[:, pts[k]:pts[k + 1]]
    (a_q, a_k, a_v, m_z, m_xbc, m_dt, c_q, c_k, c_v, c_qi, c_ki, c_wi, d_cq, d_ckv, d_kpe) = [seg(k) for k in range(15)]
    scale = HEAD_DIM ** -0.5
    zeros = lambda n: jnp.zeros((D_MODEL, n), w_in.dtype)
    cols = [a_q * scale, a_k, a_v, c_q * scale, c_k, c_v, c_qi, c_ki, c_ki, c_ki, c_ki,
            m_z, m_xbc, d_ckv, d_cq, d_kpe, m_dt, c_wi * IDX_SCALE]
    small = jnp.concatenate(cols, axis=1)
    small = jnp.concatenate([small, zeros(N_SMALL - small.shape[1])], axis=1).astype(BF16)
    wg = w_in[:, pts[15]:].reshape(D_MODEL, N_BRANCH, D_MODEL).transpose(1, 0, 2).astype(BF16)
    return small, wg


def _prep_mla_weights(w_uq, w_ukv):
    dq = MLA_NOPE + MLA_ROPE
    wq = w_uq.reshape(MLA_Q_RANK, N_HEADS, dq) * (dq ** -0.5)
    wq = jnp.pad(wq, ((0, 0), (0, 0), (0, LANES - dq))).reshape(MLA_Q_RANK, N_HEADS * LANES)
    wkv = w_ukv.reshape(MLA_KV_RANK, N_HEADS, MLA_NOPE + MLA_V)
    wk = jnp.pad(wkv[:, :, :MLA_NOPE], ((0, 0), (0, 0), (0, LANES - MLA_NOPE))).reshape(MLA_KV_RANK, N_HEADS * LANES)
    wv = wkv[:, :, MLA_NOPE:].reshape(MLA_KV_RANK, N_HEADS * MLA_V)
    return wq.astype(BF16), wk.astype(BF16), wv.astype(BF16)


def _misc_row(v):
    return jnp.zeros((1, LANES), F32).at[0, MISC_DT:MISC_DT + SSM_HEADS].set(v.astype(F32))


def _route(top_idx, t):
    tk = t * TOP_K
    flat_e = top_idx.reshape(-1)
    counts = jnp.bincount(flat_e, length=N_EXPERTS)
    padded = ((counts + MOE_ROWS - 1) // MOE_ROWS) * MOE_ROWS
    pad_end = jnp.cumsum(padded)
    pad_start = pad_end - padded
    start = jnp.cumsum(counts) - counts
    order = jnp.argsort(flat_e, stable=True)
    sorted_e = flat_e[order]
    dest_sorted = pad_start[sorted_e] + jnp.arange(tk) - start[sorted_e]
    n_blocks = tk // MOE_ROWS + N_EXPERTS
    n_rows = n_blocks * MOE_ROWS
    row_tok = jnp.zeros((n_rows,), I32).at[dest_sorted].set((order // TOP_K).astype(I32))
    pos = jnp.zeros((tk,), I32).at[order].set(dest_sorted.astype(I32))
    block_start = jnp.arange(n_blocks) * MOE_ROWS
    block_e = jnp.minimum(jnp.sum(pad_end[None, :] <= block_start[:, None], axis=1), N_EXPERTS - 1).astype(I32)
    n_used = (pad_end[-1] // MOE_ROWS).astype(I32).reshape(1)
    return row_tok, pos, block_e, n_used


def _layer(x, x_bf, tabs_p, tabs_m, flags, expand, p):
    t = x.shape[0]
    w_small, wg = _prep_in_weights(p["w_in"])
    proj = matmul_bf16(x_bf, w_small)
    cast, cast_t = rope_cast(proj, flags, tabs_p)

    o_a = moba_attention(cast, cast_t)
    o_b = ssd_mixer(proj, p["conv_w"], p["conv_b"].reshape(1, -1), _misc_row(p["dt_bias"]),
                    _misc_row(-jnp.exp(p["a_log"].astype(F32))),
                    jnp.repeat(p["d_skip"].astype(F32), SSM_HEAD_DIM).reshape(1, -1),
                    p["ssm_norm_w"].reshape(1, -1), expand)
    wq, wk, wv = _prep_mla_weights(p["w_uq"], p["w_ukv"])
    qt_m, k_m, vt_m, misc_t = mla_prep(proj, p["q_norm_w"].reshape(1, -1), p["kv_norm_w"].reshape(1, -1), wq, wk, wv,
                                       tabs_m)
    o_c = dsa_attention(cast, cast_t, dsa_select(cast, cast_t, misc_t))
    o_d = mla_attention(qt_m, k_m, vt_m)

    merged = gated_merge(x_bf, wg, p["b_gate"], (o_a, o_b, o_c, o_d), p["w_branch"].astype(BF16))
    rw = jnp.pad(p["router_w"], ((0, 0), (0, LANES - N_EXPERTS))).astype(BF16)
    rb = jnp.concatenate([p["router_b"].astype(F32), jnp.full((LANES - N_EXPERTS,), -jnp.inf, F32)]).reshape(1, -1)
    x1, x1_bf, top_idx, gates = outproj_ln_router(merged, p["w_out"].astype(BF16), x, p["ln1_g"].reshape(1, -1),
                                                  p["ln1_b"].reshape(1, -1), rw, rb)

    row_tok, pos, block_e, n_used = _route(top_idx[:, :TOP_K], t)
    buf_x = x1_bf[row_tok]
    out = expert_ffn(block_e, n_used, buf_x, p["w_gate_up"].astype(BF16), p["b_gate_up"][:, None, :],
                     p["w_down"].astype(BF16), p["b_down"][:, None, :])
    rows = out[pos.reshape(t, TOP_K).T]
    return combine_ln(rows, gates, x1, p["ln2_g"].reshape(1, -1), p["ln2_b"].reshape(1, -1))


def kernel(x, positions, w_in, b_gate, conv_w, conv_b, dt_bias, a_log, d_skip, ssm_norm_w, q_norm_w, w_uq,
           kv_norm_w, w_ukv, w_branch, w_out, ln1_g, ln1_b, router_w, router_b, w_gate_up, b_gate_up, w_down,
           b_down, ln2_g, ln2_b):
    b, s, d = x.shape
    assert b == 1 and d == D_MODEL
    names = ("w_in", "b_gate", "conv_w", "conv_b", "dt_bias", "a_log", "d_skip", "ssm_norm_w", "q_norm_w", "w_uq",
             "kv_norm_w", "w_ukv", "w_branch", "w_out", "ln1_g", "ln1_b", "router_w", "router_b", "w_gate_up",
             "b_gate_up", "w_down", "b_down", "ln2_g", "ln2_b")
    stacked = (w_in, b_gate, conv_w, conv_b, dt_bias, a_log, d_skip, ssm_norm_w, q_norm_w, w_uq, kv_norm_w, w_ukv,
               w_branch, w_out, ln1_g, ln1_b, router_w, router_b, w_gate_up, b_gate_up, w_down, b_down, ln2_g, ln2_b)
    tabs_p, tabs_m = _rope_tables(positions)
    flags = jnp.array([1, 1, 0, 1, 1, 0, 1], I32)
    head_of_lane = jnp.arange(SSM_INNER) // SSM_HEAD_DIM
    expand = (jnp.arange(LANES)[:, None] == (MISC_DT + head_of_lane)[None, :]).astype(F32)

    xt = x.reshape(s, d)
    xt_bf = xt.astype(BF16)
    for l in range(DEPTH):
        p = {n: a[l] for n, a in zip(names, stacked)}
        xt, xt_bf = _layer(xt, xt_bf, tabs_p, tabs_m, flags, expand, p)
    return xt.reshape(b, s, d)
```

```python
import functools
import math

import jax
import jax.numpy as jnp
import numpy as np
from jax import lax
from jax.experimental import pallas as pl
from jax.experimental.pallas import tpu as pltpu

F32 = jnp.float32
BF16 = jnp.bfloat16
I32 = jnp.int32

D_MODEL = 2048
DEPTH = 4
HEAD_DIM = 64
ROT_DIM = HEAD_DIM // 4
ROPE_THETA = 500000.0
N_HEADS = 8
ATT_W = N_HEADS * HEAD_DIM
MOBA_BLOCK = 256
MOBA_TOPK = 3
SSM_HEADS = 8
SSM_HEAD_DIM = 64
SSM_INNER = 512
SSM_STATE = 128
SSM_GROUPS = 2
SSM_CONV = 4
SSM_CONV_CH = 1024
SSM_CHUNK = 256
IDX_HEADS = 4
IDX_DIM = 64
DSA_TOPK = 256
IDX_SCALE = (IDX_HEADS * IDX_DIM) ** -0.5
MLA_Q_RANK = 384
MLA_KV_RANK = 256
MLA_NOPE = 64
MLA_ROPE = 32
MLA_V = 64
MLA_THETA = 10000.0
N_BRANCH = 4
BRANCH_W = 512
SPLIT_SIZES = (512, 512, 512, 512, 1024, 8, 512, 512, 512, 256, 64, 4, 384, 256, 32, N_BRANCH * D_MODEL)
N_EXPERTS = 32
TOP_K = 4
D_FF = 768
SWIGLU_ALPHA = 1.702
SWIGLU_LIMIT = 7.0
DN_ALPHA = (2 * DEPTH) ** 0.25
LN_EPS = 1e-5

LANES = 128
NEG = -1e30
VMEM_LIMIT = 56 * 1024 * 1024

COL_AQ, COL_AK, COL_AV = 0, 512, 1024
COL_CQ, COL_CK, COL_CV = 1536, 2048, 2560
COL_CQI, COL_KI4 = 3072, 3328
COL_MZ = 3584
COL_XBC = 4096
COL_CKV = 5120
COL_CQL = 5376
COL_MISC = 5760
N_SMALL = 6144
N_CAST = 3584
MISC_DT = 32
MISC_WI = 40
MOE_ROWS = 256


def _cparams(sem, vmem=None):
    return pltpu.CompilerParams(dimension_semantics=sem, vmem_limit_bytes=vmem)


def _dot(a, b):
    return jnp.dot(a, b, preferred_element_type=F32)


def _dot_nt(a, b):
    return lax.dot_general(a, b, (((1,), (1,)), ((), ())), preferred_element_type=F32)


def _dot_hi(a, b):
    return jnp.dot(a, b, preferred_element_type=F32, precision=lax.Precision.HIGHEST)


def _mm_kernel(x_ref, w_ref, o_ref):
    o_ref[...] = _dot(x_ref[...], w_ref[...])


def matmul_bf16(x, w, tm=512, tn=512):
    m, k = x.shape
    n = w.shape[1]
    return pl.pallas_call(
        _mm_kernel,
        grid=(n // tn, m // tm),
        in_specs=[pl.BlockSpec((tm, k), lambda j, i: (i, 0)),
                  pl.BlockSpec((k, tn), lambda j, i: (0, j))],
        out_specs=pl.BlockSpec((tm, tn), lambda j, i: (i, j)),
        out_shape=jax.ShapeDtypeStruct((m, n), F32),
        compiler_params=_cparams(("arbitrary", "arbitrary"), VMEM_LIMIT),
        name="in_proj",
    )(x, w)


def _rot_apply(x, c, s1, s2, shift):
    return x * c + pltpu.roll(x, LANES - shift, 1) * s1 + pltpu.roll(x, shift, 1) * s2


def _rope_cast_kernel(flag_ref, x_ref, c_ref, s1_ref, s2_ref, o_ref, ot_ref):
    j = pl.program_id(0)
    width = x_ref.shape[1]

    def emit(g, y):
        sl = slice(g * LANES, (g + 1) * LANES)
        o_ref[:, sl] = y.astype(BF16)
        ot_ref[sl, :] = y.T.astype(BF16)

    @pl.when(flag_ref[j] == 1)
    def _():
        c, s1, s2 = c_ref[...], s1_ref[...], s2_ref[...]
        for g in range(width // LANES):
            emit(g, _rot_apply(x_ref[:, g * LANES:(g + 1) * LANES], c, s1, s2, ROT_DIM // 2))

    @pl.when(flag_ref[j] == 0)
    def _():
        for g in range(width // LANES):
            emit(g, x_ref[:, g * LANES:(g + 1) * LANES])


def rope_cast(proj, flags, tabs, tw=512):
    t = proj.shape[0]
    tm = ATT_T
    grid_spec = pltpu.PrefetchScalarGridSpec(
        num_scalar_prefetch=1,
        grid=(N_CAST // tw, t // tm),
        in_specs=[pl.BlockSpec((tm, tw), lambda j, i, f: (i, j)),
                  pl.BlockSpec((tm, LANES), lambda j, i, f: (i, 0)),
                  pl.BlockSpec((tm, LANES), lambda j, i, f: (i, 0)),
                  pl.BlockSpec((tm, LANES), lambda j, i, f: (i, 0))],
        out_specs=[pl.BlockSpec((tm, tw), lambda j, i, f: (i, j)),
                   pl.BlockSpec((None, tw, tm), lambda j, i, f: (i, j, 0))],
    )
    return pl.pallas_call(
        _rope_cast_kernel,
        grid_spec=grid_spec,
        out_shape=[jax.ShapeDtypeStruct((t, N_CAST), BF16),
                   jax.ShapeDtypeStruct((t // tm, N_CAST, tm), BF16)],
        compiler_params=_cparams(("arbitrary", "arbitrary")),
        name="rope_cast",
    )(flags, proj, *tabs)


ATT_T = 256
ATT_HEADS = 4


def _softmax_step(s, vt, state):
    m, l, acc = state
    m_new = jnp.maximum(m, jnp.max(s, axis=0, keepdims=True))
    alpha = jnp.exp(m - m_new)
    p = jnp.exp(s - m_new)
    l_new = alpha * l + jnp.sum(p, axis=0, keepdims=True)
    acc_new = alpha * acc + _dot(vt, p.astype(BF16))
    return m_new, l_new, acc_new


def _init_state(tq):
    return (jnp.full((1, tq), NEG, F32), jnp.zeros((1, tq), F32), jnp.zeros((HEAD_DIM, tq), F32))


def _causal_t(tk, tq):
    return lax.broadcasted_iota(I32, (tk, tq), 0) <= lax.broadcasted_iota(I32, (tk, tq), 1)


def _head_rows(qt):
    row = lax.broadcasted_iota(I32, (LANES, 1), 0)
    out = []
    for p in range(qt.shape[0] // LANES):
        blk = qt[p * LANES:(p + 1) * LANES, :]
        zero = jnp.zeros_like(blk)
        out += [jnp.where(row < HEAD_DIM, blk, zero), jnp.where(row >= HEAD_DIM, blk, zero)]
    return out


def _pair_cols(h):
    return slice((h // 2) * LANES, (h // 2 + 1) * LANES)


def _attention_tile(i, score_fn, vt_fn, s_ref, tq, last_mask):
    nh = s_ref.shape[0]

    def put(scores):
        for h in range(nh):
            s_ref[h] = scores[h]

    put(score_fn(0))

    def body(n, carry):
        nxt = score_fn(n + 1)
        out = []
        for h in range(nh):
            out.extend(_softmax_step(s_ref[h], vt_fn(n, h), carry[3 * h:3 * h + 3]))
        put(nxt)
        return tuple(out)

    carry = lax.fori_loop(0, i, body, tuple(_init_state(tq)) * nh)
    fin = []
    for h in range(nh):
        s = s_ref[h]
        if last_mask is not None:
            s = jnp.where(last_mask, s, NEG)
        st = _softmax_step(s, vt_fn(i, h), carry[3 * h:3 * h + 3])
        fin.append(st[2] / st[1])
    return jnp.concatenate(fin, axis=0).T


def _top_n_rows(g, valid, n):
    row = lax.broadcasted_iota(I32, g.shape, 0)
    g = jnp.where(valid, g, -jnp.inf)
    sel = jnp.zeros(g.shape, jnp.bool_)
    for _ in range(n):
        m = jnp.max(g, axis=0, keepdims=True)
        idx = jnp.min(jnp.where(g == m, row, g.shape[0]), axis=0, keepdims=True)
        pick = row == idx
        sel = jnp.logical_or(sel, pick)
        g = jnp.where(pick, -jnp.inf, g)
    return jnp.logical_and(sel, valid)


def _moba_kernel(qt_ref, k_ref, vt_ref, o_ref, kmean_ref, sel_ref, s_ref):
    i = pl.program_id(1)
    tq = qt_ref.shape[1]
    nb = k_ref.shape[0] // MOBA_BLOCK

    @pl.when(i == 0)
    def _():
        kmean_ref[...] = jnp.zeros(kmean_ref.shape, F32)

        def mean_body(n, c):
            kb = k_ref[pl.ds(pl.multiple_of(n * MOBA_BLOCK, MOBA_BLOCK), MOBA_BLOCK), :].astype(F32)
            kmean_ref[pl.ds(n, 1), :] = jnp.sum(kb, axis=0, keepdims=True) * (1.0 / MOBA_BLOCK)
            return c

        lax.fori_loop(0, nb, mean_body, 0)

    nh = s_ref.shape[0]
    qh = _head_rows(qt_ref[...])
    kmean = kmean_ref[...].astype(BF16)
    row_n = lax.broadcasted_iota(I32, (kmean.shape[0], tq), 0)
    for h in range(nh):
        gate = _dot(kmean[:, _pair_cols(h)], qh[h])
        sel = _top_n_rows(gate, row_n < i, MOBA_TOPK)
        sel_ref[h] = jnp.where(jnp.logical_or(sel, row_n == i), 0.0, NEG)

    def score_fn(n):
        kb = k_ref[pl.ds(pl.multiple_of(n * MOBA_BLOCK, MOBA_BLOCK), MOBA_BLOCK), :]
        return [_dot(kb[:, _pair_cols(h)], qh[h]) + sel_ref[h, pl.ds(n, 1), :] for h in range(nh)]

    def vt_fn(n, h):
        return vt_ref[n, h * HEAD_DIM:(h + 1) * HEAD_DIM, :]

    o_ref[...] = _attention_tile(i, score_fn, vt_fn, s_ref, tq, _causal_t(MOBA_BLOCK, tq)).astype(BF16)


def moba_attention(cast, cast_t):
    t = cast.shape[0]
    tq = MOBA_BLOCK
    w = ATT_HEADS * HEAD_DIM
    nbp = max(8, t // MOBA_BLOCK)
    return pl.pallas_call(
        _moba_kernel,
        grid=(N_HEADS // ATT_HEADS, t // tq),
        in_specs=[pl.BlockSpec((None, w, tq), lambda g, i: (i, COL_AQ // w + g, 0)),
                  pl.BlockSpec((t, w), lambda g, i: (0, COL_AK // w + g)),
                  pl.BlockSpec((t // tq, w, tq), lambda g, i: (0, COL_AV // w + g, 0))],
        out_specs=pl.BlockSpec((tq, w), lambda g, i: (i, g)),
        out_shape=jax.ShapeDtypeStruct((t, ATT_W), BF16),
        scratch_shapes=[pltpu.VMEM((nbp, w), F32), pltpu.VMEM((ATT_HEADS, nbp, tq), F32),
                        pltpu.VMEM((ATT_HEADS, MOBA_BLOCK, tq), F32)],
        compiler_params=_cparams(("arbitrary", "arbitrary"), VMEM_LIMIT),
        name="moba_attention",
    )(cast_t, cast, cast_t)


def _float_key(x):
    b = lax.bitcast_convert_type(x, I32)
    return jnp.where(b < 0, b ^ jnp.int32(0x7FFFFFFF), b)


def _dsa_select_kernel(qit_ref, ki_ref, misct_ref, bias_ref, key_ref):
    i = pl.program_id(0)
    tq = qit_ref.shape[1]
    tk = ATT_T
    n_chunks = ki_ref.shape[0] // tk
    n_live = i + 1

    qit = qit_ref[...]
    row = lax.broadcasted_iota(I32, (IDX_HEADS * IDX_DIM, 1), 0)
    zero = jnp.zeros_like(qit)
    qh = [jnp.where((row >= h * IDX_DIM) & (row < (h + 1) * IDX_DIM), qit, zero) for h in range(IDX_HEADS)]
    wrow = [misct_ref[MISC_WI + h:MISC_WI + h + 1, :] for h in range(IDX_HEADS)]
    visible = _causal_t(tk, tq)

    def score_body(j, c):
        kic = ki_ref[pl.ds(pl.multiple_of(j * tk, tk), tk), :]
        acc = jnp.zeros((tk, tq), F32)
        for h in range(IDX_HEADS):
            acc = acc + wrow[h] * jnp.maximum(_dot(kic, qh[h]), 0.0)
        acc = jnp.where(jnp.logical_or(j < i, visible), acc, -jnp.inf)
        key_ref[j] = _float_key(acc)
        return c

    lax.fori_loop(0, n_live, score_body, 0)

    def count_ge(cand):
        def body(j, cnt):
            for r in range(tk // 8):
                cnt = cnt + jnp.where(key_ref[j, r * 8:(r + 1) * 8, :] >= cand, 1.0, 0.0)
            return cnt
        cnt = lax.fori_loop(0, n_live, body, jnp.zeros((8, tq), F32))
        return jnp.sum(cnt, axis=0, keepdims=True)

    topk = float(DSA_TOPK)
    int_min = jnp.int32(-2 ** 31)
    nonneg = count_ge(jnp.zeros((1, tq), I32)) >= topk
    prefix0 = jnp.where(nonneg, jnp.int32(0), int_min)

    def bit_body(b, prefix):
        cand = prefix | (jnp.int32(1) << (30 - b))
        return jnp.where(count_ge(cand) >= topk, cand, prefix)

    thr = lax.fori_loop(0, 31, bit_body, prefix0)

    need = topk - count_ge(thr + 1)
    lower = jnp.where(lax.broadcasted_iota(I32, (tk, tk), 0) >= lax.broadcasted_iota(I32, (tk, tk), 1),
                      1.0, 0.0).astype(BF16)

    def bias_body(j, seen):
        kk = key_ref[j]
        eq = kk == thr
        rank = _dot(lower, jnp.where(eq, 1.0, 0.0).astype(BF16)) + seen
        keep = ((kk > thr) | (eq & (rank <= need))) & jnp.logical_or(j < i, visible)
        bias_ref[pl.ds(pl.multiple_of(j * tk, tk), tk), :] = jnp.where(keep, 0.0, NEG).astype(BF16)
        return rank[tk - 1:tk, :]

    lax.fori_loop(0, n_live, bias_body, jnp.zeros((1, tq), F32))

    def fill_body(j, c):
        bias_ref[pl.ds(pl.multiple_of(j * tk, tk), tk), :] = jnp.full((tk, tq), NEG, BF16)
        return c

    lax.fori_loop(n_live, n_chunks, fill_body, 0)


def dsa_select(cast, cast_t, misc_t):
    t = cast.shape[0]
    tq = ATT_T
    w = IDX_HEADS * IDX_DIM
    return pl.pallas_call(
        _dsa_select_kernel,
        grid=(t // tq,),
        in_specs=[pl.BlockSpec((None, w, tq), lambda i: (i, COL_CQI // w, 0)),
                  pl.BlockSpec((t, w), lambda i: (0, COL_KI4 // w)),
                  pl.BlockSpec((None, LANES, tq), lambda i: (i, 0, 0))],
        out_specs=pl.BlockSpec((None, t, tq), lambda i: (i, 0, 0)),
        out_shape=jax.ShapeDtypeStruct((t // tq, t, tq), BF16),
        scratch_shapes=[pltpu.VMEM((t // ATT_T, ATT_T, tq), I32)],
        compiler_params=_cparams(("arbitrary",), VMEM_LIMIT),
        name="dsa_select",
    )(cast_t, cast, misc_t)


def _dsa_attn_kernel(qt_ref, k_ref, vt_ref, bias_ref, o_ref, s_ref):
    i = pl.program_id(1)
    tq = qt_ref.shape[1]
    tk = ATT_T
    nh = s_ref.shape[0]
    qh = _head_rows(qt_ref[...])

    def score_fn(n):
        off = pl.multiple_of(n * tk, tk)
        kb = k_ref[pl.ds(off, tk), :]
        bias = bias_ref[pl.ds(off, tk), :].astype(F32)
        return [_dot(kb[:, _pair_cols(h)], qh[h]) + bias for h in range(nh)]

    def vt_fn(n, h):
        return vt_ref[n, h * HEAD_DIM:(h + 1) * HEAD_DIM, :]

    o_ref[...] = _attention_tile(i, score_fn, vt_fn, s_ref, tq, None).astype(BF16)


def dsa_attention(cast, cast_t, bias):
    t = cast.shape[0]
    tq = ATT_T
    w = ATT_HEADS * HEAD_DIM
    return pl.pallas_call(
        _dsa_attn_kernel,
        grid=(N_HEADS // ATT_HEADS, t // tq),
        in_specs=[pl.BlockSpec((None, w, tq), lambda g, i: (i, COL_CQ // w + g, 0)),
                  pl.BlockSpec((t, w), lambda g, i: (0, COL_CK // w + g)),
                  pl.BlockSpec((t // tq, w, tq), lambda g, i: (0, COL_CV // w + g, 0)),
                  pl.BlockSpec((None, t, tq), lambda g, i: (i, 0, 0))],
        out_specs=pl.BlockSpec((tq, w), lambda g, i: (i, g)),
        out_shape=jax.ShapeDtypeStruct((t, ATT_W), BF16),
        scratch_shapes=[pltpu.VMEM((ATT_HEADS, ATT_T, tq), F32)],
        compiler_params=_cparams(("arbitrary", "arbitrary"), VMEM_LIMIT),
        name="dsa_attention",
    )(cast_t, cast, cast_t, bias)


def _rms(x, w):
    return x * lax.rsqrt(jnp.mean(x * x, axis=-1, keepdims=True) + LN_EPS) * w


def _mla_prep_kernel(cq_ref, ckv_ref, misc_ref, qn_ref, kvn_ref, wq_ref, wk_ref, wv_ref,
                     c_ref, s1_ref, s2_ref, qt_ref, k_ref, vt_ref, misct_ref):
    c, s1, s2 = c_ref[...], s1_ref[...], s2_ref[...]
    cq = _rms(cq_ref[...], qn_ref[...]).astype(BF16)
    ckv = _rms(ckv_ref[...], kvn_ref[...]).astype(BF16)
    misc = misc_ref[...]
    misct_ref[...] = misc.T
    lane = lax.broadcasted_iota(I32, (1, LANES), 1)
    kpe = pltpu.roll(jnp.where(lane < MLA_ROPE, misc, 0.0), MLA_NOPE, 1)
    kpe = _rot_apply(kpe, c, s1, s2, MLA_ROPE // 2)
    q = _dot(cq, wq_ref[...])
    k = _dot(ckv, wk_ref[...])
    for h in range(N_HEADS):
        sl = slice(h * LANES, (h + 1) * LANES)
        qt_ref[sl, :] = _rot_apply(q[:, sl], c, s1, s2, MLA_ROPE // 2).T.astype(BF16)
        k_ref[:, sl] = (k[:, sl] + kpe).astype(BF16)
    v = _dot(ckv, wv_ref[...])
    for g in range(ATT_W // LANES):
        sl = slice(g * LANES, (g + 1) * LANES)
        vt_ref[sl, :] = v[:, sl].T.astype(BF16)


def mla_prep(proj, qn, kvn, wq, wk, wv, tabs):
    t = proj.shape[0]
    tm = ATT_T
    full = lambda a: pl.BlockSpec(a.shape, lambda i: (0,) * a.ndim)
    row = lambda w, col: pl.BlockSpec((tm, w), lambda i: (i, col // w))
    tile_t = lambda n: pl.BlockSpec((None, n, tm), lambda i: (i, 0, 0))
    return pl.pallas_call(
        _mla_prep_kernel,
        grid=(t // tm,),
        in_specs=[row(MLA_Q_RANK, COL_CQL), row(MLA_KV_RANK, COL_CKV), row(LANES, COL_MISC),
                  full(qn), full(kvn), full(wq), full(wk), full(wv),
                  row(LANES, 0), row(LANES, 0), row(LANES, 0)],
        out_specs=[tile_t(N_HEADS * LANES),
                   pl.BlockSpec((tm, N_HEADS * LANES), lambda i: (i, 0)),
                   tile_t(ATT_W), tile_t(LANES)],
        out_shape=[jax.ShapeDtypeStruct((t // tm, N_HEADS * LANES, tm), BF16),
                   jax.ShapeDtypeStruct((t, N_HEADS * LANES), BF16),
                   jax.ShapeDtypeStruct((t // tm, ATT_W, tm), BF16),
                   jax.ShapeDtypeStruct((t // tm, LANES, tm), F32)],
        compiler_params=_cparams(("arbitrary",), VMEM_LIMIT),
        name="mla_prep",
    )(proj, proj, proj, qn, kvn, wq, wk, wv, *tabs)


def _mla_attn_kernel(qt_ref, k_ref, vt_ref, o_ref, s_ref):
    i = pl.program_id(1)
    tq = qt_ref.shape[1]
    tk = ATT_T
    nh = s_ref.shape[0]
    qh = [qt_ref[h * LANES:(h + 1) * LANES, :] for h in range(nh)]

    def score_fn(n):
        off = pl.multiple_of(n * tk, tk)
        return [_dot(k_ref[pl.ds(off, tk), h * LANES:(h + 1) * LANES], qh[h]) for h in range(nh)]

    def vt_fn(n, h):
        return vt_ref[n, h * HEAD_DIM:(h + 1) * HEAD_DIM, :]

    o_ref[...] = _attention_tile(i, score_fn, vt_fn, s_ref, tq, _causal_t(tk, tq)).astype(BF16)


def mla_attention(qt, k, vt):
    t = k.shape[0]
    tq = ATT_T
    wqk = ATT_HEADS * LANES
    wv = ATT_HEADS * HEAD_DIM
    return pl.pallas_call(
        _mla_attn_kernel,
        grid=(N_HEADS // ATT_HEADS, t // tq),
        in_specs=[pl.BlockSpec((None, wqk, tq), lambda g, i: (i, g, 0)),
                  pl.BlockSpec((t, wqk), lambda g, i: (0, g)),
                  pl.BlockSpec((t // tq, wv, tq), lambda g, i: (0, g, 0))],
        out_specs=pl.BlockSpec((tq, wv), lambda g, i: (i, g)),
        out_shape=jax.ShapeDtypeStruct((t, ATT_W), BF16),
        scratch_shapes=[pltpu.VMEM((ATT_HEADS, ATT_T, tq), F32)],
        compiler_params=_cparams(("arbitrary", "arbitrary"), VMEM_LIMIT),
        name="mla_attention",
    )(qt, k, vt)


def _silu(x):
    return x / (1.0 + jnp.exp(-x))


def _softplus(x):
    return jnp.maximum(x, 0.0) + jnp.log1p(jnp.exp(-jnp.abs(x)))


def _ssd_kernel(z_ref, xbc_ref, misc_ref, cw_ref, cb_ref, dtb_ref, a_ref, dsk_ref, nw_ref, exp_ref,
                o_ref, ext_ref, state_ref):
    c = pl.program_id(0)
    L = SSM_CHUNK
    half = SSM_INNER // SSM_GROUPS

    @pl.when(c == 0)
    def _():
        ext_ref[0:8, :] = jnp.zeros((8, SSM_CONV_CH), F32)
        state_ref[...] = jnp.zeros(state_ref.shape, F32)

    ext_ref[8:8 + L, :] = xbc_ref[...]
    conv = jnp.zeros((L, SSM_CONV_CH), F32) + cb_ref[...]
    for kk in range(SSM_CONV):
        conv = conv + cw_ref[kk:kk + 1, :] * ext_ref[pl.ds(8 - (SSM_CONV - 1) + kk, L), :]
    tail = ext_ref[L:L + 8, :]
    ext_ref[0:8, :] = tail
    xbc = _silu(conv)
    xs = xbc[:, 0:SSM_INNER]
    bm = xbc[:, SSM_INNER:SSM_INNER + SSM_GROUPS * SSM_STATE]
    cm = xbc[:, SSM_INNER + SSM_GROUPS * SSM_STATE:]

    dt = _softplus(misc_ref[...] + dtb_ref[...])
    da = dt * a_ref[...]
    r = lax.broadcasted_iota(I32, (L, L), 0)
    cc = lax.broadcasted_iota(I32, (L, L), 1)
    tri = r >= cc
    a_cum = _dot_hi(jnp.where(tri, 1.0, 0.0), da)
    a_cum_t = a_cum.T
    expand = exp_ref[...]
    dt_full = _dot_hi(dt, expand)
    acum_full = _dot_hi(a_cum, expand)
    a_last = acum_full[L - 1:L, :]
    xdt = xs * dt_full
    xdt_b = xdt.astype(BF16)

    ys = []
    for g in range(SSM_GROUPS):
        bg = bm[:, g * SSM_STATE:(g + 1) * SSM_STATE].astype(BF16)
        cg = cm[:, g * SSM_STATE:(g + 1) * SSM_STATE].astype(BF16)
        cb = _dot_nt(cg, bg)
        hpg = SSM_HEADS // SSM_GROUPS
        cols = []
        for hh in range(hpg):
            h = g * hpg + hh
            diff = a_cum[:, MISC_DT + h:MISC_DT + h + 1] - a_cum_t[MISC_DT + h:MISC_DT + h + 1, :]
            seg = jnp.where(tri, jnp.exp(jnp.where(tri, diff, 0.0)), 0.0)
            sc = (cb * seg).astype(BF16)
            pair = (h // 2) * LANES
            yp = _dot(sc, xdt_b[:, pair:pair + LANES])
            cols.append(yp[:, (h % 2) * SSM_HEAD_DIM:(h % 2 + 1) * SSM_HEAD_DIM])
        y_diag = jnp.concatenate(cols, axis=1)
        gs = slice(g * half, (g + 1) * half)
        st = state_ref[:, gs]
        y_off = _dot(cg, st.astype(BF16)) * jnp.exp(acum_full[:, gs])
        ys.append(y_diag + y_off)
        xw = (xdt[:, gs] * jnp.exp(a_last[:, gs] - acum_full[:, gs])).astype(BF16)
        new = _dot(bm[:, g * SSM_STATE:(g + 1) * SSM_STATE].T.astype(BF16), xw)
        state_ref[:, gs] = st * jnp.exp(a_last[:, gs]) + new
    y = jnp.concatenate(ys, axis=1) + xs * dsk_ref[...]
    y = y * _silu(z_ref[...])
    o_ref[...] = _rms(y, nw_ref[...]).astype(BF16)


def ssd_mixer(proj, conv_w, conv_b, dtb_row, a_row, dsk_row, norm_w, expand):
    t = proj.shape[0]
    L = SSM_CHUNK
    full = lambda a: pl.BlockSpec(a.shape, lambda c: (0,) * a.ndim)
    return pl.pallas_call(
        _ssd_kernel,
        grid=(t // L,),
        in_specs=[pl.BlockSpec((L, SSM_INNER), lambda c: (c, COL_MZ // SSM_INNER)),
                  pl.BlockSpec((L, SSM_CONV_CH), lambda c: (c, COL_XBC // SSM_CONV_CH)),
                  pl.BlockSpec((L, LANES), lambda c: (c, COL_MISC // LANES)),
                  full(conv_w), full(conv_b), full(dtb_row), full(a_row), full(dsk_row), full(norm_w),
                  full(expand)],
        out_specs=pl.BlockSpec((L, SSM_INNER), lambda c: (c, 0)),
        out_shape=jax.ShapeDtypeStruct((t, SSM_INNER), BF16),
        scratch_shapes=[pltpu.VMEM((L + 8, SSM_CONV_CH), F32),
                        pltpu.VMEM((SSM_STATE, SSM_INNER), F32)],
        compiler_params=_cparams(("arbitrary",), VMEM_LIMIT),
        name="ssd_mixer",
    )(proj, proj, proj, conv_w, conv_b, dtb_row, a_row, dsk_row, norm_w, expand)


def _merge_kernel(x_ref, wg_ref, bg_ref, oa_ref, ob_ref, oc_ref, od_ref, wb_ref, o_ref):
    x = x_ref[...]
    acc = None
    for g, br in enumerate((oa_ref, ob_ref, oc_ref, od_ref)):
        logit = _dot(x, wg_ref[g]) + bg_ref[g:g + 1, :]
        gate = 1.0 / (1.0 + jnp.exp(-logit))
        term = gate * _dot(br[...], wb_ref[g])
        acc = term if acc is None else acc + term
    o_ref[...] = acc.astype(BF16)


def gated_merge(x_bf, wg, bg, branches, wb, tm=1024, tn=256):
    t = x_bf.shape[0]
    br_spec = pl.BlockSpec((tm, BRANCH_W), lambda j, i: (i, 0))
    return pl.pallas_call(
        _merge_kernel,
        grid=(D_MODEL // tn, t // tm),
        in_specs=[pl.BlockSpec((tm, D_MODEL), lambda j, i: (i, 0)),
                  pl.BlockSpec((N_BRANCH, D_MODEL, tn), lambda j, i: (0, 0, j)),
                  pl.BlockSpec((N_BRANCH, tn), lambda j, i: (0, j)),
                  br_spec, br_spec, br_spec, br_spec,
                  pl.BlockSpec((N_BRANCH, BRANCH_W, tn), lambda j, i: (0, 0, j))],
        out_specs=pl.BlockSpec((tm, tn), lambda j, i: (i, j)),
        out_shape=jax.ShapeDtypeStruct((t, D_MODEL), BF16),
        compiler_params=_cparams(("arbitrary", "arbitrary"), VMEM_LIMIT),
        name="gated_merge",
    )(x_bf, wg, bg, *branches, wb)


def _layer_norm(v, g, b):
    mu = jnp.mean(v, axis=-1, keepdims=True)
    d = v - mu
    var = jnp.mean(d * d, axis=-1, keepdims=True)
    return d * lax.rsqrt(var + LN_EPS) * g + b


def _outproj_kernel(m_ref, w_ref, x_ref, g_ref, b_ref, rw_ref, rb_ref, x1_ref, x1b_ref, idx_ref, gate_ref):
    mixed = _dot(m_ref[...], w_ref[...])
    x1 = _layer_norm(DN_ALPHA * x_ref[...] + mixed, g_ref[...], b_ref[...])
    x1_ref[...] = x1
    x1b = x1.astype(BF16)
    x1b_ref[...] = x1b
    logits = _dot(x1b, rw_ref[...]) + rb_ref[...]
    lane = lax.broadcasted_iota(I32, logits.shape, 1)
    g = logits
    idx_out = jnp.zeros(logits.shape, I32)
    val_out = jnp.zeros(logits.shape, F32)
    vals = []
    for kk in range(TOP_K):
        m = jnp.max(g, axis=1, keepdims=True)
        idx = jnp.min(jnp.where(g == m, lane, LANES), axis=1, keepdims=True)
        g = jnp.where(lane == idx, -jnp.inf, g)
        idx_out = jnp.where(lane == kk, idx, idx_out)
        vals.append(m)
    es = [jnp.exp(v - vals[0]) for v in vals]
    den = es[0] + es[1] + es[2] + es[3]
    for kk in range(TOP_K):
        val_out = jnp.where(lane == kk, es[kk] / den, val_out)
    idx_ref[...] = idx_out
    gate_ref[...] = val_out


def outproj_ln_router(merged, w_out, x, g, b, rw, rb, tm=256):
    t = x.shape[0]
    full = lambda a: pl.BlockSpec(a.shape, lambda i: (0,) * a.ndim)
    row = lambda w: pl.BlockSpec((tm, w), lambda i: (i, 0))
    return pl.pallas_call(
        _outproj_kernel,
        grid=(t // tm,),
        in_specs=[row(D_MODEL), full(w_out), row(D_MODEL), full(g), full(b), full(rw), full(rb)],
        out_specs=[row(D_MODEL), row(D_MODEL), row(LANES), row(LANES)],
        out_shape=[jax.ShapeDtypeStruct((t, D_MODEL), F32), jax.ShapeDtypeStruct((t, D_MODEL), BF16),
                   jax.ShapeDtypeStruct((t, LANES), I32), jax.ShapeDtypeStruct((t, LANES), F32)],
        compiler_params=_cparams(("arbitrary",), VMEM_LIMIT),
        name="outproj_ln_router",
    )(merged, w_out, x, g, b, rw, rb)


def _start_row_gather(idx_ref, src_hbm, dst, sem, rows):
    for r in range(rows):
        pltpu.make_async_copy(src_hbm.at[pl.ds(idx_ref[0, r], 1)], dst.at[pl.ds(r, 1)], sem).start(priority=r % 2)


def _wait_row_gather(src_hbm, dst, sem, rows):
    pltpu.make_async_copy(src_hbm.at[pl.ds(0, rows)], dst, sem).wait()


def _expert_kernel(be_ref, nu_ref, tok_ref, tokn_ref, x_hbm, wgu_ref, bgu_ref, wd_ref, bd_ref, o_ref, xbuf, sem):
    blk = pl.program_id(0)
    n_used = nu_ref[0]
    slot = lax.rem(blk, 2)

    @pl.when(jnp.logical_and(blk == 0, n_used > 0))
    def _():
        _start_row_gather(tok_ref, x_hbm, xbuf.at[0], sem.at[0], MOE_ROWS)

    @pl.when(blk + 1 < n_used)
    def _():
        _start_row_gather(tokn_ref, x_hbm, xbuf.at[1 - slot], sem.at[1 - slot], MOE_ROWS)

    @pl.when(blk < n_used)
    def _():
        _wait_row_gather(x_hbm, xbuf.at[slot], sem.at[slot], MOE_ROWS)
        gu = _dot(xbuf[slot].astype(BF16), wgu_ref[0]) + bgu_ref[0]
        glu = jnp.minimum(gu[:, :D_FF], SWIGLU_LIMIT)
        lin = jnp.clip(gu[:, D_FF:], -SWIGLU_LIMIT, SWIGLU_LIMIT)
        act = glu / (1.0 + jnp.exp(-SWIGLU_ALPHA * glu)) * (lin + 1.0)
        o_ref[...] = _dot(act.astype(BF16), wd_ref[0]) + bd_ref[0]

    @pl.when(blk >= nu_ref[0])
    def _():
        o_ref[...] = jnp.zeros(o_ref.shape, F32)


def expert_ffn(block_e, n_used, row_tok, x, wgu, bgu, wd, bd):
    n_blocks = row_tok.shape[0]
    n_rows = n_blocks * MOE_ROWS
    grid_spec = pltpu.PrefetchScalarGridSpec(
        num_scalar_prefetch=2,
        grid=(n_blocks,),
        in_specs=[pl.BlockSpec((None, 1, MOE_ROWS), lambda b, be, nu: (b, 0, 0), memory_space=pltpu.SMEM),
                  pl.BlockSpec((None, 1, MOE_ROWS), lambda b, be, nu: (jnp.minimum(b + 1, n_blocks - 1), 0, 0),
                               memory_space=pltpu.SMEM),
                  pl.BlockSpec(memory_space=pl.ANY),
                  pl.BlockSpec((1, D_MODEL, 2 * D_FF), lambda b, be, nu: (be[b], 0, 0)),
                  pl.BlockSpec((1, 1, 2 * D_FF), lambda b, be, nu: (be[b], 0, 0)),
                  pl.BlockSpec((1, D_FF, D_MODEL), lambda b, be, nu: (be[b], 0, 0)),
                  pl.BlockSpec((1, 1, D_MODEL), lambda b, be, nu: (be[b], 0, 0))],
        out_specs=pl.BlockSpec((MOE_ROWS, D_MODEL), lambda b, be, nu: (b, 0)),
        scratch_shapes=[pltpu.VMEM((2, MOE_ROWS, D_MODEL), F32), pltpu.SemaphoreType.DMA((2,))],
    )
    return pl.pallas_call(
        _expert_kernel,
        grid_spec=grid_spec,
        out_shape=jax.ShapeDtypeStruct((n_rows, D_MODEL), F32),
        compiler_params=_cparams(("arbitrary",), VMEM_LIMIT),
        name="expert_ffn",
    )(block_e, n_used, row_tok, row_tok, x, wgu, bgu, wd, bd)


COMBINE_TM = 128


def _combine_kernel(pos_ref, posn_ref, out_hbm, gate_ref, x_ref, g_ref, b_ref, o_ref, ob_ref, rbuf, sem):
    i = pl.program_id(0)
    tm = x_ref.shape[0]
    rows = TOP_K * tm
    slot = lax.rem(i, 2)

    @pl.when(i == 0)
    def _():
        _start_row_gather(pos_ref, out_hbm, rbuf.at[0], sem.at[0], rows)

    @pl.when(i + 1 < pl.num_programs(0))
    def _():
        _start_row_gather(posn_ref, out_hbm, rbuf.at[1 - slot], sem.at[1 - slot], rows)

    _wait_row_gather(out_hbm, rbuf.at[slot], sem.at[slot], rows)
    gate = gate_ref[...]
    y = jnp.zeros(x_ref.shape, F32)
    for kk in range(TOP_K):
        y = y + gate[:, kk:kk + 1] * rbuf[slot, kk * tm:(kk + 1) * tm, :]
    x2 = _layer_norm(DN_ALPHA * x_ref[...] + y, g_ref[...], b_ref[...])
    o_ref[...] = x2
    ob_ref[...] = x2.astype(BF16)


def combine_ln(pos_km, out, gates, x1, g, b):
    t = x1.shape[0]
    tm = COMBINE_TM
    n_tiles = t // tm
    full = lambda a: pl.BlockSpec(a.shape, lambda i: (0,) * a.ndim)
    row = lambda w: pl.BlockSpec((tm, w), lambda i: (i, 0))
    return pl.pallas_call(
        _combine_kernel,
        grid=(n_tiles,),
        in_specs=[pl.BlockSpec((None, 1, TOP_K * tm), lambda i: (i, 0, 0), memory_space=pltpu.SMEM),
                  pl.BlockSpec((None, 1, TOP_K * tm), lambda i: (jnp.minimum(i + 1, n_tiles - 1), 0, 0),
                               memory_space=pltpu.SMEM),
                  pl.BlockSpec(memory_space=pl.ANY), row(LANES), row(D_MODEL), full(g), full(b)],
        out_specs=[row(D_MODEL), row(D_MODEL)],
        out_shape=[jax.ShapeDtypeStruct((t, D_MODEL), F32), jax.ShapeDtypeStruct((t, D_MODEL), BF16)],
        scratch_shapes=[pltpu.VMEM((2, TOP_K * tm, D_MODEL), F32), pltpu.SemaphoreType.DMA((2,))],
        compiler_params=_cparams(("arbitrary",), VMEM_LIMIT),
        name="combine_ln",
    )(pos_km, pos_km, out, gates, x1, g, b)


def _rope_tables(positions):
    pos = positions.reshape(-1).astype(F32)

    def cs(dim, theta):
        inv = theta ** (-jnp.arange(0, dim, 2, dtype=F32) / dim)
        ang = pos[:, None] * inv
        return jnp.cos(ang), jnp.sin(ang)

    t = pos.shape[0]
    cos_p, sin_p = cs(ROT_DIM, ROPE_THETA)
    half = ROT_DIM // 2
    one = jnp.ones((t, HEAD_DIM - ROT_DIM), F32)
    zero = jnp.zeros((t, HEAD_DIM - ROT_DIM), F32)
    zh = jnp.zeros((t, half), F32)
    c64 = jnp.concatenate([cos_p, cos_p, one], axis=1)
    s1_64 = jnp.concatenate([-sin_p, zh, zero], axis=1)
    s2_64 = jnp.concatenate([zh, sin_p, zero], axis=1)
    tabs_p = tuple(jnp.tile(a, (1, 2)) for a in (c64, s1_64, s2_64))

    cos_m, sin_m = cs(MLA_ROPE, MLA_THETA)
    hm = MLA_ROPE // 2
    pre1 = jnp.ones((t, MLA_NOPE), F32)
    pre0 = jnp.zeros((t, MLA_NOPE), F32)
    post1 = jnp.ones((t, LANES - MLA_NOPE - MLA_ROPE), F32)
    post0 = jnp.zeros((t, LANES - MLA_NOPE - MLA_ROPE), F32)
    zm = jnp.zeros((t, hm), F32)
    c_m = jnp.concatenate([pre1, cos_m, cos_m, post1], axis=1)
    s1_m = jnp.concatenate([pre0, -sin_m, zm, post0], axis=1)
    s2_m = jnp.concatenate([pre0, zm, sin_m, post0], axis=1)
    return tabs_p, (c_m, s1_m, s2_m)


def _prep_in_weights(w_in):
    pts = np.cumsum((0,) + SPLIT_SIZES)
    seg = lambda k: w_in[:, pts[k]:pts[k + 1]]
    (a_q, a_k, a_v, m_z, m_xbc, m_dt, c_q, c_k, c_v, c_qi, c_ki, c_wi, d_cq, d_ckv, d_kpe) = [seg(k) for k in range(15)]
    scale = HEAD_DIM ** -0.5
    zeros = lambda n: jnp.zeros((D_MODEL, n), w_in.dtype)
    cols = [a_q * scale, a_k, a_v, c_q * scale, c_k, c_v, c_qi, c_ki, c_ki, c_ki, c_ki,
            m_z, m_xbc, d_ckv, d_cq, d_kpe, m_dt, c_wi * IDX_SCALE]
    small = jnp.concatenate(cols, axis=1)
    small = jnp.concatenate([small, zeros(N_SMALL - small.shape[1])], axis=1).astype(BF16)
    wg = w_in[:, pts[15]:].reshape(D_MODEL, N_BRANCH, D_MODEL).transpose(1, 0, 2).astype(BF16)
    return small, wg


def _prep_mla_weights(w_uq, w_ukv):
    dq = MLA_NOPE + MLA_ROPE
    wq = w_uq.reshape(MLA_Q_RANK, N_HEADS, dq) * (dq ** -0.5)
    wq = jnp.pad(wq, ((0, 0), (0, 0), (0, LANES - dq))).reshape(MLA_Q_RANK, N_HEADS * LANES)
    wkv = w_ukv.reshape(MLA_KV_RANK, N_HEADS, MLA_NOPE + MLA_V)
    wk = jnp.pad(wkv[:, :, :MLA_NOPE], ((0, 0), (0, 0), (0, LANES - MLA_NOPE))).reshape(MLA_KV_RANK, N_HEADS * LANES)
    wv = wkv[:, :, MLA_NOPE:].reshape(MLA_KV_RANK, N_HEADS * MLA_V)
    return wq.astype(BF16), wk.astype(BF16), wv.astype(BF16)


def _misc_row(v):
    return jnp.zeros((1, LANES), F32).at[0, MISC_DT:MISC_DT + SSM_HEADS].set(v.astype(F32))


def _route(top_idx, t):
    tk = t * TOP_K
    flat_e = top_idx.reshape(-1)
    onehot = (flat_e[:, None] == jnp.arange(N_EXPERTS)[None, :]).astype(I32)
    seen = jnp.cumsum(onehot, axis=0)
    counts = seen[-1]
    padded = ((counts + MOE_ROWS - 1) // MOE_ROWS) * MOE_ROWS
    pad_end = jnp.cumsum(padded)
    pad_start = pad_end - padded
    rank = jnp.sum(onehot * seen, axis=1) - 1
    pos = (pad_start[flat_e] + rank).astype(I32)
    n_blocks = tk // MOE_ROWS + N_EXPERTS
    n_rows = n_blocks * MOE_ROWS
    row_tok = jnp.zeros((n_rows,), I32).at[pos].set(jnp.arange(tk, dtype=I32) // TOP_K)
    block_start = jnp.arange(n_blocks) * MOE_ROWS
    block_e = jnp.minimum(jnp.sum(pad_end[None, :] <= block_start[:, None], axis=1), N_EXPERTS - 1).astype(I32)
    n_used = (pad_end[-1] // MOE_ROWS).astype(I32).reshape(1)
    return row_tok, pos, block_e, n_used


def _layer(x, x_bf, tabs_p, tabs_m, flags, expand, p):
    t = x.shape[0]
    w_small, wg = _prep_in_weights(p["w_in"])
    proj = matmul_bf16(x_bf, w_small)
    cast, cast_t = rope_cast(proj, flags, tabs_p)

    o_a = moba_attention(cast, cast_t)
    o_b = ssd_mixer(proj, p["conv_w"], p["conv_b"].reshape(1, -1), _misc_row(p["dt_bias"]),
                    _misc_row(-jnp.exp(p["a_log"].astype(F32))),
                    jnp.repeat(p["d_skip"].astype(F32), SSM_HEAD_DIM).reshape(1, -1),
                    p["ssm_norm_w"].reshape(1, -1), expand)
    wq, wk, wv = _prep_mla_weights(p["w_uq"], p["w_ukv"])
    qt_m, k_m, vt_m, misc_t = mla_prep(proj, p["q_norm_w"].reshape(1, -1), p["kv_norm_w"].reshape(1, -1), wq, wk, wv,
                                       tabs_m)
    o_c = dsa_attention(cast, cast_t, dsa_select(cast, cast_t, misc_t))
    o_d = mla_attention(qt_m, k_m, vt_m)

    merged = gated_merge(x_bf, wg, p["b_gate"], (o_a, o_b, o_c, o_d), p["w_branch"].astype(BF16))
    rw = jnp.pad(p["router_w"], ((0, 0), (0, LANES - N_EXPERTS))).astype(BF16)
    rb = jnp.concatenate([p["router_b"].astype(F32), jnp.full((LANES - N_EXPERTS,), -jnp.inf, F32)]).reshape(1, -1)
    x1, x1_bf, top_idx, gates = outproj_ln_router(merged, p["w_out"].astype(BF16), x, p["ln1_g"].reshape(1, -1),
                                                  p["ln1_b"].reshape(1, -1), rw, rb)

    row_tok, pos, block_e, n_used = _route(top_idx[:, :TOP_K], t)
    out = expert_ffn(block_e, n_used, row_tok.reshape(-1, 1, MOE_ROWS), x1, p["w_gate_up"].astype(BF16),
                     p["b_gate_up"][:, None, :], p["w_down"].astype(BF16), p["b_down"][:, None, :])
    pos_km = pos.reshape(t // COMBINE_TM, COMBINE_TM, TOP_K).transpose(0, 2, 1).reshape(-1, 1, TOP_K * COMBINE_TM)
    return combine_ln(pos_km, out, gates, x1, p["ln2_g"].reshape(1, -1), p["ln2_b"].reshape(1, -1))


def kernel(x, positions, w_in, b_gate, conv_w, conv_b, dt_bias, a_log, d_skip, ssm_norm_w, q_norm_w, w_uq,
           kv_norm_w, w_ukv, w_branch, w_out, ln1_g, ln1_b, router_w, router_b, w_gate_up, b_gate_up, w_down,
           b_down, ln2_g, ln2_b):
    b, s, d = x.shape
    assert b == 1 and d == D_MODEL
    names = ("w_in", "b_gate", "conv_w", "conv_b", "dt_bias", "a_log", "d_skip", "ssm_norm_w", "q_norm_w", "w_uq",
             "kv_norm_w", "w_ukv", "w_branch", "w_out", "ln1_g", "ln1_b", "router_w", "router_b", "w_gate_up",
             "b_gate_up", "w_down", "b_down", "ln2_g", "ln2_b")
    stacked = (w_in, b_gate, conv_w, conv_b, dt_bias, a_log, d_skip, ssm_norm_w, q_norm_w, w_uq, kv_norm_w, w_ukv,
               w_branch, w_out, ln1_g, ln1_b, router_w, router_b, w_gate_up, b_gate_up, w_down, b_down, ln2_g, ln2_b)
    tabs_p, tabs_m = _rope_tables(positions)
    flags = jnp.array([1, 1, 0, 1, 1, 0, 1], I32)
    head_of_lane = jnp.arange(SSM_INNER) // SSM_HEAD_DIM
    expand = (jnp.arange(LANES)[:, None] == (MISC_DT + head_of_lane)[None, :]).astype(F32)

    xt = x.reshape(s, d)
    xt_bf = xt.astype(BF16)
    for l in range(DEPTH):
        p = {n: a[l] for n, a in zip(names, stacked)}
        xt, xt_bf = _layer(xt, xt_bf, tabs_p, tabs_m, flags, expand, p)
    return xt.reshape(b, s, d)
```

```python
import functools
import math

import jax
import jax.numpy as jnp
import numpy as np
from jax import lax
from jax.experimental import pallas as pl
from jax.experimental.pallas import tpu as pltpu

F32 = jnp.float32
BF16 = jnp.bfloat16
I32 = jnp.int32

D_MODEL = 2048
DEPTH = 4
HEAD_DIM = 64
ROT_DIM = HEAD_DIM // 4
ROPE_THETA = 500000.0
N_HEADS = 8
ATT_W = N_HEADS * HEAD_DIM
MOBA_BLOCK = 256
MOBA_TOPK = 3
SSM_HEADS = 8
SSM_HEAD_DIM = 64
SSM_INNER = 512
SSM_STATE = 128
SSM_GROUPS = 2
SSM_CONV = 4
SSM_CONV_CH = 1024
SSM_CHUNK = 256
IDX_HEADS = 4
IDX_DIM = 64
DSA_TOPK = 256
IDX_SCALE = (IDX_HEADS * IDX_DIM) ** -0.5
MLA_Q_RANK = 384
MLA_KV_RANK = 256
MLA_NOPE = 64
MLA_ROPE = 32
MLA_V = 64
MLA_THETA = 10000.0
N_BRANCH = 4
BRANCH_W = 512
SPLIT_SIZES = (512, 512, 512, 512, 1024, 8, 512, 512, 512, 256, 64, 4, 384, 256, 32, N_BRANCH * D_MODEL)
N_EXPERTS = 32
TOP_K = 4
D_FF = 768
SWIGLU_ALPHA = 1.702
SWIGLU_LIMIT = 7.0
DN_ALPHA = (2 * DEPTH) ** 0.25
LN_EPS = 1e-5

LANES = 128
NEG = -1e30
VMEM_LIMIT = 56 * 1024 * 1024

COL_AQ, COL_AK, COL_AV = 0, 512, 1024
COL_CQ, COL_CK, COL_CV = 1536, 2048, 2560
COL_CQI, COL_KI4 = 3072, 3328
COL_MZ = 3584
COL_XBC = 4096
COL_CKV = 5120
COL_CQL = 5376
COL_MISC = 5760
N_SMALL = 6144
N_CAST = 3584
MISC_DT = 32
MISC_WI = 40
MOE_ROWS = 256


def _cparams(sem, vmem=None):
    return pltpu.CompilerParams(dimension_semantics=sem, vmem_limit_bytes=vmem)


def _dot(a, b):
    return jnp.dot(a, b, preferred_element_type=F32)


def _dot_nt(a, b):
    return lax.dot_general(a, b, (((1,), (1,)), ((), ())), preferred_element_type=F32)


def _dot_hi(a, b):
    return jnp.dot(a, b, preferred_element_type=F32, precision=lax.Precision.HIGHEST)


def _mm_kernel(x_ref, w_ref, o_ref):
    o_ref[...] = _dot(x_ref[...], w_ref[...])


def matmul_bf16(x, w, tm=512, tn=512):
    m, k = x.shape
    n = w.shape[1]
    return pl.pallas_call(
        _mm_kernel,
        grid=(n // tn, m // tm),
        in_specs=[pl.BlockSpec((tm, k), lambda j, i: (i, 0)),
                  pl.BlockSpec((k, tn), lambda j, i: (0, j))],
        out_specs=pl.BlockSpec((tm, tn), lambda j, i: (i, j)),
        out_shape=jax.ShapeDtypeStruct((m, n), F32),
        compiler_params=_cparams(("arbitrary", "arbitrary"), VMEM_LIMIT),
        name="in_proj",
    )(x, w)


def _rot_apply(x, c, s1, s2, shift):
    return x * c + pltpu.roll(x, LANES - shift, 1) * s1 + pltpu.roll(x, shift, 1) * s2


def _rope_cast_kernel(flag_ref, x_ref, c_ref, s1_ref, s2_ref, o_ref, ot_ref):
    j = pl.program_id(0)
    width = x_ref.shape[1]

    def emit(g, y):
        sl = slice(g * LANES, (g + 1) * LANES)
        o_ref[:, sl] = y.astype(BF16)
        ot_ref[sl, :] = y.T.astype(BF16)

    @pl.when(flag_ref[j] == 1)
    def _():
        c, s1, s2 = c_ref[...], s1_ref[...], s2_ref[...]
        for g in range(width // LANES):
            emit(g, _rot_apply(x_ref[:, g * LANES:(g + 1) * LANES], c, s1, s2, ROT_DIM // 2))

    @pl.when(flag_ref[j] == 0)
    def _():
        for g in range(width // LANES):
            emit(g, x_ref[:, g * LANES:(g + 1) * LANES])


def rope_cast(proj, flags, tabs, tw=512):
    t = proj.shape[0]
    tm = ATT_T
    grid_spec = pltpu.PrefetchScalarGridSpec(
        num_scalar_prefetch=1,
        grid=(N_CAST // tw, t // tm),
        in_specs=[pl.BlockSpec((tm, tw), lambda j, i, f: (i, j)),
                  pl.BlockSpec((tm, LANES), lambda j, i, f: (i, 0)),
                  pl.BlockSpec((tm, LANES), lambda j, i, f: (i, 0)),
                  pl.BlockSpec((tm, LANES), lambda j, i, f: (i, 0))],
        out_specs=[pl.BlockSpec((tm, tw), lambda j, i, f: (i, j)),
                   pl.BlockSpec((None, tw, tm), lambda j, i, f: (i, j, 0))],
    )
    return pl.pallas_call(
        _rope_cast_kernel,
        grid_spec=grid_spec,
        out_shape=[jax.ShapeDtypeStruct((t, N_CAST), BF16),
                   jax.ShapeDtypeStruct((t // tm, N_CAST, tm), BF16)],
        compiler_params=_cparams(("arbitrary", "arbitrary")),
        name="rope_cast",
    )(flags, proj, *tabs)


ATT_T = 256
ATT_HEADS = 4


LOG2E = 1.4426950408889634
ACC_ROWS = HEAD_DIM + 16


def _softmax_step(s, smax, vt, state, keep=None):
    m, acc = state
    if keep is not None:
        smax = jnp.where(keep, smax, NEG)
    m_new = jnp.maximum(m, smax)
    shift = m_new if keep is None else jnp.where(keep, m_new, -NEG)
    p = jnp.exp2(s - shift).astype(BF16)
    vt_ext = jnp.concatenate([vt, jnp.ones((ACC_ROWS - HEAD_DIM, vt.shape[1]), vt.dtype)], axis=0)
    acc_new = jnp.exp2(m - m_new) * acc + _dot(vt_ext, p)
    return m_new, acc_new


def _init_state(tq):
    return (jnp.full((1, tq), NEG, F32), jnp.zeros((ACC_ROWS, tq), F32))


def _causal_t(tk, tq):
    return lax.broadcasted_iota(I32, (tk, tq), 0) <= lax.broadcasted_iota(I32, (tk, tq), 1)


def _head_rows(qt):
    row = lax.broadcasted_iota(I32, (LANES, 1), 0)
    out = []
    for p in range(qt.shape[0] // LANES):
        blk = qt[p * LANES:(p + 1) * LANES, :]
        zero = jnp.zeros_like(blk)
        out += [jnp.where(row < HEAD_DIM, blk, zero), jnp.where(row >= HEAD_DIM, blk, zero)]
    return out


def _pair_cols(h):
    return slice((h // 2) * LANES, (h // 2 + 1) * LANES)


def _attention_tile(i, score_fn, vt_fn, s_ref, tq, last_mask, keep_fn=None):
    nh = s_ref.shape[1]

    def park(scores, buf):
        for h in range(nh):
            s_ref[buf, h] = scores[h]
        return [jnp.max(sc, axis=0, keepdims=True) for sc in scores]

    def step(n, carry, buf, nxt_buf):
        nxt = score_fn(n + 1)
        states = []
        for h in range(nh):
            smax, m, acc = carry[3 * h:3 * h + 3]
            keep = None if keep_fn is None else keep_fn(n, h)
            states.append(_softmax_step(s_ref[buf, h], smax, vt_fn(n, h), (m, acc), keep))
        smax_next = park(nxt, nxt_buf)
        out = []
        for h in range(nh):
            out.extend((smax_next[h],) + tuple(states[h]))
        return tuple(out)

    def pair(mm, carry):
        return step(2 * mm + 1, step(2 * mm, carry, 0, 1), 1, 0)

    init = []
    for smax in park(score_fn(0), 0):
        init.extend((smax,) + _init_state(tq))
    carry = lax.fori_loop(0, i // 2, pair, tuple(init))
    carry = lax.cond(i % 2 == 1, lambda c: step(i - 1, c, 0, 0), lambda c: c, carry)
    fin = []
    for h in range(nh):
        smax, m, acc = carry[3 * h:3 * h + 3]
        s = s_ref[0, h]
        if last_mask is not None:
            s = jnp.where(last_mask, s, NEG)
            smax = jnp.max(s, axis=0, keepdims=True)
        _, acc = _softmax_step(s, smax, vt_fn(i, h), (m, acc))
        fin.append(acc[0:HEAD_DIM] / acc[HEAD_DIM:HEAD_DIM + 1])
    return jnp.concatenate(fin, axis=0).T


def _top_n_rows(g, valid, n):
    row = lax.broadcasted_iota(I32, g.shape, 0)
    g = jnp.where(valid, g, -jnp.inf)
    sel = jnp.zeros(g.shape, jnp.bool_)
    for _ in range(n):
        m = jnp.max(g, axis=0, keepdims=True)
        idx = jnp.min(jnp.where(g == m, row, g.shape[0]), axis=0, keepdims=True)
        pick = row == idx
        sel = jnp.logical_or(sel, pick)
        g = jnp.where(pick, -jnp.inf, g)
    return jnp.logical_and(sel, valid)


def _moba_kernel(qt_ref, k_ref, vt_ref, o_ref, kmean_ref, sel_ref, s_ref):
    i = pl.program_id(1)
    tq = qt_ref.shape[1]
    nb = k_ref.shape[0] // MOBA_BLOCK

    @pl.when(i == 0)
    def _():
        kmean_ref[...] = jnp.zeros(kmean_ref.shape, F32)

        def mean_body(n, c):
            kb = k_ref[pl.ds(pl.multiple_of(n * MOBA_BLOCK, MOBA_BLOCK), MOBA_BLOCK), :].astype(F32)
            kmean_ref[pl.ds(n, 1), :] = jnp.sum(kb, axis=0, keepdims=True) * (1.0 / MOBA_BLOCK)
            return c

        lax.fori_loop(0, nb, mean_body, 0)

    nh = s_ref.shape[1]
    qh = _head_rows(qt_ref[...])
    kmean = kmean_ref[...].astype(BF16)
    row_n = lax.broadcasted_iota(I32, (kmean.shape[0], tq), 0)
    for h in range(nh):
        gate = _dot(kmean[:, _pair_cols(h)], qh[h])
        sel_ref[h] = jnp.where(_top_n_rows(gate, row_n < i, MOBA_TOPK), 1.0, 0.0)

    def score_fn(n):
        kb = k_ref[pl.ds(pl.multiple_of(n * MOBA_BLOCK, MOBA_BLOCK), MOBA_BLOCK), :]
        return [_dot(kb[:, _pair_cols(h)], qh[h]) for h in range(nh)]

    def vt_fn(n, h):
        return vt_ref[n, h * HEAD_DIM:(h + 1) * HEAD_DIM, :]

    def keep_fn(n, h):
        return sel_ref[h, pl.ds(n, 1), :] > 0.5

    o_ref[...] = _attention_tile(i, score_fn, vt_fn, s_ref, tq, _causal_t(MOBA_BLOCK, tq), keep_fn).astype(BF16)


def moba_attention(cast, cast_t):
    t = cast.shape[0]
    tq = MOBA_BLOCK
    w = ATT_HEADS * HEAD_DIM
    nbp = max(8, t // MOBA_BLOCK)
    return pl.pallas_call(
        _moba_kernel,
        grid=(N_HEADS // ATT_HEADS, t // tq),
        in_specs=[pl.BlockSpec((None, w, tq), lambda g, i: (i, COL_AQ // w + g, 0)),
                  pl.BlockSpec((t, w), lambda g, i: (0, COL_AK // w + g)),
                  pl.BlockSpec((t // tq, w, tq), lambda g, i: (0, COL_AV // w + g, 0))],
        out_specs=pl.BlockSpec((tq, w), lambda g, i: (i, g)),
        out_shape=jax.ShapeDtypeStruct((t, ATT_W), BF16),
        scratch_shapes=[pltpu.VMEM((nbp, w), F32), pltpu.VMEM((ATT_HEADS, nbp, tq), F32),
                        pltpu.VMEM((2, ATT_HEADS, MOBA_BLOCK, tq), F32)],
        compiler_params=_cparams(("arbitrary", "arbitrary"), VMEM_LIMIT),
        name="moba_attention",
    )(cast_t, cast, cast_t)


def _float_key(x):
    b = lax.bitcast_convert_type(x, I32)
    return jnp.where(b < 0, b ^ jnp.int32(0x7FFFFFFF), b)


def _dsa_select_kernel(qit_ref, ki_ref, misct_ref, bias_ref, key_ref):
    i = pl.program_id(0)
    tq = qit_ref.shape[1]
    tk = ATT_T
    n_chunks = ki_ref.shape[0] // tk
    n_live = i + 1

    qit = qit_ref[...]
    row = lax.broadcasted_iota(I32, (IDX_HEADS * IDX_DIM, 1), 0)
    zero = jnp.zeros_like(qit)
    qh = [jnp.where((row >= h * IDX_DIM) & (row < (h + 1) * IDX_DIM), qit, zero) for h in range(IDX_HEADS)]
    wrow = [misct_ref[MISC_WI + h:MISC_WI + h + 1, :] for h in range(IDX_HEADS)]
    visible = _causal_t(tk, tq)

    def score_body(j, c):
        kic = ki_ref[pl.ds(pl.multiple_of(j * tk, tk), tk), :]
        acc = jnp.zeros((tk, tq), F32)
        for h in range(IDX_HEADS):
            acc = acc + wrow[h] * jnp.maximum(_dot(kic, qh[h]), 0.0)
        acc = jnp.where(jnp.logical_or(j < i, visible), acc, -jnp.inf)
        key_ref[j] = _float_key(acc)
        return c

    lax.fori_loop(0, n_live, score_body, 0)

    def count_ge(cand):
        def body(j, cnt):
            for r in range(tk // 8):
                cnt = cnt + jnp.where(key_ref[j, r * 8:(r + 1) * 8, :] >= cand, 1.0, 0.0)
            return cnt
        cnt = lax.fori_loop(0, n_live, body, jnp.zeros((8, tq), F32))
        return jnp.sum(cnt, axis=0, keepdims=True)

    topk = float(DSA_TOPK)
    int_min = jnp.int32(-2 ** 31)
    nonneg = count_ge(jnp.zeros((1, tq), I32)) >= topk
    prefix0 = jnp.where(nonneg, jnp.int32(0), int_min)

    def bit_body(b, prefix):
        cand = prefix | (jnp.int32(1) << (30 - b))
        return jnp.where(count_ge(cand) >= topk, cand, prefix)

    thr = lax.fori_loop(0, 31, bit_body, prefix0)

    need = topk - count_ge(thr + 1)
    lower = jnp.where(lax.broadcasted_iota(I32, (tk, tk), 0) >= lax.broadcasted_iota(I32, (tk, tk), 1),
                      1.0, 0.0).astype(BF16)

    def bias_body(j, seen):
        kk = key_ref[j]
        eq = kk == thr
        rank = _dot(lower, jnp.where(eq, 1.0, 0.0).astype(BF16)) + seen
        keep = ((kk > thr) | (eq & (rank <= need))) & jnp.logical_or(j < i, visible)
        bias_ref[pl.ds(pl.multiple_of(j * tk, tk), tk), :] = jnp.where(keep, 0.0, NEG).astype(BF16)
        return rank[tk - 1:tk, :]

    lax.fori_loop(0, n_live, bias_body, jnp.zeros((1, tq), F32))

    def fill_body(j, c):
        bias_ref[pl.ds(pl.multiple_of(j * tk, tk), tk), :] = jnp.full((tk, tq), NEG, BF16)
        return c

    lax.fori_loop(n_live, n_chunks, fill_body, 0)


def dsa_select(cast, cast_t, misc_t):
    t = cast.shape[0]
    tq = ATT_T
    w = IDX_HEADS * IDX_DIM
    return pl.pallas_call(
        _dsa_select_kernel,
        grid=(t // tq,),
        in_specs=[pl.BlockSpec((None, w, tq), lambda i: (i, COL_CQI // w, 0)),
                  pl.BlockSpec((t, w), lambda i: (0, COL_KI4 // w)),
                  pl.BlockSpec((None, LANES, tq), lambda i: (i, 0, 0))],
        out_specs=pl.BlockSpec((None, t, tq), lambda i: (i, 0, 0)),
        out_shape=jax.ShapeDtypeStruct((t // tq, t, tq), BF16),
        scratch_shapes=[pltpu.VMEM((t // ATT_T, ATT_T, tq), I32)],
        compiler_params=_cparams(("arbitrary",), VMEM_LIMIT),
        name="dsa_select",
    )(cast_t, cast, misc_t)


def _dsa_attn_kernel(qt_ref, k_ref, vt_ref, bias_ref, o_ref, s_ref):
    i = pl.program_id(1)
    tq = qt_ref.shape[1]
    tk = ATT_T
    nh = s_ref.shape[1]
    qh = _head_rows(qt_ref[...])

    def score_fn(n):
        off = pl.multiple_of(n * tk, tk)
        kb = k_ref[pl.ds(off, tk), :]
        bias = bias_ref[pl.ds(off, tk), :].astype(F32)
        return [_dot(kb[:, _pair_cols(h)], qh[h]) + bias for h in range(nh)]

    def vt_fn(n, h):
        return vt_ref[n, h * HEAD_DIM:(h + 1) * HEAD_DIM, :]

    o_ref[...] = _attention_tile(i, score_fn, vt_fn, s_ref, tq, None).astype(BF16)


def dsa_attention(cast, cast_t, bias):
    t = cast.shape[0]
    tq = ATT_T
    w = ATT_HEADS * HEAD_DIM
    return pl.pallas_call(
        _dsa_attn_kernel,
        grid=(N_HEADS // ATT_HEADS, t // tq),
        in_specs=[pl.BlockSpec((None, w, tq), lambda g, i: (i, COL_CQ // w + g, 0)),
                  pl.BlockSpec((t, w), lambda g, i: (0, COL_CK // w + g)),
                  pl.BlockSpec((t // tq, w, tq), lambda g, i: (0, COL_CV // w + g, 0)),
                  pl.BlockSpec((None, t, tq), lambda g, i: (i, 0, 0))],
        out_specs=pl.BlockSpec((tq, w), lambda g, i: (i, g)),
        out_shape=jax.ShapeDtypeStruct((t, ATT_W), BF16),
        scratch_shapes=[pltpu.VMEM((2, ATT_HEADS, ATT_T, tq), F32)],
        compiler_params=_cparams(("arbitrary", "arbitrary"), VMEM_LIMIT),
        name="dsa_attention",
    )(cast_t, cast, cast_t, bias)


def _rms(x, w):
    return x * lax.rsqrt(jnp.mean(x * x, axis=-1, keepdims=True) + LN_EPS) * w


def _mla_prep_kernel(cq_ref, ckv_ref, misc_ref, qn_ref, kvn_ref, wq_ref, wk_ref, wv_ref,
                     c_ref, s1_ref, s2_ref, qt_ref, k_ref, vt_ref, misct_ref):
    c, s1, s2 = c_ref[...], s1_ref[...], s2_ref[...]
    cq = _rms(cq_ref[...], qn_ref[...]).astype(BF16)
    ckv = _rms(ckv_ref[...], kvn_ref[...]).astype(BF16)
    misc = misc_ref[...]
    misct_ref[...] = misc.T
    lane = lax.broadcasted_iota(I32, (1, LANES), 1)
    kpe = pltpu.roll(jnp.where(lane < MLA_ROPE, misc, 0.0), MLA_NOPE, 1)
    kpe = _rot_apply(kpe, c, s1, s2, MLA_ROPE // 2)
    q = _dot(cq, wq_ref[...])
    k = _dot(ckv, wk_ref[...])
    for h in range(N_HEADS):
        sl = slice(h * LANES, (h + 1) * LANES)
        qt_ref[sl, :] = _rot_apply(q[:, sl], c, s1, s2, MLA_ROPE // 2).T.astype(BF16)
        k_ref[:, sl] = (k[:, sl] + kpe).astype(BF16)
    v = _dot(ckv, wv_ref[...])
    for g in range(ATT_W // LANES):
        sl = slice(g * LANES, (g + 1) * LANES)
        vt_ref[sl, :] = v[:, sl].T.astype(BF16)


def mla_prep(proj, qn, kvn, wq, wk, wv, tabs):
    t = proj.shape[0]
    tm = ATT_T
    full = lambda a: pl.BlockSpec(a.shape, lambda i: (0,) * a.ndim)
    row = lambda w, col: pl.BlockSpec((tm, w), lambda i: (i, col // w))
    tile_t = lambda n: pl.BlockSpec((None, n, tm), lambda i: (i, 0, 0))
    return pl.pallas_call(
        _mla_prep_kernel,
        grid=(t // tm,),
        in_specs=[row(MLA_Q_RANK, COL_CQL), row(MLA_KV_RANK, COL_CKV), row(LANES, COL_MISC),
                  full(qn), full(kvn), full(wq), full(wk), full(wv),
                  row(LANES, 0), row(LANES, 0), row(LANES, 0)],
        out_specs=[tile_t(N_HEADS * LANES),
                   pl.BlockSpec((tm, N_HEADS * LANES), lambda i: (i, 0)),
                   tile_t(ATT_W), tile_t(LANES)],
        out_shape=[jax.ShapeDtypeStruct((t // tm, N_HEADS * LANES, tm), BF16),
                   jax.ShapeDtypeStruct((t, N_HEADS * LANES), BF16),
                   jax.ShapeDtypeStruct((t // tm, ATT_W, tm), BF16),
                   jax.ShapeDtypeStruct((t // tm, LANES, tm), F32)],
        compiler_params=_cparams(("arbitrary",), VMEM_LIMIT),
        name="mla_prep",
    )(proj, proj, proj, qn, kvn, wq, wk, wv, *tabs)


def _mla_attn_kernel(qt_ref, k_ref, vt_ref, o_ref, s_ref):
    i = pl.program_id(1)
    tq = qt_ref.shape[1]
    tk = ATT_T
    nh = s_ref.shape[1]
    qh = [qt_ref[h * LANES:(h + 1) * LANES, :] for h in range(nh)]

    def score_fn(n):
        off = pl.multiple_of(n * tk, tk)
        return [_dot(k_ref[pl.ds(off, tk), h * LANES:(h + 1) * LANES], qh[h]) for h in range(nh)]

    def vt_fn(n, h):
        return vt_ref[n, h * HEAD_DIM:(h + 1) * HEAD_DIM, :]

    o_ref[...] = _attention_tile(i, score_fn, vt_fn, s_ref, tq, _causal_t(tk, tq)).astype(BF16)


def mla_attention(qt, k, vt):
    t = k.shape[0]
    tq = ATT_T
    wqk = ATT_HEADS * LANES
    wv = ATT_HEADS * HEAD_DIM
    return pl.pallas_call(
        _mla_attn_kernel,
        grid=(N_HEADS // ATT_HEADS, t // tq),
        in_specs=[pl.BlockSpec((None, wqk, tq), lambda g, i: (i, g, 0)),
                  pl.BlockSpec((t, wqk), lambda g, i: (0, g)),
                  pl.BlockSpec((t // tq, wv, tq), lambda g, i: (0, g, 0))],
        out_specs=pl.BlockSpec((tq, wv), lambda g, i: (i, g)),
        out_shape=jax.ShapeDtypeStruct((t, ATT_W), BF16),
        scratch_shapes=[pltpu.VMEM((2, ATT_HEADS, ATT_T, tq), F32)],
        compiler_params=_cparams(("arbitrary", "arbitrary"), VMEM_LIMIT),
        name="mla_attention",
    )(qt, k, vt)


def _silu(x):
    return x / (1.0 + jnp.exp(-x))


def _softplus(x):
    return jnp.maximum(x, 0.0) + jnp.log1p(jnp.exp(-jnp.abs(x)))


def _ssd_kernel(z_ref, xbc_ref, misc_ref, cw_ref, cb_ref, dtb_ref, a_ref, dsk_ref, nw_ref, exp_ref,
                o_ref, ext_ref, state_ref):
    c = pl.program_id(0)
    L = SSM_CHUNK
    half = SSM_INNER // SSM_GROUPS

    @pl.when(c == 0)
    def _():
        ext_ref[0:8, :] = jnp.zeros((8, SSM_CONV_CH), F32)
        state_ref[...] = jnp.zeros(state_ref.shape, F32)

    ext_ref[8:8 + L, :] = xbc_ref[...]
    conv = jnp.zeros((L, SSM_CONV_CH), F32) + cb_ref[...]
    for kk in range(SSM_CONV):
        conv = conv + cw_ref[kk:kk + 1, :] * ext_ref[pl.ds(8 - (SSM_CONV - 1) + kk, L), :]
    tail = ext_ref[L:L + 8, :]
    ext_ref[0:8, :] = tail
    xbc = _silu(conv)
    xs = xbc[:, 0:SSM_INNER]
    bm = xbc[:, SSM_INNER:SSM_INNER + SSM_GROUPS * SSM_STATE]
    cm = xbc[:, SSM_INNER + SSM_GROUPS * SSM_STATE:]

    dt = _softplus(misc_ref[...] + dtb_ref[...])
    da = dt * a_ref[...]
    r = lax.broadcasted_iota(I32, (L, L), 0)
    cc = lax.broadcasted_iota(I32, (L, L), 1)
    tri = r >= cc
    a_cum = _dot_hi(jnp.where(tri, 1.0, 0.0), da)
    a_cum_t = a_cum.T
    expand = exp_ref[...]
    dt_full = _dot_hi(dt, expand)
    acum_full = _dot_hi(a_cum, expand)
    a_last = acum_full[L - 1:L, :]
    xdt = xs * dt_full
    xdt_b = xdt.astype(BF16)

    ys = []
    for g in range(SSM_GROUPS):
        bg = bm[:, g * SSM_STATE:(g + 1) * SSM_STATE].astype(BF16)
        cg = cm[:, g * SSM_STATE:(g + 1) * SSM_STATE].astype(BF16)
        cb = _dot_nt(cg, bg)
        hpg = SSM_HEADS // SSM_GROUPS
        cols = []
        for hh in range(hpg):
            h = g * hpg + hh
            diff = a_cum[:, MISC_DT + h:MISC_DT + h + 1] - a_cum_t[MISC_DT + h:MISC_DT + h + 1, :]
            seg = jnp.where(tri, jnp.exp(jnp.where(tri, diff, 0.0)), 0.0)
            sc = (cb * seg).astype(BF16)
            pair = (h // 2) * LANES
            yp = _dot(sc, xdt_b[:, pair:pair + LANES])
            cols.append(yp[:, (h % 2) * SSM_HEAD_DIM:(h % 2 + 1) * SSM_HEAD_DIM])
        y_diag = jnp.concatenate(cols, axis=1)
        gs = slice(g * half, (g + 1) * half)
        st = state_ref[:, gs]
        y_off = _dot(cg, st.astype(BF16)) * jnp.exp(acum_full[:, gs])
        ys.append(y_diag + y_off)
        xw = (xdt[:, gs] * jnp.exp(a_last[:, gs] - acum_full[:, gs])).astype(BF16)
        new = _dot(bm[:, g * SSM_STATE:(g + 1) * SSM_STATE].T.astype(BF16), xw)
        state_ref[:, gs] = st * jnp.exp(a_last[:, gs]) + new
    y = jnp.concatenate(ys, axis=1) + xs * dsk_ref[...]
    y = y * _silu(z_ref[...])
    o_ref[...] = _rms(y, nw_ref[...]).astype(BF16)


def ssd_mixer(proj, conv_w, conv_b, dtb_row, a_row, dsk_row, norm_w, expand):
    t = proj.shape[0]
    L = SSM_CHUNK
    full = lambda a: pl.BlockSpec(a.shape, lambda c: (0,) * a.ndim)
    return pl.pallas_call(
        _ssd_kernel,
        grid=(t // L,),
        in_specs=[pl.BlockSpec((L, SSM_INNER), lambda c: (c, COL_MZ // SSM_INNER)),
                  pl.BlockSpec((L, SSM_CONV_CH), lambda c: (c, COL_XBC // SSM_CONV_CH)),
                  pl.BlockSpec((L, LANES), lambda c: (c, COL_MISC // LANES)),
                  full(conv_w), full(conv_b), full(dtb_row), full(a_row), full(dsk_row), full(norm_w),
                  full(expand)],
        out_specs=pl.BlockSpec((L, SSM_INNER), lambda c: (c, 0)),
        out_shape=jax.ShapeDtypeStruct((t, SSM_INNER), BF16),
        scratch_shapes=[pltpu.VMEM((L + 8, SSM_CONV_CH), F32),
                        pltpu.VMEM((SSM_STATE, SSM_INNER), F32)],
        compiler_params=_cparams(("arbitrary",), VMEM_LIMIT),
        name="ssd_mixer",
    )(proj, proj, proj, conv_w, conv_b, dtb_row, a_row, dsk_row, norm_w, expand)


def _merge_kernel(x_ref, wg_ref, bg_ref, oa_ref, ob_ref, oc_ref, od_ref, wb_ref, o_ref):
    x = x_ref[...]
    acc = None
    for g, br in enumerate((oa_ref, ob_ref, oc_ref, od_ref)):
        logit = _dot(x, wg_ref[g]) + bg_ref[g:g + 1, :]
        gate = 1.0 / (1.0 + jnp.exp(-logit))
        term = gate * _dot(br[...], wb_ref[g])
        acc = term if acc is None else acc + term
    o_ref[...] = acc.astype(BF16)


def gated_merge(x_bf, wg, bg, branches, wb, tm=1024, tn=256):
    t = x_bf.shape[0]
    br_spec = pl.BlockSpec((tm, BRANCH_W), lambda j, i: (i, 0))
    return pl.pallas_call(
        _merge_kernel,
        grid=(D_MODEL // tn, t // tm),
        in_specs=[pl.BlockSpec((tm, D_MODEL), lambda j, i: (i, 0)),
                  pl.BlockSpec((N_BRANCH, D_MODEL, tn), lambda j, i: (0, 0, j)),
                  pl.BlockSpec((N_BRANCH, tn), lambda j, i: (0, j)),
                  br_spec, br_spec, br_spec, br_spec,
                  pl.BlockSpec((N_BRANCH, BRANCH_W, tn), lambda j, i: (0, 0, j))],
        out_specs=pl.BlockSpec((tm, tn), lambda j, i: (i, j)),
        out_shape=jax.ShapeDtypeStruct((t, D_MODEL), BF16),
        compiler_params=_cparams(("arbitrary", "arbitrary"), VMEM_LIMIT),
        name="gated_merge",
    )(x_bf, wg, bg, *branches, wb)


def _layer_norm(v, g, b):
    mu = jnp.mean(v, axis=-1, keepdims=True)
    d = v - mu
    var = jnp.mean(d * d, axis=-1, keepdims=True)
    return d * lax.rsqrt(var + LN_EPS) * g + b


def _outproj_kernel(m_ref, w_ref, x_ref, g_ref, b_ref, rw_ref, rb_ref, x1_ref, x1b_ref, idx_ref, gate_ref):
    mixed = _dot(m_ref[...], w_ref[...])
    x1 = _layer_norm(DN_ALPHA * x_ref[...] + mixed, g_ref[...], b_ref[...])
    x1_ref[...] = x1
    x1b = x1.astype(BF16)
    x1b_ref[...] = x1b
    logits = _dot(x1b, rw_ref[...]) + rb_ref[...]
    lane = lax.broadcasted_iota(I32, logits.shape, 1)
    g = logits
    idx_out = jnp.zeros(logits.shape, I32)
    val_out = jnp.zeros(logits.shape, F32)
    vals = []
    for kk in range(TOP_K):
        m = jnp.max(g, axis=1, keepdims=True)
        idx = jnp.min(jnp.where(g == m, lane, LANES), axis=1, keepdims=True)
        g = jnp.where(lane == idx, -jnp.inf, g)
        idx_out = jnp.where(lane == kk, idx, idx_out)
        vals.append(m)
    es = [jnp.exp(v - vals[0]) for v in vals]
    den = es[0] + es[1] + es[2] + es[3]
    for kk in range(TOP_K):
        val_out = jnp.where(lane == kk, es[kk] / den, val_out)
    idx_ref[...] = idx_out
    gate_ref[...] = val_out


def outproj_ln_router(merged, w_out, x, g, b, rw, rb, tm=256):
    t = x.shape[0]
    full = lambda a: pl.BlockSpec(a.shape, lambda i: (0,) * a.ndim)
    row = lambda w: pl.BlockSpec((tm, w), lambda i: (i, 0))
    return pl.pallas_call(
        _outproj_kernel,
        grid=(t // tm,),
        in_specs=[row(D_MODEL), full(w_out), row(D_MODEL), full(g), full(b), full(rw), full(rb)],
        out_specs=[row(D_MODEL), row(D_MODEL), row(LANES), row(LANES)],
        out_shape=[jax.ShapeDtypeStruct((t, D_MODEL), F32), jax.ShapeDtypeStruct((t, D_MODEL), BF16),
                   jax.ShapeDtypeStruct((t, LANES), I32), jax.ShapeDtypeStruct((t, LANES), F32)],
        compiler_params=_cparams(("arbitrary",), VMEM_LIMIT),
        name="outproj_ln_router",
    )(merged, w_out, x, g, b, rw, rb)


def _start_row_gather(idx_ref, src_hbm, dst, sem, rows):
    for r in range(rows):
        pltpu.make_async_copy(src_hbm.at[pl.ds(idx_ref[0, r], 1)], dst.at[pl.ds(r, 1)], sem).start(priority=r % 2)


def _wait_row_gather(src_hbm, dst, sem, rows):
    pltpu.make_async_copy(src_hbm.at[pl.ds(0, rows)], dst, sem).wait()


def _expert_kernel(be_ref, nu_ref, tok_ref, tokn_ref, x_hbm, wgu_ref, bgu_ref, wd_ref, bd_ref, o_ref, xbuf, sem):
    blk = pl.program_id(0)
    n_used = nu_ref[0]
    slot = lax.rem(blk, 2)

    @pl.when(jnp.logical_and(blk == 0, n_used > 0))
    def _():
        _start_row_gather(tok_ref, x_hbm, xbuf.at[0], sem.at[0], MOE_ROWS)

    @pl.when(blk + 1 < n_used)
    def _():
        _start_row_gather(tokn_ref, x_hbm, xbuf.at[1 - slot], sem.at[1 - slot], MOE_ROWS)

    @pl.when(blk < n_used)
    def _():
        _wait_row_gather(x_hbm, xbuf.at[slot], sem.at[slot], MOE_ROWS)
        gu = _dot(xbuf[slot].astype(BF16), wgu_ref[0]) + bgu_ref[0]
        glu = jnp.minimum(gu[:, :D_FF], SWIGLU_LIMIT)
        lin = jnp.clip(gu[:, D_FF:], -SWIGLU_LIMIT, SWIGLU_LIMIT)
        act = glu / (1.0 + jnp.exp(-SWIGLU_ALPHA * glu)) * (lin + 1.0)
        o_ref[...] = _dot(act.astype(BF16), wd_ref[0]) + bd_ref[0]

    @pl.when(blk >= nu_ref[0])
    def _():
        o_ref[...] = jnp.zeros(o_ref.shape, F32)


def expert_ffn(block_e, n_used, row_tok, x, wgu, bgu, wd, bd):
    n_blocks = row_tok.shape[0]
    n_rows = n_blocks * MOE_ROWS
    grid_spec = pltpu.PrefetchScalarGridSpec(
        num_scalar_prefetch=2,
        grid=(n_blocks,),
        in_specs=[pl.BlockSpec((None, 1, MOE_ROWS), lambda b, be, nu: (b, 0, 0), memory_space=pltpu.SMEM),
                  pl.BlockSpec((None, 1, MOE_ROWS), lambda b, be, nu: (jnp.minimum(b + 1, n_blocks - 1), 0, 0),
                               memory_space=pltpu.SMEM),
                  pl.BlockSpec(memory_space=pl.ANY),
                  pl.BlockSpec((1, D_MODEL, 2 * D_FF), lambda b, be, nu: (be[b], 0, 0)),
                  pl.BlockSpec((1, 1, 2 * D_FF), lambda b, be, nu: (be[b], 0, 0)),
                  pl.BlockSpec((1, D_FF, D_MODEL), lambda b, be, nu: (be[b], 0, 0)),
                  pl.BlockSpec((1, 1, D_MODEL), lambda b, be, nu: (be[b], 0, 0))],
        out_specs=pl.BlockSpec((MOE_ROWS, D_MODEL), lambda b, be, nu: (b, 0)),
        scratch_shapes=[pltpu.VMEM((2, MOE_ROWS, D_MODEL), F32), pltpu.SemaphoreType.DMA((2,))],
    )
    return pl.pallas_call(
        _expert_kernel,
        grid_spec=grid_spec,
        out_shape=jax.ShapeDtypeStruct((n_rows, D_MODEL), F32),
        compiler_params=_cparams(("arbitrary",), VMEM_LIMIT),
        name="expert_ffn",
    )(block_e, n_used, row_tok, row_tok, x, wgu, bgu, wd, bd)


COMBINE_TM = 128


def _combine_kernel(pos_ref, posn_ref, out_hbm, gate_ref, x_ref, g_ref, b_ref, o_ref, ob_ref, rbuf, sem):
    i = pl.program_id(0)
    tm = x_ref.shape[0]
    rows = TOP_K * tm
    slot = lax.rem(i, 2)

    @pl.when(i == 0)
    def _():
        _start_row_gather(pos_ref, out_hbm, rbuf.at[0], sem.at[0], rows)

    @pl.when(i + 1 < pl.num_programs(0))
    def _():
        _start_row_gather(posn_ref, out_hbm, rbuf.at[1 - slot], sem.at[1 - slot], rows)

    _wait_row_gather(out_hbm, rbuf.at[slot], sem.at[slot], rows)
    gate = gate_ref[...]
    y = jnp.zeros(x_ref.shape, F32)
    for kk in range(TOP_K):
        y = y + gate[:, kk:kk + 1] * rbuf[slot, kk * tm:(kk + 1) * tm, :]
    x2 = _layer_norm(DN_ALPHA * x_ref[...] + y, g_ref[...], b_ref[...])
    o_ref[...] = x2
    ob_ref[...] = x2.astype(BF16)


def combine_ln(pos_km, out, gates, x1, g, b):
    t = x1.shape[0]
    tm = COMBINE_TM
    n_tiles = t // tm
    full = lambda a: pl.BlockSpec(a.shape, lambda i: (0,) * a.ndim)
    row = lambda w: pl.BlockSpec((tm, w), lambda i: (i, 0))
    return pl.pallas_call(
        _combine_kernel,
        grid=(n_tiles,),
        in_specs=[pl.BlockSpec((None, 1, TOP_K * tm), lambda i: (i, 0, 0), memory_space=pltpu.SMEM),
                  pl.BlockSpec((None, 1, TOP_K * tm), lambda i: (jnp.minimum(i + 1, n_tiles - 1), 0, 0),
                               memory_space=pltpu.SMEM),
                  pl.BlockSpec(memory_space=pl.ANY), row(LANES), row(D_MODEL), full(g), full(b)],
        out_specs=[row(D_MODEL), row(D_MODEL)],
        out_shape=[jax.ShapeDtypeStruct((t, D_MODEL), F32), jax.ShapeDtypeStruct((t, D_MODEL), BF16)],
        scratch_shapes=[pltpu.VMEM((2, TOP_K * tm, D_MODEL), F32), pltpu.SemaphoreType.DMA((2,))],
        compiler_params=_cparams(("arbitrary",), VMEM_LIMIT),
        name="combine_ln",
    )(pos_km, pos_km, out, gates, x1, g, b)


def _rope_tables(positions):
    pos = positions.reshape(-1).astype(F32)

    def cs(dim, theta):
        inv = theta ** (-jnp.arange(0, dim, 2, dtype=F32) / dim)
        ang = pos[:, None] * inv
        return jnp.cos(ang), jnp.sin(ang)

    t = pos.shape[0]
    cos_p, sin_p = cs(ROT_DIM, ROPE_THETA)
    half = ROT_DIM // 2
    one = jnp.ones((t, HEAD_DIM - ROT_DIM), F32)
    zero = jnp.zeros((t, HEAD_DIM - ROT_DIM), F32)
    zh = jnp.zeros((t, half), F32)
    c64 = jnp.concatenate([cos_p, cos_p, one], axis=1)
    s1_64 = jnp.concatenate([-sin_p, zh, zero], axis=1)
    s2_64 = jnp.concatenate([zh, sin_p, zero], axis=1)
    tabs_p = tuple(jnp.tile(a, (1, 2)) for a in (c64, s1_64, s2_64))

    cos_m, sin_m = cs(MLA_ROPE, MLA_THETA)
    hm = MLA_ROPE // 2
    pre1 = jnp.ones((t, MLA_NOPE), F32)
    pre0 = jnp.zeros((t, MLA_NOPE), F32)
    post1 = jnp.ones((t, LANES - MLA_NOPE - MLA_ROPE), F32)
    post0 = jnp.zeros((t, LANES - MLA_NOPE - MLA_ROPE), F32)
    zm = jnp.zeros((t, hm), F32)
    c_m = jnp.concatenate([pre1, cos_m, cos_m, post1], axis=1)
    s1_m = jnp.concatenate([pre0, -sin_m, zm, post0], axis=1)
    s2_m = jnp.concatenate([pre0, zm, sin_m, post0], axis=1)
    return tabs_p, (c_m, s1_m, s2_m)


def _prep_in_weights(w_in):
    pts = np.cumsum((0,) + SPLIT_SIZES)
    seg = lambda k: w_in[:, pts[k]:pts[k + 1]]
    (a_q, a_k, a_v, m_z, m_xbc, m_dt, c_q, c_k, c_v, c_qi, c_ki, c_wi, d_cq, d_ckv, d_kpe) = [seg(k) for k in range(15)]
    scale = HEAD_DIM ** -0.5 * LOG2E
    zeros = lambda n: jnp.zeros((D_MODEL, n), w_in.dtype)
    cols = [a_q * scale, a_k, a_v, c_q * scale, c_k, c_v, c_qi, c_ki, c_ki, c_ki, c_ki,
            m_z, m_xbc, d_ckv, d_cq, d_kpe, m_dt, c_wi * IDX_SCALE]
    small = jnp.concatenate(cols, axis=1)
    small = jnp.concatenate([small, zeros(N_SMALL - small.shape[1])], axis=1).astype(BF16)
    wg = w_in[:, pts[15]:].reshape(D_MODEL, N_BRANCH, D_MODEL).transpose(1, 0, 2).astype(BF16)
    return small, wg


def _prep_mla_weights(w_uq, w_ukv):
    dq = MLA_NOPE + MLA_ROPE
    wq = w_uq.reshape(MLA_Q_RANK, N_HEADS, dq) * (dq ** -0.5 * LOG2E)
    wq = jnp.pad(wq, ((0, 0), (0, 0), (0, LANES - dq))).reshape(MLA_Q_RANK, N_HEADS * LANES)
    wkv = w_ukv.reshape(MLA_KV_RANK, N_HEADS, MLA_NOPE + MLA_V)
    wk = jnp.pad(wkv[:, :, :MLA_NOPE], ((0, 0), (0, 0), (0, LANES - MLA_NOPE))).reshape(MLA_KV_RANK, N_HEADS * LANES)
    wv = wkv[:, :, MLA_NOPE:].reshape(MLA_KV_RANK, N_HEADS * MLA_V)
    return wq.astype(BF16), wk.astype(BF16), wv.astype(BF16)


def _misc_row(v):
    return jnp.zeros((1, LANES), F32).at[0, MISC_DT:MISC_DT + SSM_HEADS].set(v.astype(F32))


def _route(top_idx, t):
    tk = t * TOP_K
    flat_e = top_idx.reshape(-1)
    onehot = (flat_e[:, None] == jnp.arange(N_EXPERTS)[None, :]).astype(I32)
    seen = jnp.cumsum(onehot, axis=0)
    counts = seen[-1]
    padded = ((counts + MOE_ROWS - 1) // MOE_ROWS) * MOE_ROWS
    pad_end = jnp.cumsum(padded)
    pad_start = pad_end - padded
    rank = jnp.sum(onehot * seen, axis=1) - 1
    pos = (pad_start[flat_e] + rank).astype(I32)
    n_blocks = tk // MOE_ROWS + N_EXPERTS
    n_rows = n_blocks * MOE_ROWS
    row_tok = jnp.zeros((n_rows,), I32).at[pos].set(jnp.arange(tk, dtype=I32) // TOP_K)
    block_start = jnp.arange(n_blocks) * MOE_ROWS
    block_e = jnp.minimum(jnp.sum(pad_end[None, :] <= block_start[:, None], axis=1), N_EXPERTS - 1).astype(I32)
    n_used = (pad_end[-1] // MOE_ROWS).astype(I32).reshape(1)
    return row_tok, pos, block_e, n_used


def _layer(x, x_bf, tabs_p, tabs_m, flags, expand, p):
    t = x.shape[0]
    w_small, wg = _prep_in_weights(p["w_in"])
    proj = matmul_bf16(x_bf, w_small)
    cast, cast_t = rope_cast(proj, flags, tabs_p)

    o_a = moba_attention(cast, cast_t)
    o_b = ssd_mixer(proj, p["conv_w"], p["conv_b"].reshape(1, -1), _misc_row(p["dt_bias"]),
                    _misc_row(-jnp.exp(p["a_log"].astype(F32))),
                    jnp.repeat(p["d_skip"].astype(F32), SSM_HEAD_DIM).reshape(1, -1),
                    p["ssm_norm_w"].reshape(1, -1), expand)
    wq, wk, wv = _prep_mla_weights(p["w_uq"], p["w_ukv"])
    qt_m, k_m, vt_m, misc_t = mla_prep(proj, p["q_norm_w"].reshape(1, -1), p["kv_norm_w"].reshape(1, -1), wq, wk, wv,
                                       tabs_m)
    o_c = dsa_attention(cast, cast_t, dsa_select(cast, cast_t, misc_t))
    o_d = mla_attention(qt_m, k_m, vt_m)

    merged = gated_merge(x_bf, wg, p["b_gate"], (o_a, o_b, o_c, o_d), p["w_branch"].astype(BF16))
    rw = jnp.pad(p["router_w"], ((0, 0), (0, LANES - N_EXPERTS))).astype(BF16)
    rb = jnp.concatenate([p["router_b"].astype(F32), jnp.full((LANES - N_EXPERTS,), -jnp.inf, F32)]).reshape(1, -1)
    x1, x1_bf, top_idx, gates = outproj_ln_router(merged, p["w_out"].astype(BF16), x, p["ln1_g"].reshape(1, -1),
                                                  p["ln1_b"].reshape(1, -1), rw, rb)

    row_tok, pos, block_e, n_used = _route(top_idx[:, :TOP_K], t)
    out = expert_ffn(block_e, n_used, row_tok.reshape(-1, 1, MOE_ROWS), x1, p["w_gate_up"].astype(BF16),
                     p["b_gate_up"][:, None, :], p["w_down"].astype(BF16), p["b_down"][:, None, :])
    pos_km = pos.reshape(t // COMBINE_TM, COMBINE_TM, TOP_K).transpose(0, 2, 1).reshape(-1, 1, TOP_K * COMBINE_TM)
    return combine_ln(pos_km, out, gates, x1, p["ln2_g"].reshape(1, -1), p["ln2_b"].reshape(1, -1))


def kernel(x, positions, w_in, b_gate, conv_w, conv_b, dt_bias, a_log, d_skip, ssm_norm_w, q_norm_w, w_uq,
           kv_norm_w, w_ukv, w_branch, w_out, ln1_g, ln1_b, router_w, router_b, w_gate_up, b_gate_up, w_down,
           b_down, ln2_g, ln2_b):
    b, s, d = x.shape
    assert b == 1 and d == D_MODEL
    names = ("w_in", "b_gate", "conv_w", "conv_b", "dt_bias", "a_log", "d_skip", "ssm_norm_w", "q_norm_w", "w_uq",
             "kv_norm_w", "w_ukv", "w_branch", "w_out", "ln1_g", "ln1_b", "router_w", "router_b", "w_gate_up",
             "b_gate_up", "w_down", "b_down", "ln2_g", "ln2_b")
    stacked = (w_in, b_gate, conv_w, conv_b, dt_bias, a_log, d_skip, ssm_norm_w, q_norm_w, w_uq, kv_norm_w, w_ukv,
               w_branch, w_out, ln1_g, ln1_b, router_w, router_b, w_gate_up, b_gate_up, w_down, b_down, ln2_g, ln2_b)
    tabs_p, tabs_m = _rope_tables(positions)
    flags = jnp.array([1, 1, 0, 1, 1, 0, 1], I32)
    head_of_lane = jnp.arange(SSM_INNER) // SSM_HEAD_DIM
    expand = (jnp.arange(LANES)[:, None] == (MISC_DT + head_of_lane)[None, :]).astype(F32)

    xt = x.reshape(s, d)
    xt_bf = xt.astype(BF16)
    for l in range(DEPTH):
        p = {n: a[l] for n, a in zip(names, stacked)}
        xt, xt_bf = _layer(xt, xt_bf, tabs_p, tabs_m, flags, expand, p)
    return xt.reshape(b, s, d)
```

```python
import functools
import math

import jax
import jax.numpy as jnp
import numpy as np
from jax import lax
from jax.experimental import pallas as pl
from jax.experimental.pallas import tpu as pltpu

F32 = jnp.float32
BF16 = jnp.bfloat16
I32 = jnp.int32

D_MODEL = 2048
DEPTH = 4
HEAD_DIM = 64
ROT_DIM = HEAD_DIM // 4
ROPE_THETA = 500000.0
N_HEADS = 8
ATT_W = N_HEADS * HEAD_DIM
MOBA_BLOCK = 256
MOBA_TOPK = 3
SSM_HEADS = 8
SSM_HEAD_DIM = 64
SSM_INNER = 512
SSM_STATE = 128
SSM_GROUPS = 2
SSM_CONV = 4
SSM_CONV_CH = 1024
SSM_CHUNK = 256
IDX_HEADS = 4
IDX_DIM = 64
DSA_TOPK = 256
IDX_SCALE = (IDX_HEADS * IDX_DIM) ** -0.5
MLA_Q_RANK = 384
MLA_KV_RANK = 256
MLA_NOPE = 64
MLA_ROPE = 32
MLA_V = 64
MLA_THETA = 10000.0
N_BRANCH = 4
BRANCH_W = 512
SPLIT_SIZES = (512, 512, 512, 512, 1024, 8, 512, 512, 512, 256, 64, 4, 384, 256, 32, N_BRANCH * D_MODEL)
N_EXPERTS = 32
TOP_K = 4
D_FF = 768
SWIGLU_ALPHA = 1.702
SWIGLU_LIMIT = 7.0
DN_ALPHA = (2 * DEPTH) ** 0.25
LN_EPS = 1e-5

LANES = 128
NEG = -1e30
VMEM_LIMIT = 56 * 1024 * 1024

COL_AQ, COL_AK, COL_AV = 0, 512, 1024
COL_CQ, COL_CK, COL_CV = 1536, 2048, 2560
COL_CQI, COL_KI4 = 3072, 3328
COL_MZ = 3584
COL_XBC = 4096
COL_CKV = 5120
COL_CQL = 5376
COL_MISC = 5760
N_SMALL = 6144
N_CAST = 3584
MISC_DT = 32
MISC_WI = 40
MOE_ROWS = 256


def _cparams(sem, vmem=None):
    return pltpu.CompilerParams(dimension_semantics=sem, vmem_limit_bytes=vmem)


def _dot(a, b):
    return jnp.dot(a, b, preferred_element_type=F32)


def _dot_nt(a, b):
    return lax.dot_general(a, b, (((1,), (1,)), ((), ())), preferred_element_type=F32)


def _dot_hi(a, b):
    return jnp.dot(a, b, preferred_element_type=F32, precision=lax.Precision.HIGHEST)


def _mm_kernel(x_ref, w_ref, o_ref):
    o_ref[...] = _dot(x_ref[...], w_ref[...])


def matmul_bf16(x, w, tm=512, tn=1024):
    m, k = x.shape
    n = w.shape[1]
    return pl.pallas_call(
        _mm_kernel,
        grid=(n // tn, m // tm),
        in_specs=[pl.BlockSpec((tm, k), lambda j, i: (i, 0)),
                  pl.BlockSpec((k, tn), lambda j, i: (0, j))],
        out_specs=pl.BlockSpec((tm, tn), lambda j, i: (i, j)),
        out_shape=jax.ShapeDtypeStruct((m, n), F32),
        compiler_params=_cparams(("arbitrary", "arbitrary"), VMEM_LIMIT),
        name="in_proj",
    )(x, w)


def _rot_apply(x, c, s1, s2, shift):
    return x * c + pltpu.roll(x, LANES - shift, 1) * s1 + pltpu.roll(x, shift, 1) * s2


def _rope_cast_kernel(flag_ref, x_ref, c_ref, s1_ref, s2_ref, o_ref, ot_ref):
    j = pl.program_id(0)
    width = x_ref.shape[1]

    def emit(g, y):
        sl = slice(g * LANES, (g + 1) * LANES)
        o_ref[:, sl] = y.astype(BF16)
        ot_ref[sl, :] = y.T.astype(BF16)

    @pl.when(flag_ref[j] == 1)
    def _():
        c, s1, s2 = c_ref[...], s1_ref[...], s2_ref[...]
        for g in range(width // LANES):
            emit(g, _rot_apply(x_ref[:, g * LANES:(g + 1) * LANES], c, s1, s2, ROT_DIM // 2))

    @pl.when(flag_ref[j] == 0)
    def _():
        for g in range(width // LANES):
            emit(g, x_ref[:, g * LANES:(g + 1) * LANES])


def rope_cast(proj, flags, tabs, tw=512):
    t = proj.shape[0]
    tm = ATT_T
    grid_spec = pltpu.PrefetchScalarGridSpec(
        num_scalar_prefetch=1,
        grid=(N_CAST // tw, t // tm),
        in_specs=[pl.BlockSpec((tm, tw), lambda j, i, f: (i, j)),
                  pl.BlockSpec((tm, LANES), lambda j, i, f: (i, 0)),
                  pl.BlockSpec((tm, LANES), lambda j, i, f: (i, 0)),
                  pl.BlockSpec((tm, LANES), lambda j, i, f: (i, 0))],
        out_specs=[pl.BlockSpec((tm, tw), lambda j, i, f: (i, j)),
                   pl.BlockSpec((None, tw, tm), lambda j, i, f: (i, j, 0))],
    )
    return pl.pallas_call(
        _rope_cast_kernel,
        grid_spec=grid_spec,
        out_shape=[jax.ShapeDtypeStruct((t, N_CAST), BF16),
                   jax.ShapeDtypeStruct((t // tm, N_CAST, tm), BF16)],
        compiler_params=_cparams(("arbitrary", "arbitrary")),
        name="rope_cast",
    )(flags, proj, *tabs)


ATT_T = 256
ATT_HEADS = 4


LOG2E = 1.4426950408889634
ACC_ROWS = HEAD_DIM + 16


def _softmax_step(s, smax, vt, state, keep=None):
    m, acc = state
    if keep is not None:
        smax = jnp.where(keep, smax, NEG)
    m_new = jnp.maximum(m, smax)
    shift = m_new if keep is None else jnp.where(keep, m_new, -NEG)
    p = jnp.exp2(s - shift).astype(BF16)
    vt_ext = jnp.concatenate([vt, jnp.ones((ACC_ROWS - HEAD_DIM, vt.shape[1]), vt.dtype)], axis=0)
    acc_new = jnp.exp2(m - m_new) * acc + _dot(vt_ext, p)
    return m_new, acc_new


def _init_state(tq):
    return (jnp.full((1, tq), NEG, F32), jnp.zeros((ACC_ROWS, tq), F32))


def _causal_t(tk, tq):
    return lax.broadcasted_iota(I32, (tk, tq), 0) <= lax.broadcasted_iota(I32, (tk, tq), 1)


def _head_rows(qt):
    row = lax.broadcasted_iota(I32, (LANES, 1), 0)
    out = []
    for p in range(qt.shape[0] // LANES):
        blk = qt[p * LANES:(p + 1) * LANES, :]
        zero = jnp.zeros_like(blk)
        out += [jnp.where(row < HEAD_DIM, blk, zero), jnp.where(row >= HEAD_DIM, blk, zero)]
    return out


def _pair_cols(h):
    return slice((h // 2) * LANES, (h // 2 + 1) * LANES)


def _attention_tile(i, score_fn, vt_fn, s_ref, tq, last_mask, keep_fn=None):
    nh = s_ref.shape[1]

    def park(scores, buf):
        for h in range(nh):
            s_ref[buf, h] = scores[h]
        return [jnp.max(sc, axis=0, keepdims=True) for sc in scores]

    def step(n, carry, buf, nxt_buf):
        nxt = score_fn(n + 1)
        states = []
        for h in range(nh):
            smax, m, acc = carry[3 * h:3 * h + 3]
            keep = None if keep_fn is None else keep_fn(n, h)
            states.append(_softmax_step(s_ref[buf, h], smax, vt_fn(n, h), (m, acc), keep))
        smax_next = park(nxt, nxt_buf)
        out = []
        for h in range(nh):
            out.extend((smax_next[h],) + tuple(states[h]))
        return tuple(out)

    def pair(mm, carry):
        return step(2 * mm + 1, step(2 * mm, carry, 0, 1), 1, 0)

    init = []
    for smax in park(score_fn(0), 0):
        init.extend((smax,) + _init_state(tq))
    carry = lax.fori_loop(0, i // 2, pair, tuple(init))
    carry = lax.cond(i % 2 == 1, lambda c: step(i - 1, c, 0, 0), lambda c: c, carry)
    fin = []
    for h in range(nh):
        smax, m, acc = carry[3 * h:3 * h + 3]
        s = s_ref[0, h]
        if last_mask is not None:
            s = jnp.where(last_mask, s, NEG)
            smax = jnp.max(s, axis=0, keepdims=True)
        _, acc = _softmax_step(s, smax, vt_fn(i, h), (m, acc))
        fin.append(acc[0:HEAD_DIM] / acc[HEAD_DIM:HEAD_DIM + 1])
    return jnp.concatenate(fin, axis=0).T


def _top_n_rows(g, valid, n):
    row = lax.broadcasted_iota(I32, g.shape, 0)
    g = jnp.where(valid, g, -jnp.inf)
    sel = jnp.zeros(g.shape, jnp.bool_)
    for _ in range(n):
        m = jnp.max(g, axis=0, keepdims=True)
        idx = jnp.min(jnp.where(g == m, row, g.shape[0]), axis=0, keepdims=True)
        pick = row == idx
        sel = jnp.logical_or(sel, pick)
        g = jnp.where(pick, -jnp.inf, g)
    return jnp.logical_and(sel, valid)


def _moba_kernel(qt_ref, k_ref, vt_ref, o_ref, kmean_ref, sel_ref, s_ref):
    i = pl.program_id(1)
    tq = qt_ref.shape[1]
    nb = k_ref.shape[0] // MOBA_BLOCK

    @pl.when(i == 0)
    def _():
        kmean_ref[...] = jnp.zeros(kmean_ref.shape, F32)

        def mean_body(n, c):
            kb = k_ref[pl.ds(pl.multiple_of(n * MOBA_BLOCK, MOBA_BLOCK), MOBA_BLOCK), :].astype(F32)
            kmean_ref[pl.ds(n, 1), :] = jnp.sum(kb, axis=0, keepdims=True) * (1.0 / MOBA_BLOCK)
            return c

        lax.fori_loop(0, nb, mean_body, 0)

    nh = s_ref.shape[1]
    qh = _head_rows(qt_ref[...])
    kmean = kmean_ref[...].astype(BF16)
    row_n = lax.broadcasted_iota(I32, (kmean.shape[0], tq), 0)
    for h in range(nh):
        gate = _dot(kmean[:, _pair_cols(h)], qh[h])
        sel_ref[h] = jnp.where(_top_n_rows(gate, row_n < i, MOBA_TOPK), 1.0, 0.0)

    def score_fn(n):
        kb = k_ref[pl.ds(pl.multiple_of(n * MOBA_BLOCK, MOBA_BLOCK), MOBA_BLOCK), :]
        return [_dot(kb[:, _pair_cols(h)], qh[h]) for h in range(nh)]

    def vt_fn(n, h):
        return vt_ref[n, h * HEAD_DIM:(h + 1) * HEAD_DIM, :]

    def keep_fn(n, h):
        return sel_ref[h, pl.ds(n, 1), :] > 0.5

    o_ref[...] = _attention_tile(i, score_fn, vt_fn, s_ref, tq, _causal_t(MOBA_BLOCK, tq), keep_fn).astype(BF16)


def moba_attention(cast, cast_t):
    t = cast.shape[0]
    tq = MOBA_BLOCK
    w = ATT_HEADS * HEAD_DIM
    nbp = max(8, t // MOBA_BLOCK)
    return pl.pallas_call(
        _moba_kernel,
        grid=(N_HEADS // ATT_HEADS, t // tq),
        in_specs=[pl.BlockSpec((None, w, tq), lambda g, i: (i, COL_AQ // w + g, 0)),
                  pl.BlockSpec((t, w), lambda g, i: (0, COL_AK // w + g)),
                  pl.BlockSpec((t // tq, w, tq), lambda g, i: (0, COL_AV // w + g, 0))],
        out_specs=pl.BlockSpec((tq, w), lambda g, i: (i, g)),
        out_shape=jax.ShapeDtypeStruct((t, ATT_W), BF16),
        scratch_shapes=[pltpu.VMEM((nbp, w), F32), pltpu.VMEM((ATT_HEADS, nbp, tq), F32),
                        pltpu.VMEM((2, ATT_HEADS, MOBA_BLOCK, tq), F32)],
        compiler_params=_cparams(("arbitrary", "arbitrary"), VMEM_LIMIT),
        name="moba_attention",
    )(cast_t, cast, cast_t)


def _float_key(x):
    b = lax.bitcast_convert_type(x, I32)
    return jnp.where(b < 0, b ^ jnp.int32(0x7FFFFFFF), b)


def _dsa_select_kernel(qit_ref, ki_ref, misct_ref, bias_ref, key_ref):
    i = pl.program_id(0)
    tq = qit_ref.shape[1]
    tk = ATT_T
    n_chunks = ki_ref.shape[0] // tk
    n_live = i + 1

    qit = qit_ref[...]
    row = lax.broadcasted_iota(I32, (IDX_HEADS * IDX_DIM, 1), 0)
    zero = jnp.zeros_like(qit)
    qh = [jnp.where((row >= h * IDX_DIM) & (row < (h + 1) * IDX_DIM), qit, zero) for h in range(IDX_HEADS)]
    wrow = [misct_ref[MISC_WI + h:MISC_WI + h + 1, :] for h in range(IDX_HEADS)]
    visible = _causal_t(tk, tq)

    def score_body(j, c):
        kic = ki_ref[pl.ds(pl.multiple_of(j * tk, tk), tk), :]
        acc = jnp.zeros((tk, tq), F32)
        for h in range(IDX_HEADS):
            acc = acc + wrow[h] * jnp.maximum(_dot(kic, qh[h]), 0.0)
        acc = jnp.where(jnp.logical_or(j < i, visible), acc, -jnp.inf)
        key_ref[j] = _float_key(acc)
        return c

    lax.fori_loop(0, n_live, score_body, 0)

    def count_ge(cand):
        def body(j, cnt):
            for r in range(tk // 8):
                cnt = cnt + jnp.where(key_ref[j, r * 8:(r + 1) * 8, :] >= cand, 1.0, 0.0)
            return cnt
        cnt = lax.fori_loop(0, n_live, body, jnp.zeros((8, tq), F32))
        return jnp.sum(cnt, axis=0, keepdims=True)

    topk = float(DSA_TOPK)
    int_min = jnp.int32(-2 ** 31)
    nonneg = count_ge(jnp.zeros((1, tq), I32)) >= topk
    prefix0 = jnp.where(nonneg, jnp.int32(0), int_min)

    gmax = lax.fori_loop(0, n_live, lambda j, g: jnp.maximum(g, key_ref[j]), jnp.full((tk, tq), int_min, I32))
    lo = jnp.min(gmax, axis=0, keepdims=True)
    hi = jnp.max(gmax, axis=0, keepdims=True)
    mag = jnp.int32(0x7FFFFFFF)
    open_bits = jnp.where((lo < 0) == (hi < 0), (lo ^ hi) & mag, mag)
    n_bits = jnp.max((32 - lax.clz(open_bits)).astype(F32)).astype(I32)
    low_mask = (jnp.int32(1) << n_bits) - 1
    prefix0 = prefix0 | (lo & mag & ~low_mask)

    def bit_body(b, prefix):
        cand = prefix | (jnp.int32(1) << (n_bits - 1 - b))
        return jnp.where(count_ge(cand) >= topk, cand, prefix)

    thr = lax.fori_loop(0, n_bits, bit_body, prefix0)

    need = topk - count_ge(thr + 1)
    lower = jnp.where(lax.broadcasted_iota(I32, (tk, tk), 0) >= lax.broadcasted_iota(I32, (tk, tk), 1),
                      1.0, 0.0).astype(BF16)

    def bias_body(j, seen):
        kk = key_ref[j]
        eq = kk == thr
        rank = _dot(lower, jnp.where(eq, 1.0, 0.0).astype(BF16)) + seen
        keep = ((kk > thr) | (eq & (rank <= need))) & jnp.logical_or(j < i, visible)
        bias_ref[pl.ds(pl.multiple_of(j * tk, tk), tk), :] = jnp.where(keep, 0.0, NEG).astype(BF16)
        return rank[tk - 1:tk, :]

    lax.fori_loop(0, n_live, bias_body, jnp.zeros((1, tq), F32))

    def fill_body(j, c):
        bias_ref[pl.ds(pl.multiple_of(j * tk, tk), tk), :] = jnp.full((tk, tq), NEG, BF16)
        return c

    lax.fori_loop(n_live, n_chunks, fill_body, 0)


def dsa_select(cast, cast_t, misc_t):
    t = cast.shape[0]
    tq = ATT_T
    w = IDX_HEADS * IDX_DIM
    return pl.pallas_call(
        _dsa_select_kernel,
        grid=(t // tq,),
        in_specs=[pl.BlockSpec((None, w, tq), lambda i: (i, COL_CQI // w, 0)),
                  pl.BlockSpec((t, w), lambda i: (0, COL_KI4 // w)),
                  pl.BlockSpec((None, LANES, tq), lambda i: (i, 0, 0))],
        out_specs=pl.BlockSpec((None, t, tq), lambda i: (i, 0, 0)),
        out_shape=jax.ShapeDtypeStruct((t // tq, t, tq), BF16),
        scratch_shapes=[pltpu.VMEM((t // ATT_T, ATT_T, tq), I32)],
        compiler_params=_cparams(("arbitrary",), VMEM_LIMIT),
        name="dsa_select",
    )(cast_t, cast, misc_t)


def _dsa_attn_kernel(qt_ref, k_ref, vt_ref, bias_ref, o_ref, s_ref):
    i = pl.program_id(1)
    tq = qt_ref.shape[1]
    tk = ATT_T
    nh = s_ref.shape[1]
    qh = _head_rows(qt_ref[...])

    def score_fn(n):
        off = pl.multiple_of(n * tk, tk)
        kb = k_ref[pl.ds(off, tk), :]
        bias = bias_ref[pl.ds(off, tk), :].astype(F32)
        return [_dot(kb[:, _pair_cols(h)], qh[h]) + bias for h in range(nh)]

    def vt_fn(n, h):
        return vt_ref[n, h * HEAD_DIM:(h + 1) * HEAD_DIM, :]

    o_ref[...] = _attention_tile(i, score_fn, vt_fn, s_ref, tq, None).astype(BF16)


def dsa_attention(cast, cast_t, bias):
    t = cast.shape[0]
    tq = ATT_T
    w = ATT_HEADS * HEAD_DIM
    return pl.pallas_call(
        _dsa_attn_kernel,
        grid=(N_HEADS // ATT_HEADS, t // tq),
        in_specs=[pl.BlockSpec((None, w, tq), lambda g, i: (i, COL_CQ // w + g, 0)),
                  pl.BlockSpec((t, w), lambda g, i: (0, COL_CK // w + g)),
                  pl.BlockSpec((t // tq, w, tq), lambda g, i: (0, COL_CV // w + g, 0)),
                  pl.BlockSpec((None, t, tq), lambda g, i: (i, 0, 0))],
        out_specs=pl.BlockSpec((tq, w), lambda g, i: (i, g)),
        out_shape=jax.ShapeDtypeStruct((t, ATT_W), BF16),
        scratch_shapes=[pltpu.VMEM((2, ATT_HEADS, ATT_T, tq), F32)],
        compiler_params=_cparams(("arbitrary", "arbitrary"), VMEM_LIMIT),
        name="dsa_attention",
    )(cast_t, cast, cast_t, bias)


def _rms(x, w):
    return x * lax.rsqrt(jnp.mean(x * x, axis=-1, keepdims=True) + LN_EPS) * w


def _mla_prep_kernel(cq_ref, ckv_ref, misc_ref, qn_ref, kvn_ref, wq_ref, wk_ref, wv_ref,
                     c_ref, s1_ref, s2_ref, qt_ref, k_ref, vt_ref, misct_ref):
    c, s1, s2 = c_ref[...], s1_ref[...], s2_ref[...]
    cq = _rms(cq_ref[...], qn_ref[...]).astype(BF16)
    ckv = _rms(ckv_ref[...], kvn_ref[...]).astype(BF16)
    misc = misc_ref[...]
    misct_ref[...] = misc.T
    lane = lax.broadcasted_iota(I32, (1, LANES), 1)
    kpe = pltpu.roll(jnp.where(lane < MLA_ROPE, misc, 0.0), MLA_NOPE, 1)
    kpe = _rot_apply(kpe, c, s1, s2, MLA_ROPE // 2)
    q = _dot(cq, wq_ref[...])
    k = _dot(ckv, wk_ref[...])
    for h in range(N_HEADS):
        sl = slice(h * LANES, (h + 1) * LANES)
        qt_ref[sl, :] = _rot_apply(q[:, sl], c, s1, s2, MLA_ROPE // 2).T.astype(BF16)
        k_ref[:, sl] = (k[:, sl] + kpe).astype(BF16)
    v = _dot(ckv, wv_ref[...])
    for g in range(ATT_W // LANES):
        sl = slice(g * LANES, (g + 1) * LANES)
        vt_ref[sl, :] = v[:, sl].T.astype(BF16)


def mla_prep(proj, qn, kvn, wq, wk, wv, tabs):
    t = proj.shape[0]
    tm = ATT_T
    full = lambda a: pl.BlockSpec(a.shape, lambda i: (0,) * a.ndim)
    row = lambda w, col: pl.BlockSpec((tm, w), lambda i: (i, col // w))
    tile_t = lambda n: pl.BlockSpec((None, n, tm), lambda i: (i, 0, 0))
    return pl.pallas_call(
        _mla_prep_kernel,
        grid=(t // tm,),
        in_specs=[row(MLA_Q_RANK, COL_CQL), row(MLA_KV_RANK, COL_CKV), row(LANES, COL_MISC),
                  full(qn), full(kvn), full(wq), full(wk), full(wv),
                  row(LANES, 0), row(LANES, 0), row(LANES, 0)],
        out_specs=[tile_t(N_HEADS * LANES),
                   pl.BlockSpec((tm, N_HEADS * LANES), lambda i: (i, 0)),
                   tile_t(ATT_W), tile_t(LANES)],
        out_shape=[jax.ShapeDtypeStruct((t // tm, N_HEADS * LANES, tm), BF16),
                   jax.ShapeDtypeStruct((t, N_HEADS * LANES), BF16),
                   jax.ShapeDtypeStruct((t // tm, ATT_W, tm), BF16),
                   jax.ShapeDtypeStruct((t // tm, LANES, tm), F32)],
        compiler_params=_cparams(("arbitrary",), VMEM_LIMIT),
        name="mla_prep",
    )(proj, proj, proj, qn, kvn, wq, wk, wv, *tabs)


def _mla_attn_kernel(qt_ref, k_ref, vt_ref, o_ref, s_ref):
    i = pl.program_id(1)
    tq = qt_ref.shape[1]
    tk = ATT_T
    nh = s_ref.shape[1]
    qh = [qt_ref[h * LANES:(h + 1) * LANES, :] for h in range(nh)]

    def score_fn(n):
        off = pl.multiple_of(n * tk, tk)
        return [_dot(k_ref[pl.ds(off, tk), h * LANES:(h + 1) * LANES], qh[h]) for h in range(nh)]

    def vt_fn(n, h):
        return vt_ref[n, h * HEAD_DIM:(h + 1) * HEAD_DIM, :]

    o_ref[...] = _attention_tile(i, score_fn, vt_fn, s_ref, tq, _causal_t(tk, tq)).astype(BF16)


def mla_attention(qt, k, vt):
    t = k.shape[0]
    tq = ATT_T
    wqk = ATT_HEADS * LANES
    wv = ATT_HEADS * HEAD_DIM
    return pl.pallas_call(
        _mla_attn_kernel,
        grid=(N_HEADS // ATT_HEADS, t // tq),
        in_specs=[pl.BlockSpec((None, wqk, tq), lambda g, i: (i, g, 0)),
                  pl.BlockSpec((t, wqk), lambda g, i: (0, g)),
                  pl.BlockSpec((t // tq, wv, tq), lambda g, i: (0, g, 0))],
        out_specs=pl.BlockSpec((tq, wv), lambda g, i: (i, g)),
        out_shape=jax.ShapeDtypeStruct((t, ATT_W), BF16),
        scratch_shapes=[pltpu.VMEM((2, ATT_HEADS, ATT_T, tq), F32)],
        compiler_params=_cparams(("arbitrary", "arbitrary"), VMEM_LIMIT),
        name="mla_attention",
    )(qt, k, vt)


def _silu(x):
    return x / (1.0 + jnp.exp(-x))


def _softplus(x):
    return jnp.maximum(x, 0.0) + jnp.log1p(jnp.exp(-jnp.abs(x)))


def _ssd_kernel(z_ref, xbc_ref, misc_ref, cw_ref, cb_ref, dtb_ref, a_ref, dsk_ref, nw_ref, exp_ref,
                o_ref, ext_ref, state_ref):
    c = pl.program_id(0)
    L = SSM_CHUNK
    half = SSM_INNER // SSM_GROUPS

    @pl.when(c == 0)
    def _():
        ext_ref[0:8, :] = jnp.zeros((8, SSM_CONV_CH), F32)
        state_ref[...] = jnp.zeros(state_ref.shape, F32)

    ext_ref[8:8 + L, :] = xbc_ref[...]
    conv = jnp.zeros((L, SSM_CONV_CH), F32) + cb_ref[...]
    for kk in range(SSM_CONV):
        conv = conv + cw_ref[kk:kk + 1, :] * ext_ref[pl.ds(8 - (SSM_CONV - 1) + kk, L), :]
    tail = ext_ref[L:L + 8, :]
    ext_ref[0:8, :] = tail
    xbc = _silu(conv)
    xs = xbc[:, 0:SSM_INNER]
    bm = xbc[:, SSM_INNER:SSM_INNER + SSM_GROUPS * SSM_STATE]
    cm = xbc[:, SSM_INNER + SSM_GROUPS * SSM_STATE:]

    dt = _softplus(misc_ref[...] + dtb_ref[...])
    da = dt * a_ref[...]
    r = lax.broadcasted_iota(I32, (L, L), 0)
    cc = lax.broadcasted_iota(I32, (L, L), 1)
    tri = r >= cc
    a_cum = _dot_hi(jnp.where(tri, 1.0, 0.0), da)
    a_cum_t = a_cum.T
    expand = exp_ref[...]
    dt_full = _dot_hi(dt, expand)
    acum_full = _dot_hi(a_cum, expand)
    a_last = acum_full[L - 1:L, :]
    xdt = xs * dt_full
    xdt_b = xdt.astype(BF16)

    ys = []
    for g in range(SSM_GROUPS):
        bg = bm[:, g * SSM_STATE:(g + 1) * SSM_STATE].astype(BF16)
        cg = cm[:, g * SSM_STATE:(g + 1) * SSM_STATE].astype(BF16)
        cb = _dot_nt(cg, bg)
        hpg = SSM_HEADS // SSM_GROUPS
        cols = []
        for hh in range(hpg):
            h = g * hpg + hh
            diff = a_cum[:, MISC_DT + h:MISC_DT + h + 1] - a_cum_t[MISC_DT + h:MISC_DT + h + 1, :]
            seg = jnp.where(tri, jnp.exp(jnp.where(tri, diff, 0.0)), 0.0)
            sc = (cb * seg).astype(BF16)
            pair = (h // 2) * LANES
            yp = _dot(sc, xdt_b[:, pair:pair + LANES])
            cols.append(yp[:, (h % 2) * SSM_HEAD_DIM:(h % 2 + 1) * SSM_HEAD_DIM])
        y_diag = jnp.concatenate(cols, axis=1)
        gs = slice(g * half, (g + 1) * half)
        st = state_ref[:, gs]
        y_off = _dot(cg, st.astype(BF16)) * jnp.exp(acum_full[:, gs])
        ys.append(y_diag + y_off)
        xw = (xdt[:, gs] * jnp.exp(a_last[:, gs] - acum_full[:, gs])).astype(BF16)
        new = _dot(bm[:, g * SSM_STATE:(g + 1) * SSM_STATE].T.astype(BF16), xw)
        state_ref[:, gs] = st * jnp.exp(a_last[:, gs]) + new
    y = jnp.concatenate(ys, axis=1) + xs * dsk_ref[...]
    y = y * _silu(z_ref[...])
    o_ref[...] = _rms(y, nw_ref[...]).astype(BF16)


def ssd_mixer(proj, conv_w, conv_b, dtb_row, a_row, dsk_row, norm_w, expand):
    t = proj.shape[0]
    L = SSM_CHUNK
    full = lambda a: pl.BlockSpec(a.shape, lambda c: (0,) * a.ndim)
    return pl.pallas_call(
        _ssd_kernel,
        grid=(t // L,),
        in_specs=[pl.BlockSpec((L, SSM_INNER), lambda c: (c, COL_MZ // SSM_INNER)),
                  pl.BlockSpec((L, SSM_CONV_CH), lambda c: (c, COL_XBC // SSM_CONV_CH)),
                  pl.BlockSpec((L, LANES), lambda c: (c, COL_MISC // LANES)),
                  full(conv_w), full(conv_b), full(dtb_row), full(a_row), full(dsk_row), full(norm_w),
                  full(expand)],
        out_specs=pl.BlockSpec((L, SSM_INNER), lambda c: (c, 0)),
        out_shape=jax.ShapeDtypeStruct((t, SSM_INNER), BF16),
        scratch_shapes=[pltpu.VMEM((L + 8, SSM_CONV_CH), F32),
                        pltpu.VMEM((SSM_STATE, SSM_INNER), F32)],
        compiler_params=_cparams(("arbitrary",), VMEM_LIMIT),
        name="ssd_mixer",
    )(proj, proj, proj, conv_w, conv_b, dtb_row, a_row, dsk_row, norm_w, expand)


def _merge_kernel(x_ref, wg0_ref, wg1_ref, wg2_ref, wg3_ref, bg_ref, oa_ref, ob_ref, oc_ref, od_ref, wb_ref, o_ref):
    x = x_ref[...]
    acc = None
    wg_refs = (wg0_ref, wg1_ref, wg2_ref, wg3_ref)
    for g, br in enumerate((oa_ref, ob_ref, oc_ref, od_ref)):
        logit = _dot(x, wg_refs[g][...]) + bg_ref[g:g + 1, :]
        gate = 1.0 / (1.0 + jnp.exp(-logit))
        term = gate * _dot(br[...], wb_ref[g])
        acc = term if acc is None else acc + term
    o_ref[...] = acc.astype(BF16)


def gated_merge(x_bf, wg, bg, branches, wb, tm=1024, tn=256):
    t = x_bf.shape[0]
    br_spec = pl.BlockSpec((tm, BRANCH_W), lambda j, i: (i, 0))
    nj = D_MODEL // tn
    wg_specs = [pl.BlockSpec((D_MODEL, tn), functools.partial(lambda j, i, g: (0, g * nj + j), g=g))
                for g in range(N_BRANCH)]
    return pl.pallas_call(
        _merge_kernel,
        grid=(nj, t // tm),
        in_specs=[pl.BlockSpec((tm, D_MODEL), lambda j, i: (i, 0)),
                  *wg_specs,
                  pl.BlockSpec((N_BRANCH, tn), lambda j, i: (0, j)),
                  br_spec, br_spec, br_spec, br_spec,
                  pl.BlockSpec((N_BRANCH, BRANCH_W, tn), lambda j, i: (0, 0, j))],
        out_specs=pl.BlockSpec((tm, tn), lambda j, i: (i, j)),
        out_shape=jax.ShapeDtypeStruct((t, D_MODEL), BF16),
        compiler_params=_cparams(("arbitrary", "arbitrary"), VMEM_LIMIT),
        name="gated_merge",
    )(x_bf, wg, wg, wg, wg, bg, *branches, wb)


def _layer_norm(v, g, b):
    mu = jnp.mean(v, axis=-1, keepdims=True)
    d = v - mu
    var = jnp.mean(d * d, axis=-1, keepdims=True)
    return d * lax.rsqrt(var + LN_EPS) * g + b


def _outproj_kernel(m_ref, w_ref, x_ref, g_ref, b_ref, rw_ref, rb_ref, x1_ref, x1b_ref, idx_ref, gate_ref):
    mixed = _dot(m_ref[...], w_ref[...])
    x1 = _layer_norm(DN_ALPHA * x_ref[...] + mixed, g_ref[...], b_ref[...])
    x1_ref[...] = x1
    x1b = x1.astype(BF16)
    x1b_ref[...] = x1b
    logits = _dot(x1b, rw_ref[...]) + rb_ref[...]
    lane = lax.broadcasted_iota(I32, logits.shape, 1)
    g = logits
    idx_out = jnp.zeros(logits.shape, I32)
    val_out = jnp.zeros(logits.shape, F32)
    vals = []
    for kk in range(TOP_K):
        m = jnp.max(g, axis=1, keepdims=True)
        idx = jnp.min(jnp.where(g == m, lane, LANES), axis=1, keepdims=True)
        g = jnp.where(lane == idx, -jnp.inf, g)
        idx_out = jnp.where(lane == kk, idx, idx_out)
        vals.append(m)
    es = [jnp.exp(v - vals[0]) for v in vals]
    den = es[0] + es[1] + es[2] + es[3]
    for kk in range(TOP_K):
        val_out = jnp.where(lane == kk, es[kk] / den, val_out)
    idx_ref[...] = idx_out
    gate_ref[...] = val_out


def outproj_ln_router(merged, w_out, x, g, b, rw, rb, tm=256):
    t = x.shape[0]
    full = lambda a: pl.BlockSpec(a.shape, lambda i: (0,) * a.ndim)
    row = lambda w: pl.BlockSpec((tm, w), lambda i: (i, 0))
    return pl.pallas_call(
        _outproj_kernel,
        grid=(t // tm,),
        in_specs=[row(D_MODEL), full(w_out), row(D_MODEL), full(g), full(b), full(rw), full(rb)],
        out_specs=[row(D_MODEL), row(D_MODEL), row(LANES), row(LANES)],
        out_shape=[jax.ShapeDtypeStruct((t, D_MODEL), F32), jax.ShapeDtypeStruct((t, D_MODEL), BF16),
                   jax.ShapeDtypeStruct((t, LANES), I32), jax.ShapeDtypeStruct((t, LANES), F32)],
        compiler_params=_cparams(("arbitrary",), VMEM_LIMIT),
        name="outproj_ln_router",
    )(merged, w_out, x, g, b, rw, rb)


def _start_row_gather(idx_ref, src_hbm, dst, sem, rows):
    for r in range(rows):
        pltpu.make_async_copy(src_hbm.at[pl.ds(idx_ref[0, r], 1)], dst.at[pl.ds(r, 1)], sem).start(priority=r % 2)


def _wait_row_gather(src_hbm, dst, sem, rows):
    pltpu.make_async_copy(src_hbm.at[pl.ds(0, rows)], dst, sem).wait()


def _expert_kernel(be_ref, nu_ref, tok_ref, tokn_ref, x_hbm, wgu_ref, bgu_ref, wd_ref, bd_ref, o_ref, xbuf, sem):
    blk = pl.program_id(0)
    n_used = nu_ref[0]
    slot = lax.rem(blk, 2)

    @pl.when(jnp.logical_and(blk == 0, n_used > 0))
    def _():
        _start_row_gather(tok_ref, x_hbm, xbuf.at[0], sem.at[0], MOE_ROWS)

    @pl.when(blk + 1 < n_used)
    def _():
        _start_row_gather(tokn_ref, x_hbm, xbuf.at[1 - slot], sem.at[1 - slot], MOE_ROWS)

    @pl.when(blk < n_used)
    def _():
        _wait_row_gather(x_hbm, xbuf.at[slot], sem.at[slot], MOE_ROWS)
        gu = _dot(xbuf[slot].astype(BF16), wgu_ref[0].astype(BF16)) + bgu_ref[0]
        glu = jnp.minimum(gu[:, :D_FF], SWIGLU_LIMIT)
        lin = jnp.clip(gu[:, D_FF:], -SWIGLU_LIMIT, SWIGLU_LIMIT)
        act = glu / (1.0 + jnp.exp(-SWIGLU_ALPHA * glu)) * (lin + 1.0)
        o_ref[...] = _dot(act.astype(BF16), wd_ref[0].astype(BF16)) + bd_ref[0]

    @pl.when(blk >= nu_ref[0])
    def _():
        o_ref[...] = jnp.zeros(o_ref.shape, F32)


def expert_ffn(block_e, n_used, row_tok, x, wgu, bgu, wd, bd):
    n_blocks = row_tok.shape[0]
    n_rows = n_blocks * MOE_ROWS
    grid_spec = pltpu.PrefetchScalarGridSpec(
        num_scalar_prefetch=2,
        grid=(n_blocks,),
        in_specs=[pl.BlockSpec((None, 1, MOE_ROWS), lambda b, be, nu: (b, 0, 0), memory_space=pltpu.SMEM),
                  pl.BlockSpec((None, 1, MOE_ROWS), lambda b, be, nu: (jnp.minimum(b + 1, n_blocks - 1), 0, 0),
                               memory_space=pltpu.SMEM),
                  pl.BlockSpec(memory_space=pl.ANY),
                  pl.BlockSpec((1, D_MODEL, 2 * D_FF), lambda b, be, nu: (be[b], 0, 0)),
                  pl.BlockSpec((1, 1, 2 * D_FF), lambda b, be, nu: (be[b], 0, 0)),
                  pl.BlockSpec((1, D_FF, D_MODEL), lambda b, be, nu: (be[b], 0, 0)),
                  pl.BlockSpec((1, 1, D_MODEL), lambda b, be, nu: (be[b], 0, 0))],
        out_specs=pl.BlockSpec((MOE_ROWS, D_MODEL), lambda b, be, nu: (b, 0)),
        scratch_shapes=[pltpu.VMEM((2, MOE_ROWS, D_MODEL), F32), pltpu.SemaphoreType.DMA((2,))],
    )
    return pl.pallas_call(
        _expert_kernel,
        grid_spec=grid_spec,
        out_shape=jax.ShapeDtypeStruct((n_rows, D_MODEL), F32),
        compiler_params=_cparams(("arbitrary",), VMEM_LIMIT),
        name="expert_ffn",
    )(block_e, n_used, row_tok, row_tok, x, wgu, bgu, wd, bd)


COMBINE_TM = 128


def _combine_kernel(pos_ref, posn_ref, out_hbm, gate_ref, x_ref, g_ref, b_ref, o_ref, ob_ref, rbuf, sem):
    i = pl.program_id(0)
    tm = x_ref.shape[0]
    rows = TOP_K * tm
    slot = lax.rem(i, 2)

    @pl.when(i == 0)
    def _():
        _start_row_gather(pos_ref, out_hbm, rbuf.at[0], sem.at[0], rows)

    @pl.when(i + 1 < pl.num_programs(0))
    def _():
        _start_row_gather(posn_ref, out_hbm, rbuf.at[1 - slot], sem.at[1 - slot], rows)

    _wait_row_gather(out_hbm, rbuf.at[slot], sem.at[slot], rows)
    gate = gate_ref[...]
    y = jnp.zeros(x_ref.shape, F32)
    for kk in range(TOP_K):
        y = y + gate[:, kk:kk + 1] * rbuf[slot, kk * tm:(kk + 1) * tm, :]
    x2 = _layer_norm(DN_ALPHA * x_ref[...] + y, g_ref[...], b_ref[...])
    o_ref[...] = x2
    ob_ref[...] = x2.astype(BF16)


def combine_ln(pos_km, out, gates, x1, g, b):
    t = x1.shape[0]
    tm = COMBINE_TM
    n_tiles = t // tm
    full = lambda a: pl.BlockSpec(a.shape, lambda i: (0,) * a.ndim)
    row = lambda w: pl.BlockSpec((tm, w), lambda i: (i, 0))
    return pl.pallas_call(
        _combine_kernel,
        grid=(n_tiles,),
        in_specs=[pl.BlockSpec((None, 1, TOP_K * tm), lambda i: (i, 0, 0), memory_space=pltpu.SMEM),
                  pl.BlockSpec((None, 1, TOP_K * tm), lambda i: (jnp.minimum(i + 1, n_tiles - 1), 0, 0),
                               memory_space=pltpu.SMEM),
                  pl.BlockSpec(memory_space=pl.ANY), row(LANES), row(D_MODEL), full(g), full(b)],
        out_specs=[row(D_MODEL), row(D_MODEL)],
        out_shape=[jax.ShapeDtypeStruct((t, D_MODEL), F32), jax.ShapeDtypeStruct((t, D_MODEL), BF16)],
        scratch_shapes=[pltpu.VMEM((2, TOP_K * tm, D_MODEL), F32), pltpu.SemaphoreType.DMA((2,))],
        compiler_params=_cparams(("arbitrary",), VMEM_LIMIT),
        name="combine_ln",
    )(pos_km, pos_km, out, gates, x1, g, b)


def _rope_tables(positions):
    pos = positions.reshape(-1).astype(F32)

    def cs(dim, theta):
        inv = theta ** (-jnp.arange(0, dim, 2, dtype=F32) / dim)
        ang = pos[:, None] * inv
        return jnp.cos(ang), jnp.sin(ang)

    t = pos.shape[0]
    cos_p, sin_p = cs(ROT_DIM, ROPE_THETA)
    half = ROT_DIM // 2
    one = jnp.ones((t, HEAD_DIM - ROT_DIM), F32)
    zero = jnp.zeros((t, HEAD_DIM - ROT_DIM), F32)
    zh = jnp.zeros((t, half), F32)
    c64 = jnp.concatenate([cos_p, cos_p, one], axis=1)
    s1_64 = jnp.concatenate([-sin_p, zh, zero], axis=1)
    s2_64 = jnp.concatenate([zh, sin_p, zero], axis=1)
    tabs_p = tuple(jnp.tile(a, (1, 2)) for a in (c64, s1_64, s2_64))

    cos_m, sin_m = cs(MLA_ROPE, MLA_THETA)
    hm = MLA_ROPE // 2
    pre1 = jnp.ones((t, MLA_NOPE), F32)
    pre0 = jnp.zeros((t, MLA_NOPE), F32)
    post1 = jnp.ones((t, LANES - MLA_NOPE - MLA_ROPE), F32)
    post0 = jnp.zeros((t, LANES - MLA_NOPE - MLA_ROPE), F32)
    zm = jnp.zeros((t, hm), F32)
    c_m = jnp.concatenate([pre1, cos_m, cos_m, post1], axis=1)
    s1_m = jnp.concatenate([pre0, -sin_m, zm, post0], axis=1)
    s2_m = jnp.concatenate([pre0, zm, sin_m, post0], axis=1)
    return tabs_p, (c_m, s1_m, s2_m)


def _prep_in_weights(w_in):
    pts = np.cumsum((0,) + SPLIT_SIZES)
    seg = lambda k: w_in[:, pts[k]:pts[k + 1]]
    (a_q, a_k, a_v, m_z, m_xbc, m_dt, c_q, c_k, c_v, c_qi, c_ki, c_wi, d_cq, d_ckv, d_kpe) = [seg(k) for k in range(15)]
    scale = HEAD_DIM ** -0.5 * LOG2E
    zeros = lambda n: jnp.zeros((D_MODEL, n), w_in.dtype)
    cols = [a_q * scale, a_k, a_v, c_q * scale, c_k, c_v, c_qi, c_ki, c_ki, c_ki, c_ki,
            m_z, m_xbc, d_ckv, d_cq, d_kpe, m_dt, c_wi * IDX_SCALE]
    small = jnp.concatenate(cols, axis=1)
    small = jnp.concatenate([small, zeros(N_SMALL - small.shape[1])], axis=1).astype(BF16)
    wg = w_in[:, pts[15]:].astype(BF16)
    return small, wg


def _prep_mla_weights(w_uq, w_ukv):
    dq = MLA_NOPE + MLA_ROPE
    wq = w_uq.reshape(MLA_Q_RANK, N_HEADS, dq) * (dq ** -0.5 * LOG2E)
    wq = jnp.pad(wq, ((0, 0), (0, 0), (0, LANES - dq))).reshape(MLA_Q_RANK, N_HEADS * LANES)
    wkv = w_ukv.reshape(MLA_KV_RANK, N_HEADS, MLA_NOPE + MLA_V)
    wk = jnp.pad(wkv[:, :, :MLA_NOPE], ((0, 0), (0, 0), (0, LANES - MLA_NOPE))).reshape(MLA_KV_RANK, N_HEADS * LANES)
    wv = wkv[:, :, MLA_NOPE:].reshape(MLA_KV_RANK, N_HEADS * MLA_V)
    return wq.astype(BF16), wk.astype(BF16), wv.astype(BF16)


def _misc_row(v):
    return jnp.zeros((1, LANES), F32).at[0, MISC_DT:MISC_DT + SSM_HEADS].set(v.astype(F32))


def _route(top_idx, t):
    tk = t * TOP_K
    flat_e = top_idx.reshape(-1)
    onehot = (flat_e[:, None] == jnp.arange(N_EXPERTS)[None, :]).astype(I32)
    seen = jnp.cumsum(onehot, axis=0)
    counts = seen[-1]
    padded = ((counts + MOE_ROWS - 1) // MOE_ROWS) * MOE_ROWS
    pad_end = jnp.cumsum(padded)
    pad_start = pad_end - padded
    rank = jnp.sum(onehot * seen, axis=1) - 1
    pos = (pad_start[flat_e] + rank).astype(I32)
    n_blocks = tk // MOE_ROWS + N_EXPERTS
    n_rows = n_blocks * MOE_ROWS
    row_tok = jnp.zeros((n_rows,), I32).at[pos].set(jnp.arange(tk, dtype=I32) // TOP_K)
    block_start = jnp.arange(n_blocks) * MOE_ROWS
    block_e = jnp.minimum(jnp.sum(pad_end[None, :] <= block_start[:, None], axis=1), N_EXPERTS - 1).astype(I32)
    n_used = (pad_end[-1] // MOE_ROWS).astype(I32).reshape(1)
    return row_tok, pos, block_e, n_used


def _layer(x, x_bf, tabs_p, tabs_m, flags, expand, p):
    t = x.shape[0]
    w_small, wg = _prep_in_weights(p["w_in"])
    proj = matmul_bf16(x_bf, w_small)
    cast, cast_t = rope_cast(proj, flags, tabs_p)

    o_a = moba_attention(cast, cast_t)
    o_b = ssd_mixer(proj, p["conv_w"], p["conv_b"].reshape(1, -1), _misc_row(p["dt_bias"]),
                    _misc_row(-jnp.exp(p["a_log"].astype(F32))),
                    jnp.repeat(p["d_skip"].astype(F32), SSM_HEAD_DIM).reshape(1, -1),
                    p["ssm_norm_w"].reshape(1, -1), expand)
    wq, wk, wv = _prep_mla_weights(p["w_uq"], p["w_ukv"])
    qt_m, k_m, vt_m, misc_t = mla_prep(proj, p["q_norm_w"].reshape(1, -1), p["kv_norm_w"].reshape(1, -1), wq, wk, wv,
                                       tabs_m)
    o_c = dsa_attention(cast, cast_t, dsa_select(cast, cast_t, misc_t))
    o_d = mla_attention(qt_m, k_m, vt_m)

    merged = gated_merge(x_bf, wg, p["b_gate"], (o_a, o_b, o_c, o_d), p["w_branch"].astype(BF16))
    rw = jnp.pad(p["router_w"], ((0, 0), (0, LANES - N_EXPERTS))).astype(BF16)
    rb = jnp.concatenate([p["router_b"].astype(F32), jnp.full((LANES - N_EXPERTS,), -jnp.inf, F32)]).reshape(1, -1)
    x1, x1_bf, top_idx, gates = outproj_ln_router(merged, p["w_out"].astype(BF16), x, p["ln1_g"].reshape(1, -1),
                                                  p["ln1_b"].reshape(1, -1), rw, rb)

    row_tok, pos, block_e, n_used = _route(top_idx[:, :TOP_K], t)
    out = expert_ffn(block_e, n_used, row_tok.reshape(-1, 1, MOE_ROWS), x1, p["w_gate_up"],
                     p["b_gate_up"][:, None, :], p["w_down"], p["b_down"][:, None, :])
    pos_km = pos.reshape(t // COMBINE_TM, COMBINE_TM, TOP_K).transpose(0, 2, 1).reshape(-1, 1, TOP_K * COMBINE_TM)
    return combine_ln(pos_km, out, gates, x1, p["ln2_g"].reshape(1, -1), p["ln2_b"].reshape(1, -1))


def kernel(x, positions, w_in, b_gate, conv_w, conv_b, dt_bias, a_log, d_skip, ssm_norm_w, q_norm_w, w_uq,
           kv_norm_w, w_ukv, w_branch, w_out, ln1_g, ln1_b, router_w, router_b, w_gate_up, b_gate_up, w_down,
           b_down, ln2_g, ln2_b):
    b, s, d = x.shape
    assert b == 1 and d == D_MODEL
    names = ("w_in", "b_gate", "conv_w", "conv_b", "dt_bias", "a_log", "d_skip", "ssm_norm_w", "q_norm_w", "w_uq",
             "kv_norm_w", "w_ukv", "w_branch", "w_out", "ln1_g", "ln1_b", "router_w", "router_b", "w_gate_up",
             "b_gate_up", "w_down", "b_down", "ln2_g", "ln2_b")
    stacked = (w_in, b_gate, conv_w, conv_b, dt_bias, a_log, d_skip, ssm_norm_w, q_norm_w, w_uq, kv_norm_w, w_ukv,
               w_branch, w_out, ln1_g, ln1_b, router_w, router_b, w_gate_up, b_gate_up, w_down, b_down, ln2_g, ln2_b)
    tabs_p, tabs_m = _rope_tables(positions)
    flags = jnp.array([1, 1, 0, 1, 1, 0, 1], I32)
    head_of_lane = jnp.arange(SSM_INNER) // SSM_HEAD_DIM
    expand = (jnp.arange(LANES)[:, None] == (MISC_DT + head_of_lane)[None, :]).astype(F32)

    xt = x.reshape(s, d)
    xt_bf = xt.astype(BF16)
    for l in range(DEPTH):
        p = {n: a[l] for n, a in zip(names, stacked)}
        xt, xt_bf = _layer(xt, xt_bf, tabs_p, tabs_m, flags, expand, p)
    return xt.reshape(b, s, d)
```

```python
import functools
import math

import jax
import jax.numpy as jnp
import numpy as np
from jax import lax
from jax.experimental import pallas as pl
from jax.experimental.pallas import tpu as pltpu

F32 = jnp.float32
BF16 = jnp.bfloat16
I32 = jnp.int32

D_MODEL = 2048
DEPTH = 4
HEAD_DIM = 64
ROT_DIM = HEAD_DIM // 4
ROPE_THETA = 500000.0
N_HEADS = 8
ATT_W = N_HEADS * HEAD_DIM
MOBA_BLOCK = 256
MOBA_TOPK = 3
SSM_HEADS = 8
SSM_HEAD_DIM = 64
SSM_INNER = 512
SSM_STATE = 128
SSM_GROUPS = 2
SSM_CONV = 4
SSM_CONV_CH = 1024
SSM_CHUNK = 256
IDX_HEADS = 4
IDX_DIM = 64
DSA_TOPK = 256
IDX_SCALE = (IDX_HEADS * IDX_DIM) ** -0.5
MLA_Q_RANK = 384
MLA_KV_RANK = 256
MLA_NOPE = 64
MLA_ROPE = 32
MLA_V = 64
MLA_THETA = 10000.0
N_BRANCH = 4
BRANCH_W = 512
SPLIT_SIZES = (512, 512, 512, 512, 1024, 8, 512, 512, 512, 256, 64, 4, 384, 256, 32, N_BRANCH * D_MODEL)
N_EXPERTS = 32
TOP_K = 4
D_FF = 768
SWIGLU_ALPHA = 1.702
SWIGLU_LIMIT = 7.0
DN_ALPHA = (2 * DEPTH) ** 0.25
LN_EPS = 1e-5

LANES = 128
NEG = -1e30
VMEM_LIMIT = 56 * 1024 * 1024

COL_AQ, COL_AK, COL_AV = 0, 512, 1024
COL_CQ, COL_CK, COL_CV = 1536, 2048, 2560
COL_CQI, COL_KI4 = 3072, 3328
COL_MZ = 3584
COL_XBC = 4096
COL_CKV = 5120
COL_CQL = 5376
COL_MISC = 5760
N_SMALL = 6144
N_CAST = 3584
MISC_DT = 32
MISC_WI = 40
MOE_ROWS = 256


def _cparams(sem, vmem=None):
    return pltpu.CompilerParams(dimension_semantics=sem, vmem_limit_bytes=vmem)


def _dot(a, b):
    return jnp.dot(a, b, preferred_element_type=F32)


def _dot_nt(a, b):
    return lax.dot_general(a, b, (((1,), (1,)), ((), ())), preferred_element_type=F32)


def _dot_hi(a, b):
    return jnp.dot(a, b, preferred_element_type=F32, precision=lax.Precision.HIGHEST)


def _mm_kernel(x_ref, w_ref, o_ref):
    o_ref[...] = _dot(x_ref[...], w_ref[...])


def matmul_bf16(x, w, tm=512, tn=1024):
    m, k = x.shape
    n = w.shape[1]
    return pl.pallas_call(
        _mm_kernel,
        grid=(n // tn, m // tm),
        in_specs=[pl.BlockSpec((tm, k), lambda j, i: (i, 0)),
                  pl.BlockSpec((k, tn), lambda j, i: (0, j))],
        out_specs=pl.BlockSpec((tm, tn), lambda j, i: (i, j)),
        out_shape=jax.ShapeDtypeStruct((m, n), F32),
        compiler_params=_cparams(("arbitrary", "arbitrary"), VMEM_LIMIT),
        name="in_proj",
    )(x, w)


def _rot_apply(x, c, s1, s2, shift):
    return x * c + pltpu.roll(x, LANES - shift, 1) * s1 + pltpu.roll(x, shift, 1) * s2


def _rope_cast_kernel(flag_ref, x_ref, c_ref, s1_ref, s2_ref, o_ref, ot_ref):
    j = pl.program_id(0)
    width = x_ref.shape[1]

    def emit(g, y):
        sl = slice(g * LANES, (g + 1) * LANES)
        o_ref[:, sl] = y.astype(BF16)
        ot_ref[sl, :] = y.T.astype(BF16)

    @pl.when(flag_ref[j] == 1)
    def _():
        c, s1, s2 = c_ref[...], s1_ref[...], s2_ref[...]
        for g in range(width // LANES):
            emit(g, _rot_apply(x_ref[:, g * LANES:(g + 1) * LANES], c, s1, s2, ROT_DIM // 2))

    @pl.when(flag_ref[j] == 0)
    def _():
        for g in range(width // LANES):
            emit(g, x_ref[:, g * LANES:(g + 1) * LANES])


def rope_cast(proj, flags, tabs, tw=512):
    t = proj.shape[0]
    tm = ATT_T
    grid_spec = pltpu.PrefetchScalarGridSpec(
        num_scalar_prefetch=1,
        grid=(N_CAST // tw, t // tm),
        in_specs=[pl.BlockSpec((tm, tw), lambda j, i, f: (i, j)),
                  pl.BlockSpec((tm, LANES), lambda j, i, f: (i, 0)),
                  pl.BlockSpec((tm, LANES), lambda j, i, f: (i, 0)),
                  pl.BlockSpec((tm, LANES), lambda j, i, f: (i, 0))],
        out_specs=[pl.BlockSpec((tm, tw), lambda j, i, f: (i, j)),
                   pl.BlockSpec((None, tw, tm), lambda j, i, f: (i, j, 0))],
    )
    return pl.pallas_call(
        _rope_cast_kernel,
        grid_spec=grid_spec,
        out_shape=[jax.ShapeDtypeStruct((t, N_CAST), BF16),
                   jax.ShapeDtypeStruct((t // tm, N_CAST, tm), BF16)],
        compiler_params=_cparams(("arbitrary", "arbitrary")),
        name="rope_cast",
    )(flags, proj, *tabs)


ATT_T = 256
ATT_HEADS = 4


LOG2E = 1.4426950408889634
ACC_ROWS = HEAD_DIM + 16


def _softmax_step(s, smax, vt, state, keep=None):
    m, acc = state
    if keep is not None:
        smax = jnp.where(keep, smax, NEG)
    m_new = jnp.maximum(m, smax)
    shift = m_new if keep is None else jnp.where(keep, m_new, -NEG)
    p = jnp.exp2(s - shift).astype(BF16)
    vt_ext = jnp.concatenate([vt, jnp.ones((ACC_ROWS - HEAD_DIM, vt.shape[1]), vt.dtype)], axis=0)
    acc_new = jnp.exp2(m - m_new) * acc + _dot(vt_ext, p)
    return m_new, acc_new


def _init_state(tq):
    return (jnp.full((1, tq), NEG, F32), jnp.zeros((ACC_ROWS, tq), F32))


def _causal_t(tk, tq):
    return lax.broadcasted_iota(I32, (tk, tq), 0) <= lax.broadcasted_iota(I32, (tk, tq), 1)


def _head_rows(qt):
    row = lax.broadcasted_iota(I32, (LANES, 1), 0)
    out = []
    for p in range(qt.shape[0] // LANES):
        blk = qt[p * LANES:(p + 1) * LANES, :]
        zero = jnp.zeros_like(blk)
        out += [jnp.where(row < HEAD_DIM, blk, zero), jnp.where(row >= HEAD_DIM, blk, zero)]
    return out


def _pair_cols(h):
    return slice((h // 2) * LANES, (h // 2 + 1) * LANES)


def _attention_tile(i, score_fn, vt_fn, s_ref, tq, last_mask, keep_fn=None):
    nh = s_ref.shape[1]

    def park(scores, buf):
        for h in range(nh):
            s_ref[buf, h] = scores[h]
        return [jnp.max(sc, axis=0, keepdims=True) for sc in scores]

    def step(n, carry, buf, nxt_buf):
        nxt = score_fn(n + 1)
        states = []
        for h in range(nh):
            smax, m, acc = carry[3 * h:3 * h + 3]
            keep = None if keep_fn is None else keep_fn(n, h)
            states.append(_softmax_step(s_ref[buf, h], smax, vt_fn(n, h), (m, acc), keep))
        smax_next = park(nxt, nxt_buf)
        out = []
        for h in range(nh):
            out.extend((smax_next[h],) + tuple(states[h]))
        return tuple(out)

    def pair(mm, carry):
        return step(2 * mm + 1, step(2 * mm, carry, 0, 1), 1, 0)

    init = []
    for smax in park(score_fn(0), 0):
        init.extend((smax,) + _init_state(tq))
    carry = lax.fori_loop(0, i // 2, pair, tuple(init))
    carry = lax.cond(i % 2 == 1, lambda c: step(i - 1, c, 0, 0), lambda c: c, carry)
    fin = []
    for h in range(nh):
        smax, m, acc = carry[3 * h:3 * h + 3]
        s = s_ref[0, h]
        if last_mask is not None:
            s = jnp.where(last_mask, s, NEG)
            smax = jnp.max(s, axis=0, keepdims=True)
        _, acc = _softmax_step(s, smax, vt_fn(i, h), (m, acc))
        fin.append(acc[0:HEAD_DIM] / acc[HEAD_DIM:HEAD_DIM + 1])
    return jnp.concatenate(fin, axis=0).T


def _top_n_rows(g, valid, n):
    row = lax.broadcasted_iota(I32, g.shape, 0)
    g = jnp.where(valid, g, -jnp.inf)
    sel = jnp.zeros(g.shape, jnp.bool_)
    for _ in range(n):
        m = jnp.max(g, axis=0, keepdims=True)
        idx = jnp.min(jnp.where(g == m, row, g.shape[0]), axis=0, keepdims=True)
        pick = row == idx
        sel = jnp.logical_or(sel, pick)
        g = jnp.where(pick, -jnp.inf, g)
    return jnp.logical_and(sel, valid)


def _moba_kernel(qt_ref, k_ref, vt_ref, o_ref, kmean_ref, sel_ref, s_ref):
    i = pl.program_id(1)
    tq = qt_ref.shape[1]
    nb = k_ref.shape[0] // MOBA_BLOCK

    @pl.when(i == 0)
    def _():
        kmean_ref[...] = jnp.zeros(kmean_ref.shape, F32)

        def mean_body(n, c):
            kb = k_ref[pl.ds(pl.multiple_of(n * MOBA_BLOCK, MOBA_BLOCK), MOBA_BLOCK), :].astype(F32)
            kmean_ref[pl.ds(n, 1), :] = jnp.sum(kb, axis=0, keepdims=True) * (1.0 / MOBA_BLOCK)
            return c

        lax.fori_loop(0, nb, mean_body, 0)

    nh = s_ref.shape[1]
    qh = _head_rows(qt_ref[...])
    kmean = kmean_ref[...].astype(BF16)
    row_n = lax.broadcasted_iota(I32, (kmean.shape[0], tq), 0)
    for h in range(nh):
        gate = _dot(kmean[:, _pair_cols(h)], qh[h])
        sel_ref[h] = jnp.where(_top_n_rows(gate, row_n < i, MOBA_TOPK), 1.0, 0.0)

    def score_fn(n):
        kb = k_ref[pl.ds(pl.multiple_of(n * MOBA_BLOCK, MOBA_BLOCK), MOBA_BLOCK), :]
        return [_dot(kb[:, _pair_cols(h)], qh[h]) for h in range(nh)]

    def vt_fn(n, h):
        return vt_ref[n, h * HEAD_DIM:(h + 1) * HEAD_DIM, :]

    def keep_fn(n, h):
        return sel_ref[h, pl.ds(n, 1), :] > 0.5

    o_ref[...] = _attention_tile(i, score_fn, vt_fn, s_ref, tq, _causal_t(MOBA_BLOCK, tq), keep_fn).astype(BF16)


def moba_attention(cast, cast_t):
    t = cast.shape[0]
    tq = MOBA_BLOCK
    w = ATT_HEADS * HEAD_DIM
    nbp = max(8, t // MOBA_BLOCK)
    return pl.pallas_call(
        _moba_kernel,
        grid=(N_HEADS // ATT_HEADS, t // tq),
        in_specs=[pl.BlockSpec((None, w, tq), lambda g, i: (i, COL_AQ // w + g, 0)),
                  pl.BlockSpec((t, w), lambda g, i: (0, COL_AK // w + g)),
                  pl.BlockSpec((t // tq, w, tq), lambda g, i: (0, COL_AV // w + g, 0))],
        out_specs=pl.BlockSpec((tq, w), lambda g, i: (i, g)),
        out_shape=jax.ShapeDtypeStruct((t, ATT_W), BF16),
        scratch_shapes=[pltpu.VMEM((nbp, w), F32), pltpu.VMEM((ATT_HEADS, nbp, tq), F32),
                        pltpu.VMEM((2, ATT_HEADS, MOBA_BLOCK, tq), F32)],
        compiler_params=_cparams(("arbitrary", "arbitrary"), VMEM_LIMIT),
        name="moba_attention",
    )(cast_t, cast, cast_t)


def _float_key(x):
    b = lax.bitcast_convert_type(x, I32)
    return jnp.where(b < 0, b ^ jnp.int32(0x7FFFFFFF), b)


def _dsa_select_kernel(qit_ref, ki_ref, misct_ref, bias_ref, key_ref):
    i = pl.program_id(0)
    tq = qit_ref.shape[1]
    tk = ATT_T
    n_chunks = ki_ref.shape[0] // tk
    n_live = i + 1

    qit = qit_ref[...]
    row = lax.broadcasted_iota(I32, (IDX_HEADS * IDX_DIM, 1), 0)
    zero = jnp.zeros_like(qit)
    qh = [jnp.where((row >= h * IDX_DIM) & (row < (h + 1) * IDX_DIM), qit, zero) for h in range(IDX_HEADS)]
    wrow = [misct_ref[MISC_WI + h:MISC_WI + h + 1, :] for h in range(IDX_HEADS)]
    visible = _causal_t(tk, tq)

    def score_body(j, c):
        kic = ki_ref[pl.ds(pl.multiple_of(j * tk, tk), tk), :]
        acc = jnp.zeros((tk, tq), F32)
        for h in range(IDX_HEADS):
            acc = acc + wrow[h] * jnp.maximum(_dot(kic, qh[h]), 0.0)
        acc = jnp.where(jnp.logical_or(j < i, visible), acc, -jnp.inf)
        key_ref[j] = _float_key(acc)
        return c

    lax.fori_loop(0, n_live, score_body, 0)

    def count_ge(cand):
        def body(j, cnt):
            for r in range(tk // 8):
                cnt = cnt + jnp.where(key_ref[j, r * 8:(r + 1) * 8, :] >= cand, 1.0, 0.0)
            return cnt
        cnt = lax.fori_loop(0, n_live, body, jnp.zeros((8, tq), F32))
        return jnp.sum(cnt, axis=0, keepdims=True)

    topk = float(DSA_TOPK)
    int_min = jnp.int32(-2 ** 31)
    nonneg = count_ge(jnp.zeros((1, tq), I32)) >= topk

    gmax = lax.fori_loop(0, n_live, lambda j, g: jnp.maximum(g, key_ref[j]), jnp.full((tk, tq), int_min, I32))
    lo = jnp.min(gmax, axis=0, keepdims=True)
    hi = jnp.max(gmax, axis=0, keepdims=True)
    lo = jnp.where(nonneg, jnp.maximum(lo, 0), lo)
    hi = jnp.where(nonneg, hi, jnp.minimum(hi, -1))
    width = hi - lo
    n_bits = jnp.max((32 - lax.clz(width)).astype(F32)).astype(I32)

    def bit_body(b, off):
        cand = off | (jnp.int32(1) << (n_bits - 1 - b))
        ok = jnp.logical_and(cand <= width, count_ge(lo + jnp.minimum(cand, width)) >= topk)
        return jnp.where(ok, cand, off)

    thr = lo + lax.fori_loop(0, n_bits, bit_body, jnp.zeros((1, tq), I32))

    need = topk - count_ge(thr + 1)
    lower = jnp.where(lax.broadcasted_iota(I32, (tk, tk), 0) >= lax.broadcasted_iota(I32, (tk, tk), 1),
                      1.0, 0.0).astype(BF16)

    def bias_body(j, seen):
        kk = key_ref[j]
        eq = kk == thr
        rank = _dot(lower, jnp.where(eq, 1.0, 0.0).astype(BF16)) + seen
        keep = ((kk > thr) | (eq & (rank <= need))) & jnp.logical_or(j < i, visible)
        bias_ref[pl.ds(pl.multiple_of(j * tk, tk), tk), :] = jnp.where(keep, 0.0, NEG).astype(BF16)
        return rank[tk - 1:tk, :]

    lax.fori_loop(0, n_live, bias_body, jnp.zeros((1, tq), F32))

    def fill_body(j, c):
        bias_ref[pl.ds(pl.multiple_of(j * tk, tk), tk), :] = jnp.full((tk, tq), NEG, BF16)
        return c

    lax.fori_loop(n_live, n_chunks, fill_body, 0)


def dsa_select(cast, cast_t, misc_t):
    t = cast.shape[0]
    tq = ATT_T
    w = IDX_HEADS * IDX_DIM
    return pl.pallas_call(
        _dsa_select_kernel,
        grid=(t // tq,),
        in_specs=[pl.BlockSpec((None, w, tq), lambda i: (i, COL_CQI // w, 0)),
                  pl.BlockSpec((t, w), lambda i: (0, COL_KI4 // w)),
                  pl.BlockSpec((None, LANES, tq), lambda i: (i, 0, 0))],
        out_specs=pl.BlockSpec((None, t, tq), lambda i: (i, 0, 0)),
        out_shape=jax.ShapeDtypeStruct((t // tq, t, tq), BF16),
        scratch_shapes=[pltpu.VMEM((t // ATT_T, ATT_T, tq), I32)],
        compiler_params=_cparams(("arbitrary",), VMEM_LIMIT),
        name="dsa_select",
    )(cast_t, cast, misc_t)


def _dsa_attn_kernel(qt_ref, k_ref, vt_ref, bias_ref, o_ref, s_ref):
    i = pl.program_id(1)
    tq = qt_ref.shape[1]
    tk = ATT_T
    nh = s_ref.shape[1]
    qh = _head_rows(qt_ref[...])

    def score_fn(n):
        off = pl.multiple_of(n * tk, tk)
        kb = k_ref[pl.ds(off, tk), :]
        bias = bias_ref[pl.ds(off, tk), :].astype(F32)
        return [_dot(kb[:, _pair_cols(h)], qh[h]) + bias for h in range(nh)]

    def vt_fn(n, h):
        return vt_ref[n, h * HEAD_DIM:(h + 1) * HEAD_DIM, :]

    o_ref[...] = _attention_tile(i, score_fn, vt_fn, s_ref, tq, None).astype(BF16)


def dsa_attention(cast, cast_t, bias):
    t = cast.shape[0]
    tq = ATT_T
    w = ATT_HEADS * HEAD_DIM
    return pl.pallas_call(
        _dsa_attn_kernel,
        grid=(N_HEADS // ATT_HEADS, t // tq),
        in_specs=[pl.BlockSpec((None, w, tq), lambda g, i: (i, COL_CQ // w + g, 0)),
                  pl.BlockSpec((t, w), lambda g, i: (0, COL_CK // w + g)),
                  pl.BlockSpec((t // tq, w, tq), lambda g, i: (0, COL_CV // w + g, 0)),
                  pl.BlockSpec((None, t, tq), lambda g, i: (i, 0, 0))],
        out_specs=pl.BlockSpec((tq, w), lambda g, i: (i, g)),
        out_shape=jax.ShapeDtypeStruct((t, ATT_W), BF16),
        scratch_shapes=[pltpu.VMEM((2, ATT_HEADS, ATT_T, tq), F32)],
        compiler_params=_cparams(("arbitrary", "arbitrary"), VMEM_LIMIT),
        name="dsa_attention",
    )(cast_t, cast, cast_t, bias)


def _rms(x, w):
    return x * lax.rsqrt(jnp.mean(x * x, axis=-1, keepdims=True) + LN_EPS) * w


def _mla_prep_kernel(cq_ref, ckv_ref, misc_ref, qn_ref, kvn_ref, wq_ref, wk_ref, wv_ref,
                     c_ref, s1_ref, s2_ref, qt_ref, k_ref, vt_ref, misct_ref):
    c, s1, s2 = c_ref[...], s1_ref[...], s2_ref[...]
    cq = _rms(cq_ref[...], qn_ref[...]).astype(BF16)
    ckv = _rms(ckv_ref[...], kvn_ref[...]).astype(BF16)
    misc = misc_ref[...]
    misct_ref[...] = misc.T
    lane = lax.broadcasted_iota(I32, (1, LANES), 1)
    kpe = pltpu.roll(jnp.where(lane < MLA_ROPE, misc, 0.0), MLA_NOPE, 1)
    kpe = _rot_apply(kpe, c, s1, s2, MLA_ROPE // 2)
    q = _dot(cq, wq_ref[...])
    k = _dot(ckv, wk_ref[...])
    for h in range(N_HEADS):
        sl = slice(h * LANES, (h + 1) * LANES)
        qt_ref[sl, :] = _rot_apply(q[:, sl], c, s1, s2, MLA_ROPE // 2).T.astype(BF16)
        k_ref[:, sl] = (k[:, sl] + kpe).astype(BF16)
    v = _dot(ckv, wv_ref[...])
    for g in range(ATT_W // LANES):
        sl = slice(g * LANES, (g + 1) * LANES)
        vt_ref[sl, :] = v[:, sl].T.astype(BF16)


def mla_prep(proj, qn, kvn, wq, wk, wv, tabs):
    t = proj.shape[0]
    tm = ATT_T
    full = lambda a: pl.BlockSpec(a.shape, lambda i: (0,) * a.ndim)
    row = lambda w, col: pl.BlockSpec((tm, w), lambda i: (i, col // w))
    tile_t = lambda n: pl.BlockSpec((None, n, tm), lambda i: (i, 0, 0))
    return pl.pallas_call(
        _mla_prep_kernel,
        grid=(t // tm,),
        in_specs=[row(MLA_Q_RANK, COL_CQL), row(MLA_KV_RANK, COL_CKV), row(LANES, COL_MISC),
                  full(qn), full(kvn), full(wq), full(wk), full(wv),
                  row(LANES, 0), row(LANES, 0), row(LANES, 0)],
        out_specs=[tile_t(N_HEADS * LANES),
                   pl.BlockSpec((tm, N_HEADS * LANES), lambda i: (i, 0)),
                   tile_t(ATT_W), tile_t(LANES)],
        out_shape=[jax.ShapeDtypeStruct((t // tm, N_HEADS * LANES, tm), BF16),
                   jax.ShapeDtypeStruct((t, N_HEADS * LANES), BF16),
                   jax.ShapeDtypeStruct((t // tm, ATT_W, tm), BF16),
                   jax.ShapeDtypeStruct((t // tm, LANES, tm), F32)],
        compiler_params=_cparams(("arbitrary",), VMEM_LIMIT),
        name="mla_prep",
    )(proj, proj, proj, qn, kvn, wq, wk, wv, *tabs)


def _mla_attn_kernel(qt_ref, k_ref, vt_ref, o_ref, s_ref):
    i = pl.program_id(1)
    tq = qt_ref.shape[1]
    tk = ATT_T
    nh = s_ref.shape[1]
    qh = [qt_ref[h * LANES:(h + 1) * LANES, :] for h in range(nh)]

    def score_fn(n):
        off = pl.multiple_of(n * tk, tk)
        return [_dot(k_ref[pl.ds(off, tk), h * LANES:(h + 1) * LANES], qh[h]) for h in range(nh)]

    def vt_fn(n, h):
        return vt_ref[n, h * HEAD_DIM:(h + 1) * HEAD_DIM, :]

    o_ref[...] = _attention_tile(i, score_fn, vt_fn, s_ref, tq, _causal_t(tk, tq)).astype(BF16)


def mla_attention(qt, k, vt):
    t = k.shape[0]
    tq = ATT_T
    wqk = ATT_HEADS * LANES
    wv = ATT_HEADS * HEAD_DIM
    return pl.pallas_call(
        _mla_attn_kernel,
        grid=(N_HEADS // ATT_HEADS, t // tq),
        in_specs=[pl.BlockSpec((None, wqk, tq), lambda g, i: (i, g, 0)),
                  pl.BlockSpec((t, wqk), lambda g, i: (0, g)),
                  pl.BlockSpec((t // tq, wv, tq), lambda g, i: (0, g, 0))],
        out_specs=pl.BlockSpec((tq, wv), lambda g, i: (i, g)),
        out_shape=jax.ShapeDtypeStruct((t, ATT_W), BF16),
        scratch_shapes=[pltpu.VMEM((2, ATT_HEADS, ATT_T, tq), F32)],
        compiler_params=_cparams(("arbitrary", "arbitrary"), VMEM_LIMIT),
        name="mla_attention",
    )(qt, k, vt)


def _silu(x):
    return x / (1.0 + jnp.exp(-x))


def _softplus(x):
    return jnp.maximum(x, 0.0) + jnp.log1p(jnp.exp(-jnp.abs(x)))


def _ssd_kernel(z_ref, xbc_ref, misc_ref, cw_ref, cb_ref, dtb_ref, a_ref, dsk_ref, nw_ref, exp_ref,
                o_ref, ext_ref, state_ref):
    c = pl.program_id(0)
    L = SSM_CHUNK
    half = SSM_INNER // SSM_GROUPS

    @pl.when(c == 0)
    def _():
        ext_ref[0:8, :] = jnp.zeros((8, SSM_CONV_CH), F32)
        state_ref[...] = jnp.zeros(state_ref.shape, F32)

    ext_ref[8:8 + L, :] = xbc_ref[...]
    conv = jnp.zeros((L, SSM_CONV_CH), F32) + cb_ref[...]
    for kk in range(SSM_CONV):
        conv = conv + cw_ref[kk:kk + 1, :] * ext_ref[pl.ds(8 - (SSM_CONV - 1) + kk, L), :]
    tail = ext_ref[L:L + 8, :]
    ext_ref[0:8, :] = tail
    xbc = _silu(conv)
    xs = xbc[:, 0:SSM_INNER]
    bm = xbc[:, SSM_INNER:SSM_INNER + SSM_GROUPS * SSM_STATE]
    cm = xbc[:, SSM_INNER + SSM_GROUPS * SSM_STATE:]

    dt = _softplus(misc_ref[...] + dtb_ref[...])
    da = dt * a_ref[...]
    r = lax.broadcasted_iota(I32, (L, L), 0)
    cc = lax.broadcasted_iota(I32, (L, L), 1)
    tri = r >= cc
    a_cum = _dot_hi(jnp.where(tri, 1.0, 0.0), da)
    a_cum_t = a_cum.T
    expand = exp_ref[...]
    dt_full = _dot_hi(dt, expand)
    acum_full = _dot_hi(a_cum, expand)
    a_last = acum_full[L - 1:L, :]
    xdt = xs * dt_full
    xdt_b = xdt.astype(BF16)

    ys = []
    for g in range(SSM_GROUPS):
        bg = bm[:, g * SSM_STATE:(g + 1) * SSM_STATE].astype(BF16)
        cg = cm[:, g * SSM_STATE:(g + 1) * SSM_STATE].astype(BF16)
        cb = _dot_nt(cg, bg)
        hpg = SSM_HEADS // SSM_GROUPS
        cols = []
        for hh in range(hpg):
            h = g * hpg + hh
            diff = a_cum[:, MISC_DT + h:MISC_DT + h + 1] - a_cum_t[MISC_DT + h:MISC_DT + h + 1, :]
            seg = jnp.where(tri, jnp.exp(jnp.where(tri, diff, 0.0)), 0.0)
            sc = (cb * seg).astype(BF16)
            pair = (h // 2) * LANES
            yp = _dot(sc, xdt_b[:, pair:pair + LANES])
            cols.append(yp[:, (h % 2) * SSM_HEAD_DIM:(h % 2 + 1) * SSM_HEAD_DIM])
        y_diag = jnp.concatenate(cols, axis=1)
        gs = slice(g * half, (g + 1) * half)
        st = state_ref[:, gs]
        y_off = _dot(cg, st.astype(BF16)) * jnp.exp(acum_full[:, gs])
        ys.append(y_diag + y_off)
        xw = (xdt[:, gs] * jnp.exp(a_last[:, gs] - acum_full[:, gs])).astype(BF16)
        new = _dot(bm[:, g * SSM_STATE:(g + 1) * SSM_STATE].T.astype(BF16), xw)
        state_ref[:, gs] = st * jnp.exp(a_last[:, gs]) + new
    y = jnp.concatenate(ys, axis=1) + xs * dsk_ref[...]
    y = y * _silu(z_ref[...])
    o_ref[...] = _rms(y, nw_ref[...]).astype(BF16)


def ssd_mixer(proj, conv_w, conv_b, dtb_row, a_row, dsk_row, norm_w, expand):
    t = proj.shape[0]
    L = SSM_CHUNK
    full = lambda a: pl.BlockSpec(a.shape, lambda c: (0,) * a.ndim)
    return pl.pallas_call(
        _ssd_kernel,
        grid=(t // L,),
        in_specs=[pl.BlockSpec((L, SSM_INNER), lambda c: (c, COL_MZ // SSM_INNER)),
                  pl.BlockSpec((L, SSM_CONV_CH), lambda c: (c, COL_XBC // SSM_CONV_CH)),
                  pl.BlockSpec((L, LANES), lambda c: (c, COL_MISC // LANES)),
                  full(conv_w), full(conv_b), full(dtb_row), full(a_row), full(dsk_row), full(norm_w),
                  full(expand)],
        out_specs=pl.BlockSpec((L, SSM_INNER), lambda c: (c, 0)),
        out_shape=jax.ShapeDtypeStruct((t, SSM_INNER), BF16),
        scratch_shapes=[pltpu.VMEM((L + 8, SSM_CONV_CH), F32),
                        pltpu.VMEM((SSM_STATE, SSM_INNER), F32)],
        compiler_params=_cparams(("arbitrary",), VMEM_LIMIT),
        name="ssd_mixer",
    )(proj, proj, proj, conv_w, conv_b, dtb_row, a_row, dsk_row, norm_w, expand)


GATE_COL0 = sum(SPLIT_SIZES[:-1])
GATE_WIN = 3


def _merge_kernel(x_ref, *refs):
    w_refs = refs[:N_BRANCH * GATE_WIN]
    bg_ref, oa_ref, ob_ref, oc_ref, od_ref, wb_ref, o_ref, wg_ref = refs[N_BRANCH * GATE_WIN:]
    tn = o_ref.shape[1]
    off = GATE_COL0 % LANES

    @pl.when(pl.program_id(1) == 0)
    def _():
        for g in range(N_BRANCH):
            win = jnp.concatenate([w_refs[g * GATE_WIN + k][...] for k in range(GATE_WIN)], axis=1)
            wg_ref[g] = win[:, off:off + tn].astype(BF16)

    x = x_ref[...]
    acc = None
    for g, br in enumerate((oa_ref, ob_ref, oc_ref, od_ref)):
        logit = _dot(x, wg_ref[g]) + bg_ref[g:g + 1, :]
        gate = 1.0 / (1.0 + jnp.exp(-logit))
        term = gate * _dot(br[...], wb_ref[g])
        acc = term if acc is None else acc + term
    o_ref[...] = acc.astype(BF16)


def gated_merge(x_bf, w_in, bg, branches, wb, tm=1024):
    t = x_bf.shape[0]
    tn = (GATE_WIN - 1) * LANES
    br_spec = pl.BlockSpec((tm, BRANCH_W), lambda j, i: (i, 0))
    nj = D_MODEL // tn
    blk0 = GATE_COL0 // LANES

    def win_spec(g, k):
        return pl.BlockSpec((D_MODEL, LANES), lambda j, i: (0, blk0 + (g * D_MODEL + j * tn) // LANES + k))

    w_specs = [win_spec(g, k) for g in range(N_BRANCH) for k in range(GATE_WIN)]
    return pl.pallas_call(
        _merge_kernel,
        grid=(nj, t // tm),
        in_specs=[pl.BlockSpec((tm, D_MODEL), lambda j, i: (i, 0)),
                  *w_specs,
                  pl.BlockSpec((N_BRANCH, tn), lambda j, i: (0, j)),
                  br_spec, br_spec, br_spec, br_spec,
                  pl.BlockSpec((N_BRANCH, BRANCH_W, tn), lambda j, i: (0, 0, j))],
        out_specs=pl.BlockSpec((tm, tn), lambda j, i: (i, j)),
        out_shape=jax.ShapeDtypeStruct((t, D_MODEL), BF16),
        scratch_shapes=[pltpu.VMEM((N_BRANCH, D_MODEL, tn), BF16)],
        compiler_params=_cparams(("arbitrary", "arbitrary"), VMEM_LIMIT),
        name="gated_merge",
    )(x_bf, *([w_in] * (N_BRANCH * GATE_WIN)), bg, *branches, wb)


def _layer_norm(v, g, b):
    mu = jnp.mean(v, axis=-1, keepdims=True)
    d = v - mu
    var = jnp.mean(d * d, axis=-1, keepdims=True)
    return d * lax.rsqrt(var + LN_EPS) * g + b


def _outproj_kernel(m_ref, w_ref, x_ref, g_ref, b_ref, rw_ref, rb_ref, x1_ref, x1b_ref, idx_ref, gate_ref):
    mixed = _dot(m_ref[...], w_ref[...])
    x1 = _layer_norm(DN_ALPHA * x_ref[...] + mixed, g_ref[...], b_ref[...])
    x1_ref[...] = x1
    x1b = x1.astype(BF16)
    x1b_ref[...] = x1b
    logits = _dot(x1b, rw_ref[...]) + rb_ref[...]
    lane = lax.broadcasted_iota(I32, logits.shape, 1)
    g = logits
    idx_out = jnp.zeros(logits.shape, I32)
    val_out = jnp.zeros(logits.shape, F32)
    vals = []
    for kk in range(TOP_K):
        m = jnp.max(g, axis=1, keepdims=True)
        idx = jnp.min(jnp.where(g == m, lane, LANES), axis=1, keepdims=True)
        g = jnp.where(lane == idx, -jnp.inf, g)
        idx_out = jnp.where(lane == kk, idx, idx_out)
        vals.append(m)
    es = [jnp.exp(v - vals[0]) for v in vals]
    den = es[0] + es[1] + es[2] + es[3]
    for kk in range(TOP_K):
        val_out = jnp.where(lane == kk, es[kk] / den, val_out)
    idx_ref[...] = idx_out
    gate_ref[...] = val_out


def outproj_ln_router(merged, w_out, x, g, b, rw, rb, tm=256):
    t = x.shape[0]
    full = lambda a: pl.BlockSpec(a.shape, lambda i: (0,) * a.ndim)
    row = lambda w: pl.BlockSpec((tm, w), lambda i: (i, 0))
    return pl.pallas_call(
        _outproj_kernel,
        grid=(t // tm,),
        in_specs=[row(D_MODEL), full(w_out), row(D_MODEL), full(g), full(b), full(rw), full(rb)],
        out_specs=[row(D_MODEL), row(D_MODEL), row(LANES), row(LANES)],
        out_shape=[jax.ShapeDtypeStruct((t, D_MODEL), F32), jax.ShapeDtypeStruct((t, D_MODEL), BF16),
                   jax.ShapeDtypeStruct((t, LANES), I32), jax.ShapeDtypeStruct((t, LANES), F32)],
        compiler_params=_cparams(("arbitrary",), VMEM_LIMIT),
        name="outproj_ln_router",
    )(merged, w_out, x, g, b, rw, rb)


def _start_row_gather(idx_ref, src_hbm, dst, sem, rows):
    for r in range(rows):
        pltpu.make_async_copy(src_hbm.at[pl.ds(idx_ref[0, r], 1)], dst.at[pl.ds(r, 1)], sem).start(priority=r % 2)


def _wait_row_gather(src_hbm, dst, sem, rows):
    pltpu.make_async_copy(src_hbm.at[pl.ds(0, rows)], dst, sem).wait()


def _expert_kernel(be_ref, nu_ref, tok_ref, tokn_ref, x_hbm, wgu_ref, bgu_ref, wd_ref, bd_ref, o_ref, xbuf, sem):
    blk = pl.program_id(0)
    n_used = nu_ref[0]
    slot = lax.rem(blk, 2)

    @pl.when(jnp.logical_and(blk == 0, n_used > 0))
    def _():
        _start_row_gather(tok_ref, x_hbm, xbuf.at[0], sem.at[0], MOE_ROWS)

    @pl.when(blk + 1 < n_used)
    def _():
        _start_row_gather(tokn_ref, x_hbm, xbuf.at[1 - slot], sem.at[1 - slot], MOE_ROWS)

    @pl.when(blk < n_used)
    def _():
        _wait_row_gather(x_hbm, xbuf.at[slot], sem.at[slot], MOE_ROWS)
        gu = _dot(xbuf[slot].astype(BF16), wgu_ref[0].astype(BF16)) + bgu_ref[0]
        glu = jnp.minimum(gu[:, :D_FF], SWIGLU_LIMIT)
        lin = jnp.clip(gu[:, D_FF:], -SWIGLU_LIMIT, SWIGLU_LIMIT)
        act = glu / (1.0 + jnp.exp(-SWIGLU_ALPHA * glu)) * (lin + 1.0)
        o_ref[...] = _dot(act.astype(BF16), wd_ref[0].astype(BF16)) + bd_ref[0]

    @pl.when(blk >= nu_ref[0])
    def _():
        o_ref[...] = jnp.zeros(o_ref.shape, F32)


def expert_ffn(block_e, n_used, row_tok, x, wgu, bgu, wd, bd):
    n_blocks = row_tok.shape[0]
    n_rows = n_blocks * MOE_ROWS
    grid_spec = pltpu.PrefetchScalarGridSpec(
        num_scalar_prefetch=2,
        grid=(n_blocks,),
        in_specs=[pl.BlockSpec((None, 1, MOE_ROWS), lambda b, be, nu: (b, 0, 0), memory_space=pltpu.SMEM),
                  pl.BlockSpec((None, 1, MOE_ROWS), lambda b, be, nu: (jnp.minimum(b + 1, n_blocks - 1), 0, 0),
                               memory_space=pltpu.SMEM),
                  pl.BlockSpec(memory_space=pl.ANY),
                  pl.BlockSpec((1, D_MODEL, 2 * D_FF), lambda b, be, nu: (be[b], 0, 0)),
                  pl.BlockSpec((1, 1, 2 * D_FF), lambda b, be, nu: (be[b], 0, 0)),
                  pl.BlockSpec((1, D_FF, D_MODEL), lambda b, be, nu: (be[b], 0, 0)),
                  pl.BlockSpec((1, 1, D_MODEL), lambda b, be, nu: (be[b], 0, 0))],
        out_specs=pl.BlockSpec((MOE_ROWS, D_MODEL), lambda b, be, nu: (b, 0)),
        scratch_shapes=[pltpu.VMEM((2, MOE_ROWS, D_MODEL), F32), pltpu.SemaphoreType.DMA((2,))],
    )
    return pl.pallas_call(
        _expert_kernel,
        grid_spec=grid_spec,
        out_shape=jax.ShapeDtypeStruct((n_rows, D_MODEL), F32),
        compiler_params=_cparams(("arbitrary",), VMEM_LIMIT),
        name="expert_ffn",
    )(block_e, n_used, row_tok, row_tok, x, wgu, bgu, wd, bd)


COMBINE_TM = 128


def _combine_kernel(pos_ref, posn_ref, out_hbm, gate_ref, x_ref, g_ref, b_ref, o_ref, ob_ref, rbuf, sem):
    i = pl.program_id(0)
    tm = x_ref.shape[0]
    rows = TOP_K * tm
    slot = lax.rem(i, 2)

    @pl.when(i == 0)
    def _():
        _start_row_gather(pos_ref, out_hbm, rbuf.at[0], sem.at[0], rows)

    @pl.when(i + 1 < pl.num_programs(0))
    def _():
        _start_row_gather(posn_ref, out_hbm, rbuf.at[1 - slot], sem.at[1 - slot], rows)

    _wait_row_gather(out_hbm, rbuf.at[slot], sem.at[slot], rows)
    gate = gate_ref[...]
    y = jnp.zeros(x_ref.shape, F32)
    for kk in range(TOP_K):
        y = y + gate[:, kk:kk + 1] * rbuf[slot, kk * tm:(kk + 1) * tm, :]
    x2 = _layer_norm(DN_ALPHA * x_ref[...] + y, g_ref[...], b_ref[...])
    o_ref[...] = x2
    ob_ref[...] = x2.astype(BF16)


def combine_ln(pos_km, out, gates, x1, g, b):
    t = x1.shape[0]
    tm = COMBINE_TM
    n_tiles = t // tm
    full = lambda a: pl.BlockSpec(a.shape, lambda i: (0,) * a.ndim)
    row = lambda w: pl.BlockSpec((tm, w), lambda i: (i, 0))
    return pl.pallas_call(
        _combine_kernel,
        grid=(n_tiles,),
        in_specs=[pl.BlockSpec((None, 1, TOP_K * tm), lambda i: (i, 0, 0), memory_space=pltpu.SMEM),
                  pl.BlockSpec((None, 1, TOP_K * tm), lambda i: (jnp.minimum(i + 1, n_tiles - 1), 0, 0),
                               memory_space=pltpu.SMEM),
                  pl.BlockSpec(memory_space=pl.ANY), row(LANES), row(D_MODEL), full(g), full(b)],
        out_specs=[row(D_MODEL), row(D_MODEL)],
        out_shape=[jax.ShapeDtypeStruct((t, D_MODEL), F32), jax.ShapeDtypeStruct((t, D_MODEL), BF16)],
        scratch_shapes=[pltpu.VMEM((2, TOP_K * tm, D_MODEL), F32), pltpu.SemaphoreType.DMA((2,))],
        compiler_params=_cparams(("arbitrary",), VMEM_LIMIT),
        name="combine_ln",
    )(pos_km, pos_km, out, gates, x1, g, b)


def _rope_tables(positions):
    pos = positions.reshape(-1).astype(F32)

    def cs(dim, theta):
        inv = theta ** (-jnp.arange(0, dim, 2, dtype=F32) / dim)
        ang = pos[:, None] * inv
        return jnp.cos(ang), jnp.sin(ang)

    t = pos.shape[0]
    cos_p, sin_p = cs(ROT_DIM, ROPE_THETA)
    half = ROT_DIM // 2
    one = jnp.ones((t, HEAD_DIM - ROT_DIM), F32)
    zero = jnp.zeros((t, HEAD_DIM - ROT_DIM), F32)
    zh = jnp.zeros((t, half), F32)
    c64 = jnp.concatenate([cos_p, cos_p, one], axis=1)
    s1_64 = jnp.concatenate([-sin_p, zh, zero], axis=1)
    s2_64 = jnp.concatenate([zh, sin_p, zero], axis=1)
    tabs_p = tuple(jnp.tile(a, (1, 2)) for a in (c64, s1_64, s2_64))

    cos_m, sin_m = cs(MLA_ROPE, MLA_THETA)
    hm = MLA_ROPE // 2
    pre1 = jnp.ones((t, MLA_NOPE), F32)
    pre0 = jnp.zeros((t, MLA_NOPE), F32)
    post1 = jnp.ones((t, LANES - MLA_NOPE - MLA_ROPE), F32)
    post0 = jnp.zeros((t, LANES - MLA_NOPE - MLA_ROPE), F32)
    zm = jnp.zeros((t, hm), F32)
    c_m = jnp.concatenate([pre1, cos_m, cos_m, post1], axis=1)
    s1_m = jnp.concatenate([pre0, -sin_m, zm, post0], axis=1)
    s2_m = jnp.concatenate([pre0, zm, sin_m, post0], axis=1)
    return tabs_p, (c_m, s1_m, s2_m)


def _prep_in_weights(w_in):
    pts = np.cumsum((0,) + SPLIT_SIZES)
    seg = lambda k: w_in[:, pts[k]:pts[k + 1]]
    (a_q, a_k, a_v, m_z, m_xbc, m_dt, c_q, c_k, c_v, c_qi, c_ki, c_wi, d_cq, d_ckv, d_kpe) = [seg(k) for k in range(15)]
    scale = HEAD_DIM ** -0.5 * LOG2E
    zeros = lambda n: jnp.zeros((D_MODEL, n), w_in.dtype)
    cols = [a_q * scale, a_k, a_v, c_q * scale, c_k, c_v, c_qi, c_ki, c_ki, c_ki, c_ki,
            m_z, m_xbc, d_ckv, d_cq, d_kpe, m_dt, c_wi * IDX_SCALE]
    small = jnp.concatenate(cols, axis=1)
    small = jnp.concatenate([small, zeros(N_SMALL - small.shape[1])], axis=1).astype(BF16)
    return small


def _prep_mla_weights(w_uq, w_ukv):
    dq = MLA_NOPE + MLA_ROPE
    wq = w_uq.reshape(MLA_Q_RANK, N_HEADS, dq) * (dq ** -0.5 * LOG2E)
    wq = jnp.pad(wq, ((0, 0), (0, 0), (0, LANES - dq))).reshape(MLA_Q_RANK, N_HEADS * LANES)
    wkv = w_ukv.reshape(MLA_KV_RANK, N_HEADS, MLA_NOPE + MLA_V)
    wk = jnp.pad(wkv[:, :, :MLA_NOPE], ((0, 0), (0, 0), (0, LANES - MLA_NOPE))).reshape(MLA_KV_RANK, N_HEADS * LANES)
    wv = wkv[:, :, MLA_NOPE:].reshape(MLA_KV_RANK, N_HEADS * MLA_V)
    return wq.astype(BF16), wk.astype(BF16), wv.astype(BF16)


def _misc_row(v):
    return jnp.zeros((1, LANES), F32).at[0, MISC_DT:MISC_DT + SSM_HEADS].set(v.astype(F32))


def _route(top_idx, t):
    tk = t * TOP_K
    flat_e = top_idx.reshape(-1)
    onehot = (flat_e[:, None] == jnp.arange(N_EXPERTS)[None, :]).astype(I32)
    seen = jnp.cumsum(onehot, axis=0)
    counts = seen[-1]
    padded = ((counts + MOE_ROWS - 1) // MOE_ROWS) * MOE_ROWS
    pad_end = jnp.cumsum(padded)
    pad_start = pad_end - padded
    rank = jnp.sum(onehot * seen, axis=1) - 1
    pos = (pad_start[flat_e] + rank).astype(I32)
    n_blocks = tk // MOE_ROWS + N_EXPERTS
    n_rows = n_blocks * MOE_ROWS
    row_tok = jnp.zeros((n_rows,), I32).at[pos].set(jnp.arange(tk, dtype=I32) // TOP_K)
    block_start = jnp.arange(n_blocks) * MOE_ROWS
    block_e = jnp.minimum(jnp.sum(pad_end[None, :] <= block_start[:, None], axis=1), N_EXPERTS - 1).astype(I32)
    n_used = (pad_end[-1] // MOE_ROWS).astype(I32).reshape(1)
    return row_tok, pos, block_e, n_used


def _layer(x, x_bf, tabs_p, tabs_m, flags, expand, p):
    t = x.shape[0]
    w_small = _prep_in_weights(p["w_in"])
    proj = matmul_bf16(x_bf, w_small)
    cast, cast_t = rope_cast(proj, flags, tabs_p)

    o_a = moba_attention(cast, cast_t)
    o_b = ssd_mixer(proj, p["conv_w"], p["conv_b"].reshape(1, -1), _misc_row(p["dt_bias"]),
                    _misc_row(-jnp.exp(p["a_log"].astype(F32))),
                    jnp.repeat(p["d_skip"].astype(F32), SSM_HEAD_DIM).reshape(1, -1),
                    p["ssm_norm_w"].reshape(1, -1), expand)
    wq, wk, wv = _prep_mla_weights(p["w_uq"], p["w_ukv"])
    qt_m, k_m, vt_m, misc_t = mla_prep(proj, p["q_norm_w"].reshape(1, -1), p["kv_norm_w"].reshape(1, -1), wq, wk, wv,
                                       tabs_m)
    o_c = dsa_attention(cast, cast_t, dsa_select(cast, cast_t, misc_t))
    o_d = mla_attention(qt_m, k_m, vt_m)

    merged = gated_merge(x_bf, p["w_in"], p["b_gate"], (o_a, o_b, o_c, o_d), p["w_branch"].astype(BF16))
    rw = jnp.pad(p["router_w"], ((0, 0), (0, LANES - N_EXPERTS))).astype(BF16)
    rb = jnp.concatenate([p["router_b"].astype(F32), jnp.full((LANES - N_EXPERTS,), -jnp.inf, F32)]).reshape(1, -1)
    x1, x1_bf, top_idx, gates = outproj_ln_router(merged, p["w_out"].astype(BF16), x, p["ln1_g"].reshape(1, -1),
                                                  p["ln1_b"].reshape(1, -1), rw, rb)

    row_tok, pos, block_e, n_used = _route(top_idx[:, :TOP_K], t)
    out = expert_ffn(block_e, n_used, row_tok.reshape(-1, 1, MOE_ROWS), x1, p["w_gate_up"],
                     p["b_gate_up"][:, None, :], p["w_down"], p["b_down"][:, None, :])
    pos_km = pos.reshape(t // COMBINE_TM, COMBINE_TM, TOP_K).transpose(0, 2, 1).reshape(-1, 1, TOP_K * COMBINE_TM)
    return combine_ln(pos_km, out, gates, x1, p["ln2_g"].reshape(1, -1), p["ln2_b"].reshape(1, -1))


def kernel(x, positions, w_in, b_gate, conv_w, conv_b, dt_bias, a_log, d_skip, ssm_norm_w, q_norm_w, w_uq,
           kv_norm_w, w_ukv, w_branch, w_out, ln1_g, ln1_b, router_w, router_b, w_gate_up, b_gate_up, w_down,
           b_down, ln2_g, ln2_b):
    b, s, d = x.shape
    assert b == 1 and d == D_MODEL
    names = ("w_in", "b_gate", "conv_w", "conv_b", "dt_bias", "a_log", "d_skip", "ssm_norm_w", "q_norm_w", "w_uq",
             "kv_norm_w", "w_ukv", "w_branch", "w_out", "ln1_g", "ln1_b", "router_w", "router_b", "w_gate_up",
             "b_gate_up", "w_down", "b_down", "ln2_g", "ln2_b")
    stacked = (w_in, b_gate, conv_w, conv_b, dt_bias, a_log, d_skip, ssm_norm_w, q_norm_w, w_uq, kv_norm_w, w_ukv,
               w_branch, w_out, ln1_g, ln1_b, router_w, router_b, w_gate_up, b_gate_up, w_down, b_down, ln2_g, ln2_b)
    tabs_p, tabs_m = _rope_tables(positions)
    flags = jnp.array([1, 1, 0, 1, 1, 0, 1], I32)
    head_of_lane = jnp.arange(SSM_INNER) // SSM_HEAD_DIM
    expand = (jnp.arange(LANES)[:, None] == (MISC_DT + head_of_lane)[None, :]).astype(F32)

    xt = x.reshape(s, d)
    xt_bf = xt.astype(BF16)
    for l in range(DEPTH):
        p = {n: a[l] for n, a in zip(names, stacked)}
        xt, xt_bf = _layer(xt, xt_bf, tabs_p, tabs_m, flags, expand, p)
    return xt.reshape(b, s, d)
```

```python
import functools
import math

import jax
import jax.numpy as jnp
import numpy as np
from jax import lax
from jax.experimental import pallas as pl
from jax.experimental.pallas import tpu as pltpu

F32 = jnp.float32
BF16 = jnp.bfloat16
I32 = jnp.int32
I16 = jnp.int16

D_MODEL = 2048
DEPTH = 4
HEAD_DIM = 64
ROT_DIM = HEAD_DIM // 4
ROPE_THETA = 500000.0
N_HEADS = 8
ATT_W = N_HEADS * HEAD_DIM
MOBA_BLOCK = 256
MOBA_TOPK = 3
SSM_HEADS = 8
SSM_HEAD_DIM = 64
SSM_INNER = 512
SSM_STATE = 128
SSM_GROUPS = 2
SSM_CONV = 4
SSM_CONV_CH = 1024
SSM_CHUNK = 256
IDX_HEADS = 4
IDX_DIM = 64
DSA_TOPK = 256
IDX_SCALE = (IDX_HEADS * IDX_DIM) ** -0.5
MLA_Q_RANK = 384
MLA_KV_RANK = 256
MLA_NOPE = 64
MLA_ROPE = 32
MLA_V = 64
MLA_THETA = 10000.0
N_BRANCH = 4
BRANCH_W = 512
SPLIT_SIZES = (512, 512, 512, 512, 1024, 8, 512, 512, 512, 256, 64, 4, 384, 256, 32, N_BRANCH * D_MODEL)
N_EXPERTS = 32
TOP_K = 4
D_FF = 768
SWIGLU_ALPHA = 1.702
SWIGLU_LIMIT = 7.0
DN_ALPHA = (2 * DEPTH) ** 0.25
LN_EPS = 1e-5

LANES = 128
NEG = -1e30
VMEM_LIMIT = 56 * 1024 * 1024

COL_AQ, COL_AK, COL_AV = 0, 512, 1024
COL_CQ, COL_CK, COL_CV = 1536, 2048, 2560
COL_CQI, COL_KI4 = 3072, 3328
COL_MZ = 3584
COL_XBC = 4096
COL_CKV = 5120
COL_CQL = 5376
COL_MISC = 5760
N_SMALL = 6144
N_CAST = 3584
MISC_DT = 32
MISC_WI = 40
MOE_ROWS = 256


def _cparams(sem, vmem=None):
    return pltpu.CompilerParams(dimension_semantics=sem, vmem_limit_bytes=vmem)


def _dot(a, b):
    return jnp.dot(a, b, preferred_element_type=F32)


def _dot_nt(a, b):
    return lax.dot_general(a, b, (((1,), (1,)), ((), ())), preferred_element_type=F32)


def _dot_hi(a, b):
    return jnp.dot(a, b, preferred_element_type=F32, precision=lax.Precision.HIGHEST)


def _mm_kernel(x_ref, w_ref, o_ref):
    o_ref[...] = _dot(x_ref[...], w_ref[...])


def matmul_bf16(x, w, tm=512, tn=1024):
    m, k = x.shape
    n = w.shape[1]
    return pl.pallas_call(
        _mm_kernel,
        grid=(n // tn, m // tm),
        in_specs=[pl.BlockSpec((tm, k), lambda j, i: (i, 0)),
                  pl.BlockSpec((k, tn), lambda j, i: (0, j))],
        out_specs=pl.BlockSpec((tm, tn), lambda j, i: (i, j)),
        out_shape=jax.ShapeDtypeStruct((m, n), F32),
        compiler_params=_cparams(("arbitrary", "arbitrary"), VMEM_LIMIT),
        name="in_proj",
    )(x, w)


def _rot_apply(x, c, s1, s2, shift):
    return x * c + pltpu.roll(x, LANES - shift, 1) * s1 + pltpu.roll(x, shift, 1) * s2


def _rope_cast_kernel(flag_ref, x_ref, c_ref, s1_ref, s2_ref, o_ref, ot_ref):
    j = pl.program_id(0)
    width = x_ref.shape[1]

    def emit(g, y):
        sl = slice(g * LANES, (g + 1) * LANES)
        o_ref[:, sl] = y.astype(BF16)
        ot_ref[sl, :] = y.T.astype(BF16)

    @pl.when(flag_ref[j] == 1)
    def _():
        c, s1, s2 = c_ref[...], s1_ref[...], s2_ref[...]
        for g in range(width // LANES):
            emit(g, _rot_apply(x_ref[:, g * LANES:(g + 1) * LANES], c, s1, s2, ROT_DIM // 2))

    @pl.when(flag_ref[j] == 0)
    def _():
        for g in range(width // LANES):
            emit(g, x_ref[:, g * LANES:(g + 1) * LANES])


def rope_cast(proj, flags, tabs, tw=512):
    t = proj.shape[0]
    tm = ATT_T
    grid_spec = pltpu.PrefetchScalarGridSpec(
        num_scalar_prefetch=1,
        grid=(N_CAST // tw, t // tm),
        in_specs=[pl.BlockSpec((tm, tw), lambda j, i, f: (i, j)),
                  pl.BlockSpec((tm, LANES), lambda j, i, f: (i, 0)),
                  pl.BlockSpec((tm, LANES), lambda j, i, f: (i, 0)),
                  pl.BlockSpec((tm, LANES), lambda j, i, f: (i, 0))],
        out_specs=[pl.BlockSpec((tm, tw), lambda j, i, f: (i, j)),
                   pl.BlockSpec((None, tw, tm), lambda j, i, f: (i, j, 0))],
    )
    return pl.pallas_call(
        _rope_cast_kernel,
        grid_spec=grid_spec,
        out_shape=[jax.ShapeDtypeStruct((t, N_CAST), BF16),
                   jax.ShapeDtypeStruct((t // tm, N_CAST, tm), BF16)],
        compiler_params=_cparams(("arbitrary", "arbitrary")),
        name="rope_cast",
    )(flags, proj, *tabs)


ATT_T = 256
ATT_HEADS = 4


LOG2E = 1.4426950408889634
ACC_ROWS = HEAD_DIM + 16


def _softmax_step(s, smax, vt, state, keep=None):
    m, acc = state
    if keep is not None:
        smax = jnp.where(keep, smax, NEG)
    m_new = jnp.maximum(m, smax)
    shift = m_new if keep is None else jnp.where(keep, m_new, -NEG)
    p = jnp.exp2(s - shift).astype(BF16)
    vt_ext = jnp.concatenate([vt, jnp.ones((ACC_ROWS - HEAD_DIM, vt.shape[1]), vt.dtype)], axis=0)
    acc_new = jnp.exp2(m - m_new) * acc + _dot(vt_ext, p)
    return m_new, acc_new


def _init_state(tq):
    return (jnp.full((1, tq), NEG, F32), jnp.zeros((ACC_ROWS, tq), F32))


def _causal_t(tk, tq):
    return lax.broadcasted_iota(I32, (tk, tq), 0) <= lax.broadcasted_iota(I32, (tk, tq), 1)


def _head_rows(qt):
    row = lax.broadcasted_iota(I32, (LANES, 1), 0)
    out = []
    for p in range(qt.shape[0] // LANES):
        blk = qt[p * LANES:(p + 1) * LANES, :]
        zero = jnp.zeros_like(blk)
        out += [jnp.where(row < HEAD_DIM, blk, zero), jnp.where(row >= HEAD_DIM, blk, zero)]
    return out


def _pair_cols(h):
    return slice((h // 2) * LANES, (h // 2 + 1) * LANES)


def _attention_tile(i, score_fn, vt_fn, s_ref, tq, last_mask, keep_fn=None):
    nh = s_ref.shape[1]

    def park(scores, buf):
        for h in range(nh):
            s_ref[buf, h] = scores[h]
        return [jnp.max(sc, axis=0, keepdims=True) for sc in scores]

    def step(n, carry, buf, nxt_buf):
        nxt = score_fn(n + 1)
        states = []
        for h in range(nh):
            smax, m, acc = carry[3 * h:3 * h + 3]
            keep = None if keep_fn is None else keep_fn(n, h)
            states.append(_softmax_step(s_ref[buf, h], smax, vt_fn(n, h), (m, acc), keep))
        smax_next = park(nxt, nxt_buf)
        out = []
        for h in range(nh):
            out.extend((smax_next[h],) + tuple(states[h]))
        return tuple(out)

    def pair(mm, carry):
        return step(2 * mm + 1, step(2 * mm, carry, 0, 1), 1, 0)

    init = []
    for smax in park(score_fn(0), 0):
        init.extend((smax,) + _init_state(tq))
    carry = lax.fori_loop(0, i // 2, pair, tuple(init))
    carry = lax.cond(i % 2 == 1, lambda c: step(i - 1, c, 0, 0), lambda c: c, carry)
    fin = []
    for h in range(nh):
        smax, m, acc = carry[3 * h:3 * h + 3]
        s = s_ref[0, h]
        if last_mask is not None:
            s = jnp.where(last_mask, s, NEG)
            smax = jnp.max(s, axis=0, keepdims=True)
        _, acc = _softmax_step(s, smax, vt_fn(i, h), (m, acc))
        fin.append(acc[0:HEAD_DIM] / acc[HEAD_DIM:HEAD_DIM + 1])
    return jnp.concatenate(fin, axis=0).T


def _top_n_rows(g, valid, n):
    row = lax.broadcasted_iota(I32, g.shape, 0)
    g = jnp.where(valid, g, -jnp.inf)
    sel = jnp.zeros(g.shape, jnp.bool_)
    for _ in range(n):
        m = jnp.max(g, axis=0, keepdims=True)
        idx = jnp.min(jnp.where(g == m, row, g.shape[0]), axis=0, keepdims=True)
        pick = row == idx
        sel = jnp.logical_or(sel, pick)
        g = jnp.where(pick, -jnp.inf, g)
    return jnp.logical_and(sel, valid)


def _moba_kernel(qt_ref, k_ref, vt_ref, o_ref, kmean_ref, sel_ref, s_ref):
    i = pl.program_id(1)
    tq = qt_ref.shape[1]
    nb = k_ref.shape[0] // MOBA_BLOCK

    @pl.when(i == 0)
    def _():
        kmean_ref[...] = jnp.zeros(kmean_ref.shape, F32)

        def mean_body(n, c):
            kb = k_ref[pl.ds(pl.multiple_of(n * MOBA_BLOCK, MOBA_BLOCK), MOBA_BLOCK), :].astype(F32)
            kmean_ref[pl.ds(n, 1), :] = jnp.sum(kb, axis=0, keepdims=True) * (1.0 / MOBA_BLOCK)
            return c

        lax.fori_loop(0, nb, mean_body, 0)

    nh = s_ref.shape[1]
    qh = _head_rows(qt_ref[...])
    kmean = kmean_ref[...].astype(BF16)
    row_n = lax.broadcasted_iota(I32, (kmean.shape[0], tq), 0)
    for h in range(nh):
        gate = _dot(kmean[:, _pair_cols(h)], qh[h])
        sel_ref[h] = jnp.where(_top_n_rows(gate, row_n < i, MOBA_TOPK), 1.0, 0.0)

    def score_fn(n):
        kb = k_ref[pl.ds(pl.multiple_of(n * MOBA_BLOCK, MOBA_BLOCK), MOBA_BLOCK), :]
        return [_dot(kb[:, _pair_cols(h)], qh[h]) for h in range(nh)]

    def vt_fn(n, h):
        return vt_ref[n, h * HEAD_DIM:(h + 1) * HEAD_DIM, :]

    def keep_fn(n, h):
        return sel_ref[h, pl.ds(n, 1), :] > 0.5

    o_ref[...] = _attention_tile(i, score_fn, vt_fn, s_ref, tq, _causal_t(MOBA_BLOCK, tq), keep_fn).astype(BF16)


def moba_attention(cast, cast_t):
    t = cast.shape[0]
    tq = MOBA_BLOCK
    w = ATT_HEADS * HEAD_DIM
    nbp = max(8, t // MOBA_BLOCK)
    return pl.pallas_call(
        _moba_kernel,
        grid=(N_HEADS // ATT_HEADS, t // tq),
        in_specs=[pl.BlockSpec((None, w, tq), lambda g, i: (i, COL_AQ // w + g, 0)),
                  pl.BlockSpec((t, w), lambda g, i: (0, COL_AK // w + g)),
                  pl.BlockSpec((t // tq, w, tq), lambda g, i: (0, COL_AV // w + g, 0))],
        out_specs=pl.BlockSpec((tq, w), lambda g, i: (i, g)),
        out_shape=jax.ShapeDtypeStruct((t, ATT_W), BF16),
        scratch_shapes=[pltpu.VMEM((nbp, w), F32), pltpu.VMEM((ATT_HEADS, nbp, tq), F32),
                        pltpu.VMEM((2, ATT_HEADS, MOBA_BLOCK, tq), F32)],
        compiler_params=_cparams(("arbitrary", "arbitrary"), VMEM_LIMIT),
        name="moba_attention",
    )(cast_t, cast, cast_t)


HALF16 = 1 << 15


def _float_key(x):
    b = lax.bitcast_convert_type(x, I32)
    return jnp.where(b < 0, b ^ jnp.int32(0x7FFFFFFF), b)


def _dsa_select_kernel(qit_ref, ki_ref, misct_ref, bias_ref, key_ref, hi_ref, lo_ref):
    i = pl.program_id(0)
    tq = qit_ref.shape[1]
    tk = ATT_T
    n_chunks = ki_ref.shape[0] // tk
    n_live = i + 1

    qit = qit_ref[...]
    row = lax.broadcasted_iota(I32, (IDX_HEADS * IDX_DIM, 1), 0)
    zero = jnp.zeros_like(qit)
    qh = [jnp.where((row >= h * IDX_DIM) & (row < (h + 1) * IDX_DIM), qit, zero) for h in range(IDX_HEADS)]
    wrow = [misct_ref[MISC_WI + h:MISC_WI + h + 1, :] for h in range(IDX_HEADS)]
    visible = _causal_t(tk, tq)

    def score_body(j, c):
        kic = ki_ref[pl.ds(pl.multiple_of(j * tk, tk), tk), :]
        acc = jnp.zeros((tk, tq), F32)
        for h in range(IDX_HEADS):
            acc = acc + wrow[h] * jnp.maximum(_dot(kic, qh[h]), 0.0)
        acc = jnp.where(jnp.logical_or(j < i, visible), acc, -jnp.inf)
        key = _float_key(acc)
        key_ref[j] = key
        hi_ref[j] = (key >> 16).astype(I16)
        lo_ref[j] = ((key & 0xFFFF) - HALF16).astype(I16)
        return c

    lax.fori_loop(0, n_live, score_body, 0)

    def count_ge(cand):
        def body(j, cnt):
            for r in range(tk // 8):
                cnt = cnt + jnp.where(key_ref[j, r * 8:(r + 1) * 8, :] >= cand, 1.0, 0.0)
            return cnt
        cnt = lax.fori_loop(0, n_live, body, jnp.zeros((8, tq), F32))
        return jnp.sum(cnt, axis=0, keepdims=True)

    def count16(ref, cand):
        c16 = cand.astype(I16)

        def body(j, cnt):
            for r in range(tk // 16):
                cnt = cnt + jnp.where(ref[j, r * 16:(r + 1) * 16, :] >= c16, jnp.int16(1), jnp.int16(0))
            return cnt
        cnt = lax.fori_loop(0, n_live, body, jnp.zeros((16, tq), I16))
        return jnp.sum(cnt.astype(I32), axis=0, keepdims=True).astype(F32)

    def digit_select(ref, want):
        base = jnp.where(count16(ref, jnp.zeros((1, tq), I32)) >= want, 0, -HALF16)

        def bit_body(b, d):
            cand = d | (jnp.int32(1) << (14 - b))
            return jnp.where(count16(ref, cand) >= want, cand, d)

        return lax.fori_loop(0, 15, bit_body, base)

    topk = float(DSA_TOPK)
    t_hi = digit_select(hi_ref, topk)
    want_lo = topk - count16(hi_ref, t_hi + 1)
    t_hi16 = t_hi.astype(I16)

    def mask_body(j, c):
        lo_ref[j] = jnp.where(hi_ref[j] == t_hi16, lo_ref[j], jnp.int16(-HALF16))
        return c

    lax.fori_loop(0, n_live, mask_body, 0)
    t_lo = digit_select(lo_ref, want_lo)
    thr = (t_hi << 16) | (t_lo + HALF16)

    need = topk - count_ge(thr + 1)
    lower = jnp.where(lax.broadcasted_iota(I32, (tk, tk), 0) >= lax.broadcasted_iota(I32, (tk, tk), 1),
                      1.0, 0.0).astype(BF16)

    def bias_body(j, seen):
        kk = key_ref[j]
        eq = kk == thr
        rank = _dot(lower, jnp.where(eq, 1.0, 0.0).astype(BF16)) + seen
        keep = ((kk > thr) | (eq & (rank <= need))) & jnp.logical_or(j < i, visible)
        bias_ref[pl.ds(pl.multiple_of(j * tk, tk), tk), :] = jnp.where(keep, 0.0, NEG).astype(BF16)
        return rank[tk - 1:tk, :]

    lax.fori_loop(0, n_live, bias_body, jnp.zeros((1, tq), F32))

    def fill_body(j, c):
        bias_ref[pl.ds(pl.multiple_of(j * tk, tk), tk), :] = jnp.full((tk, tq), NEG, BF16)
        return c

    lax.fori_loop(n_live, n_chunks, fill_body, 0)


def dsa_select(cast, cast_t, misc_t):
    t = cast.shape[0]
    tq = ATT_T
    w = IDX_HEADS * IDX_DIM
    return pl.pallas_call(
        _dsa_select_kernel,
        grid=(t // tq,),
        in_specs=[pl.BlockSpec((None, w, tq), lambda i: (i, COL_CQI // w, 0)),
                  pl.BlockSpec((t, w), lambda i: (0, COL_KI4 // w)),
                  pl.BlockSpec((None, LANES, tq), lambda i: (i, 0, 0))],
        out_specs=pl.BlockSpec((None, t, tq), lambda i: (i, 0, 0)),
        out_shape=jax.ShapeDtypeStruct((t // tq, t, tq), BF16),
        scratch_shapes=[pltpu.VMEM((t // ATT_T, ATT_T, tq), I32), pltpu.VMEM((t // ATT_T, ATT_T, tq), I16),
                        pltpu.VMEM((t // ATT_T, ATT_T, tq), I16)],
        compiler_params=_cparams(("arbitrary",), VMEM_LIMIT),
        name="dsa_select",
    )(cast_t, cast, misc_t)


def _dsa_attn_kernel(qt_ref, k_ref, vt_ref, bias_ref, o_ref, s_ref):
    i = pl.program_id(1)
    tq = qt_ref.shape[1]
    tk = ATT_T
    nh = s_ref.shape[1]
    qh = _head_rows(qt_ref[...])

    def score_fn(n):
        off = pl.multiple_of(n * tk, tk)
        kb = k_ref[pl.ds(off, tk), :]
        bias = bias_ref[pl.ds(off, tk), :].astype(F32)
        return [_dot(kb[:, _pair_cols(h)], qh[h]) + bias for h in range(nh)]

    def vt_fn(n, h):
        return vt_ref[n, h * HEAD_DIM:(h + 1) * HEAD_DIM, :]

    o_ref[...] = _attention_tile(i, score_fn, vt_fn, s_ref, tq, None).astype(BF16)


def dsa_attention(cast, cast_t, bias):
    t = cast.shape[0]
    tq = ATT_T
    w = ATT_HEADS * HEAD_DIM
    return pl.pallas_call(
        _dsa_attn_kernel,
        grid=(N_HEADS // ATT_HEADS, t // tq),
        in_specs=[pl.BlockSpec((None, w, tq), lambda g, i: (i, COL_CQ // w + g, 0)),
                  pl.BlockSpec((t, w), lambda g, i: (0, COL_CK // w + g)),
                  pl.BlockSpec((t // tq, w, tq), lambda g, i: (0, COL_CV // w + g, 0)),
                  pl.BlockSpec((None, t, tq), lambda g, i: (i, 0, 0))],
        out_specs=pl.BlockSpec((tq, w), lambda g, i: (i, g)),
        out_shape=jax.ShapeDtypeStruct((t, ATT_W), BF16),
        scratch_shapes=[pltpu.VMEM((2, ATT_HEADS, ATT_T, tq), F32)],
        compiler_params=_cparams(("arbitrary", "arbitrary"), VMEM_LIMIT),
        name="dsa_attention",
    )(cast_t, cast, cast_t, bias)


def _rms(x, w):
    return x * lax.rsqrt(jnp.mean(x * x, axis=-1, keepdims=True) + LN_EPS) * w


def _mla_prep_kernel(cq_ref, ckv_ref, misc_ref, qn_ref, kvn_ref, wq_ref, wk_ref, wv_ref,
                     c_ref, s1_ref, s2_ref, qt_ref, k_ref, vt_ref, misct_ref):
    c, s1, s2 = c_ref[...], s1_ref[...], s2_ref[...]
    cq = _rms(cq_ref[...], qn_ref[...]).astype(BF16)
    ckv = _rms(ckv_ref[...], kvn_ref[...]).astype(BF16)
    misc = misc_ref[...]
    misct_ref[...] = misc.T
    lane = lax.broadcasted_iota(I32, (1, LANES), 1)
    kpe = pltpu.roll(jnp.where(lane < MLA_ROPE, misc, 0.0), MLA_NOPE, 1)
    kpe = _rot_apply(kpe, c, s1, s2, MLA_ROPE // 2)
    q = _dot(cq, wq_ref[...])
    k = _dot(ckv, wk_ref[...])
    for h in range(N_HEADS):
        sl = slice(h * LANES, (h + 1) * LANES)
        qt_ref[sl, :] = _rot_apply(q[:, sl], c, s1, s2, MLA_ROPE // 2).T.astype(BF16)
        k_ref[:, sl] = (k[:, sl] + kpe).astype(BF16)
    v = _dot(ckv, wv_ref[...])
    for g in range(ATT_W // LANES):
        sl = slice(g * LANES, (g + 1) * LANES)
        vt_ref[sl, :] = v[:, sl].T.astype(BF16)


def mla_prep(proj, qn, kvn, wq, wk, wv, tabs):
    t = proj.shape[0]
    tm = ATT_T
    full = lambda a: pl.BlockSpec(a.shape, lambda i: (0,) * a.ndim)
    row = lambda w, col: pl.BlockSpec((tm, w), lambda i: (i, col // w))
    tile_t = lambda n: pl.BlockSpec((None, n, tm), lambda i: (i, 0, 0))
    return pl.pallas_call(
        _mla_prep_kernel,
        grid=(t // tm,),
        in_specs=[row(MLA_Q_RANK, COL_CQL), row(MLA_KV_RANK, COL_CKV), row(LANES, COL_MISC),
                  full(qn), full(kvn), full(wq), full(wk), full(wv),
                  row(LANES, 0), row(LANES, 0), row(LANES, 0)],
        out_specs=[tile_t(N_HEADS * LANES),
                   pl.BlockSpec((tm, N_HEADS * LANES), lambda i: (i, 0)),
                   tile_t(ATT_W), tile_t(LANES)],
        out_shape=[jax.ShapeDtypeStruct((t // tm, N_HEADS * LANES, tm), BF16),
                   jax.ShapeDtypeStruct((t, N_HEADS * LANES), BF16),
                   jax.ShapeDtypeStruct((t // tm, ATT_W, tm), BF16),
                   jax.ShapeDtypeStruct((t // tm, LANES, tm), F32)],
        compiler_params=_cparams(("arbitrary",), VMEM_LIMIT),
        name="mla_prep",
    )(proj, proj, proj, qn, kvn, wq, wk, wv, *tabs)


def _mla_attn_kernel(qt_ref, k_ref, vt_ref, o_ref, s_ref):
    i = pl.program_id(1)
    tq = qt_ref.shape[1]
    tk = ATT_T
    nh = s_ref.shape[1]
    qh = [qt_ref[h * LANES:(h + 1) * LANES, :] for h in range(nh)]

    def score_fn(n):
        off = pl.multiple_of(n * tk, tk)
        return [_dot(k_ref[pl.ds(off, tk), h * LANES:(h + 1) * LANES], qh[h]) for h in range(nh)]

    def vt_fn(n, h):
        return vt_ref[n, h * HEAD_DIM:(h + 1) * HEAD_DIM, :]

    o_ref[...] = _attention_tile(i, score_fn, vt_fn, s_ref, tq, _causal_t(tk, tq)).astype(BF16)


def mla_attention(qt, k, vt):
    t = k.shape[0]
    tq = ATT_T
    wqk = ATT_HEADS * LANES
    wv = ATT_HEADS * HEAD_DIM
    return pl.pallas_call(
        _mla_attn_kernel,
        grid=(N_HEADS // ATT_HEADS, t // tq),
        in_specs=[pl.BlockSpec((None, wqk, tq), lambda g, i: (i, g, 0)),
                  pl.BlockSpec((t, wqk), lambda g, i: (0, g)),
                  pl.BlockSpec((t // tq, wv, tq), lambda g, i: (0, g, 0))],
        out_specs=pl.BlockSpec((tq, wv), lambda g, i: (i, g)),
        out_shape=jax.ShapeDtypeStruct((t, ATT_W), BF16),
        scratch_shapes=[pltpu.VMEM((2, ATT_HEADS, ATT_T, tq), F32)],
        compiler_params=_cparams(("arbitrary", "arbitrary"), VMEM_LIMIT),
        name="mla_attention",
    )(qt, k, vt)


def _silu(x):
    return x / (1.0 + jnp.exp(-x))


def _softplus(x):
    return jnp.maximum(x, 0.0) + jnp.log1p(jnp.exp(-jnp.abs(x)))


def _ssd_kernel(z_ref, xbc_ref, misc_ref, cw_ref, cb_ref, dtb_ref, a_ref, dsk_ref, nw_ref, exp_ref,
                o_ref, ext_ref, state_ref):
    c = pl.program_id(0)
    L = SSM_CHUNK
    half = SSM_INNER // SSM_GROUPS

    @pl.when(c == 0)
    def _():
        ext_ref[0:8, :] = jnp.zeros((8, SSM_CONV_CH), F32)
        state_ref[...] = jnp.zeros(state_ref.shape, F32)

    ext_ref[8:8 + L, :] = xbc_ref[...]
    conv = jnp.zeros((L, SSM_CONV_CH), F32) + cb_ref[...]
    for kk in range(SSM_CONV):
        conv = conv + cw_ref[kk:kk + 1, :] * ext_ref[pl.ds(8 - (SSM_CONV - 1) + kk, L), :]
    tail = ext_ref[L:L + 8, :]
    ext_ref[0:8, :] = tail
    xbc = _silu(conv)
    xs = xbc[:, 0:SSM_INNER]
    bm = xbc[:, SSM_INNER:SSM_INNER + SSM_GROUPS * SSM_STATE]
    cm = xbc[:, SSM_INNER + SSM_GROUPS * SSM_STATE:]

    dt = _softplus(misc_ref[...] + dtb_ref[...])
    da = dt * a_ref[...]
    r = lax.broadcasted_iota(I32, (L, L), 0)
    cc = lax.broadcasted_iota(I32, (L, L), 1)
    tri = r >= cc
    a_cum = _dot_hi(jnp.where(tri, 1.0, 0.0), da)
    a_cum_t = a_cum.T
    expand = exp_ref[...]
    dt_full = _dot_hi(dt, expand)
    acum_full = _dot_hi(a_cum, expand)
    a_last = acum_full[L - 1:L, :]
    xdt = xs * dt_full
    xdt_b = xdt.astype(BF16)

    ys = []
    for g in range(SSM_GROUPS):
        bg = bm[:, g * SSM_STATE:(g + 1) * SSM_STATE].astype(BF16)
        cg = cm[:, g * SSM_STATE:(g + 1) * SSM_STATE].astype(BF16)
        cb = _dot_nt(cg, bg)
        hpg = SSM_HEADS // SSM_GROUPS
        cols = []
        for hh in range(hpg):
            h = g * hpg + hh
            diff = a_cum[:, MISC_DT + h:MISC_DT + h + 1] - a_cum_t[MISC_DT + h:MISC_DT + h + 1, :]
            seg = jnp.where(tri, jnp.exp(jnp.where(tri, diff, 0.0)), 0.0)
            sc = (cb * seg).astype(BF16)
            pair = (h // 2) * LANES
            yp = _dot(sc, xdt_b[:, pair:pair + LANES])
            cols.append(yp[:, (h % 2) * SSM_HEAD_DIM:(h % 2 + 1) * SSM_HEAD_DIM])
        y_diag = jnp.concatenate(cols, axis=1)
        gs = slice(g * half, (g + 1) * half)
        st = state_ref[:, gs]
        y_off = _dot(cg, st.astype(BF16)) * jnp.exp(acum_full[:, gs])
        ys.append(y_diag + y_off)
        xw = (xdt[:, gs] * jnp.exp(a_last[:, gs] - acum_full[:, gs])).astype(BF16)
        new = _dot(bm[:, g * SSM_STATE:(g + 1) * SSM_STATE].T.astype(BF16), xw)
        state_ref[:, gs] = st * jnp.exp(a_last[:, gs]) + new
    y = jnp.concatenate(ys, axis=1) + xs * dsk_ref[...]
    y = y * _silu(z_ref[...])
    o_ref[...] = _rms(y, nw_ref[...]).astype(BF16)


def ssd_mixer(proj, conv_w, conv_b, dtb_row, a_row, dsk_row, norm_w, expand):
    t = proj.shape[0]
    L = SSM_CHUNK
    full = lambda a: pl.BlockSpec(a.shape, lambda c: (0,) * a.ndim)
    return pl.pallas_call(
        _ssd_kernel,
        grid=(t // L,),
        in_specs=[pl.BlockSpec((L, SSM_INNER), lambda c: (c, COL_MZ // SSM_INNER)),
                  pl.BlockSpec((L, SSM_CONV_CH), lambda c: (c, COL_XBC // SSM_CONV_CH)),
                  pl.BlockSpec((L, LANES), lambda c: (c, COL_MISC // LANES)),
                  full(conv_w), full(conv_b), full(dtb_row), full(a_row), full(dsk_row), full(norm_w),
                  full(expand)],
        out_specs=pl.BlockSpec((L, SSM_INNER), lambda c: (c, 0)),
        out_shape=jax.ShapeDtypeStruct((t, SSM_INNER), BF16),
        scratch_shapes=[pltpu.VMEM((L + 8, SSM_CONV_CH), F32),
                        pltpu.VMEM((SSM_STATE, SSM_INNER), F32)],
        compiler_params=_cparams(("arbitrary",), VMEM_LIMIT),
        name="ssd_mixer",
    )(proj, proj, proj, conv_w, conv_b, dtb_row, a_row, dsk_row, norm_w, expand)


GATE_COL0 = sum(SPLIT_SIZES[:-1])
GATE_WIN = 3


def _merge_kernel(x_ref, *refs):
    w_refs = refs[:N_BRANCH * GATE_WIN]
    bg_ref, oa_ref, ob_ref, oc_ref, od_ref, wb_ref, o_ref, wg_ref = refs[N_BRANCH * GATE_WIN:]
    tn = o_ref.shape[1]
    off = GATE_COL0 % LANES

    @pl.when(pl.program_id(1) == 0)
    def _():
        for g in range(N_BRANCH):
            win = jnp.concatenate([w_refs[g * GATE_WIN + k][...] for k in range(GATE_WIN)], axis=1)
            wg_ref[g] = win[:, off:off + tn].astype(BF16)

    x = x_ref[...]
    acc = None
    for g, br in enumerate((oa_ref, ob_ref, oc_ref, od_ref)):
        logit = _dot(x, wg_ref[g]) + bg_ref[g:g + 1, :]
        gate = 1.0 / (1.0 + jnp.exp(-logit))
        term = gate * _dot(br[...], wb_ref[g])
        acc = term if acc is None else acc + term
    o_ref[...] = acc.astype(BF16)


def gated_merge(x_bf, layer, w_in, bg, branches, wb, tm=1024):
    t = x_bf.shape[0]
    tn = (GATE_WIN - 1) * LANES
    br_spec = pl.BlockSpec((tm, BRANCH_W), lambda j, i: (i, 0))
    nj = D_MODEL // tn
    blk0 = GATE_COL0 // LANES

    def win_spec(g, k):
        return pl.BlockSpec((None, D_MODEL, LANES),
                            lambda j, i: (layer, 0, blk0 + (g * D_MODEL + j * tn) // LANES + k))

    w_specs = [win_spec(g, k) for g in range(N_BRANCH) for k in range(GATE_WIN)]
    return pl.pallas_call(
        _merge_kernel,
        grid=(nj, t // tm),
        in_specs=[pl.BlockSpec((tm, D_MODEL), lambda j, i: (i, 0)),
                  *w_specs,
                  pl.BlockSpec((N_BRANCH, tn), lambda j, i: (0, j)),
                  br_spec, br_spec, br_spec, br_spec,
                  pl.BlockSpec((N_BRANCH, BRANCH_W, tn), lambda j, i: (0, 0, j))],
        out_specs=pl.BlockSpec((tm, tn), lambda j, i: (i, j)),
        out_shape=jax.ShapeDtypeStruct((t, D_MODEL), BF16),
        scratch_shapes=[pltpu.VMEM((N_BRANCH, D_MODEL, tn), BF16)],
        compiler_params=_cparams(("arbitrary", "arbitrary"), VMEM_LIMIT),
        name="gated_merge",
    )(x_bf, *([w_in] * (N_BRANCH * GATE_WIN)), bg, *branches, wb)


def _layer_norm(v, g, b):
    mu = jnp.mean(v, axis=-1, keepdims=True)
    d = v - mu
    var = jnp.mean(d * d, axis=-1, keepdims=True)
    return d * lax.rsqrt(var + LN_EPS) * g + b


def _outproj_kernel(m_ref, w_ref, x_ref, g_ref, b_ref, rw_ref, rb_ref, x1_ref, x1b_ref, idx_ref, gate_ref):
    mixed = _dot(m_ref[...], w_ref[...])
    x1 = _layer_norm(DN_ALPHA * x_ref[...] + mixed, g_ref[...], b_ref[...])
    x1_ref[...] = x1
    x1b = x1.astype(BF16)
    x1b_ref[...] = x1b
    logits = _dot(x1b, rw_ref[...]) + rb_ref[...]
    lane = lax.broadcasted_iota(I32, logits.shape, 1)
    g = logits
    idx_out = jnp.zeros(logits.shape, I32)
    val_out = jnp.zeros(logits.shape, F32)
    vals = []
    for kk in range(TOP_K):
        m = jnp.max(g, axis=1, keepdims=True)
        idx = jnp.min(jnp.where(g == m, lane, LANES), axis=1, keepdims=True)
        g = jnp.where(lane == idx, -jnp.inf, g)
        idx_out = jnp.where(lane == kk, idx, idx_out)
        vals.append(m)
    es = [jnp.exp(v - vals[0]) for v in vals]
    den = es[0] + es[1] + es[2] + es[3]
    for kk in range(TOP_K):
        val_out = jnp.where(lane == kk, es[kk] / den, val_out)
    idx_ref[...] = idx_out
    gate_ref[...] = val_out


def outproj_ln_router(merged, w_out, x, g, b, rw, rb, tm=256):
    t = x.shape[0]
    full = lambda a: pl.BlockSpec(a.shape, lambda i: (0,) * a.ndim)
    row = lambda w: pl.BlockSpec((tm, w), lambda i: (i, 0))
    return pl.pallas_call(
        _outproj_kernel,
        grid=(t // tm,),
        in_specs=[row(D_MODEL), full(w_out), row(D_MODEL), full(g), full(b), full(rw), full(rb)],
        out_specs=[row(D_MODEL), row(D_MODEL), row(LANES), row(LANES)],
        out_shape=[jax.ShapeDtypeStruct((t, D_MODEL), F32), jax.ShapeDtypeStruct((t, D_MODEL), BF16),
                   jax.ShapeDtypeStruct((t, LANES), I32), jax.ShapeDtypeStruct((t, LANES), F32)],
        compiler_params=_cparams(("arbitrary",), VMEM_LIMIT),
        name="outproj_ln_router",
    )(merged, w_out, x, g, b, rw, rb)


def _start_row_gather(idx_ref, src_hbm, dst, sem, rows):
    for r in range(rows):
        pltpu.make_async_copy(src_hbm.at[pl.ds(idx_ref[0, r], 1)], dst.at[pl.ds(r, 1)], sem).start(priority=r % 2)


def _wait_row_gather(src_hbm, dst, sem, rows):
    pltpu.make_async_copy(src_hbm.at[pl.ds(0, rows)], dst, sem).wait()


def _expert_kernel(be_ref, nu_ref, tok_ref, tokn_ref, x_hbm, wgu_ref, bgu_ref, wd_ref, bd_ref, o_ref, xbuf, sem):
    blk = pl.program_id(0)
    n_used = nu_ref[0]
    slot = lax.rem(blk, 2)

    @pl.when(jnp.logical_and(blk == 0, n_used > 0))
    def _():
        _start_row_gather(tok_ref, x_hbm, xbuf.at[0], sem.at[0], MOE_ROWS)

    @pl.when(blk + 1 < n_used)
    def _():
        _start_row_gather(tokn_ref, x_hbm, xbuf.at[1 - slot], sem.at[1 - slot], MOE_ROWS)

    @pl.when(blk < n_used)
    def _():
        _wait_row_gather(x_hbm, xbuf.at[slot], sem.at[slot], MOE_ROWS)
        gu = _dot(xbuf[slot].astype(BF16), wgu_ref[0].astype(BF16)) + bgu_ref[0]
        glu = jnp.minimum(gu[:, :D_FF], SWIGLU_LIMIT)
        lin = jnp.clip(gu[:, D_FF:], -SWIGLU_LIMIT, SWIGLU_LIMIT)
        act = glu / (1.0 + jnp.exp(-SWIGLU_ALPHA * glu)) * (lin + 1.0)
        o_ref[...] = _dot(act.astype(BF16), wd_ref[0].astype(BF16)) + bd_ref[0]

    @pl.when(blk >= nu_ref[0])
    def _():
        o_ref[...] = jnp.zeros(o_ref.shape, F32)


def expert_ffn(block_e, n_used, row_tok, x, layer, wgu, bgu, wd, bd):
    n_blocks = row_tok.shape[0]
    n_rows = n_blocks * MOE_ROWS
    grid_spec = pltpu.PrefetchScalarGridSpec(
        num_scalar_prefetch=2,
        grid=(n_blocks,),
        in_specs=[pl.BlockSpec((None, 1, MOE_ROWS), lambda b, be, nu: (b, 0, 0), memory_space=pltpu.SMEM),
                  pl.BlockSpec((None, 1, MOE_ROWS), lambda b, be, nu: (jnp.minimum(b + 1, n_blocks - 1), 0, 0),
                               memory_space=pltpu.SMEM),
                  pl.BlockSpec(memory_space=pl.ANY),
                  pl.BlockSpec((None, 1, D_MODEL, 2 * D_FF), lambda b, be, nu: (layer, be[b], 0, 0)),
                  pl.BlockSpec((1, 1, 2 * D_FF), lambda b, be, nu: (be[b], 0, 0)),
                  pl.BlockSpec((None, 1, D_FF, D_MODEL), lambda b, be, nu: (layer, be[b], 0, 0)),
                  pl.BlockSpec((1, 1, D_MODEL), lambda b, be, nu: (be[b], 0, 0))],
        out_specs=pl.BlockSpec((MOE_ROWS, D_MODEL), lambda b, be, nu: (b, 0)),
        scratch_shapes=[pltpu.VMEM((2, MOE_ROWS, D_MODEL), F32), pltpu.SemaphoreType.DMA((2,))],
    )
    return pl.pallas_call(
        _expert_kernel,
        grid_spec=grid_spec,
        out_shape=jax.ShapeDtypeStruct((n_rows, D_MODEL), F32),
        compiler_params=_cparams(("arbitrary",), VMEM_LIMIT),
        name="expert_ffn",
    )(block_e, n_used, row_tok, row_tok, x, wgu, bgu, wd, bd)


COMBINE_TM = 128


def _combine_kernel(pos_ref, posn_ref, out_hbm, gate_ref, x_ref, g_ref, b_ref, o_ref, ob_ref, rbuf, sem):
    i = pl.program_id(0)
    tm = x_ref.shape[0]
    rows = TOP_K * tm
    slot = lax.rem(i, 2)

    @pl.when(i == 0)
    def _():
        _start_row_gather(pos_ref, out_hbm, rbuf.at[0], sem.at[0], rows)

    @pl.when(i + 1 < pl.num_programs(0))
    def _():
        _start_row_gather(posn_ref, out_hbm, rbuf.at[1 - slot], sem.at[1 - slot], rows)

    _wait_row_gather(out_hbm, rbuf.at[slot], sem.at[slot], rows)
    gate = gate_ref[...]
    y = jnp.zeros(x_ref.shape, F32)
    for kk in range(TOP_K):
        y = y + gate[:, kk:kk + 1] * rbuf[slot, kk * tm:(kk + 1) * tm, :]
    x2 = _layer_norm(DN_ALPHA * x_ref[...] + y, g_ref[...], b_ref[...])
    o_ref[...] = x2
    ob_ref[...] = x2.astype(BF16)


def combine_ln(pos_km, out, gates, x1, g, b):
    t = x1.shape[0]
    tm = COMBINE_TM
    n_tiles = t // tm
    full = lambda a: pl.BlockSpec(a.shape, lambda i: (0,) * a.ndim)
    row = lambda w: pl.BlockSpec((tm, w), lambda i: (i, 0))
    return pl.pallas_call(
        _combine_kernel,
        grid=(n_tiles,),
        in_specs=[pl.BlockSpec((None, 1, TOP_K * tm), lambda i: (i, 0, 0), memory_space=pltpu.SMEM),
                  pl.BlockSpec((None, 1, TOP_K * tm), lambda i: (jnp.minimum(i + 1, n_tiles - 1), 0, 0),
                               memory_space=pltpu.SMEM),
                  pl.BlockSpec(memory_space=pl.ANY), row(LANES), row(D_MODEL), full(g), full(b)],
        out_specs=[row(D_MODEL), row(D_MODEL)],
        out_shape=[jax.ShapeDtypeStruct((t, D_MODEL), F32), jax.ShapeDtypeStruct((t, D_MODEL), BF16)],
        scratch_shapes=[pltpu.VMEM((2, TOP_K * tm, D_MODEL), F32), pltpu.SemaphoreType.DMA((2,))],
        compiler_params=_cparams(("arbitrary",), VMEM_LIMIT),
        name="combine_ln",
    )(pos_km, pos_km, out, gates, x1, g, b)


def _rope_tables(positions):
    pos = positions.reshape(-1).astype(F32)

    def cs(dim, theta):
        inv = theta ** (-jnp.arange(0, dim, 2, dtype=F32) / dim)
        ang = pos[:, None] * inv
        return jnp.cos(ang), jnp.sin(ang)

    t = pos.shape[0]
    cos_p, sin_p = cs(ROT_DIM, ROPE_THETA)
    half = ROT_DIM // 2
    one = jnp.ones((t, HEAD_DIM - ROT_DIM), F32)
    zero = jnp.zeros((t, HEAD_DIM - ROT_DIM), F32)
    zh = jnp.zeros((t, half), F32)
    c64 = jnp.concatenate([cos_p, cos_p, one], axis=1)
    s1_64 = jnp.concatenate([-sin_p, zh, zero], axis=1)
    s2_64 = jnp.concatenate([zh, sin_p, zero], axis=1)
    tabs_p = tuple(jnp.tile(a, (1, 2)) for a in (c64, s1_64, s2_64))

    cos_m, sin_m = cs(MLA_ROPE, MLA_THETA)
    hm = MLA_ROPE // 2
    pre1 = jnp.ones((t, MLA_NOPE), F32)
    pre0 = jnp.zeros((t, MLA_NOPE), F32)
    post1 = jnp.ones((t, LANES - MLA_NOPE - MLA_ROPE), F32)
    post0 = jnp.zeros((t, LANES - MLA_NOPE - MLA_ROPE), F32)
    zm = jnp.zeros((t, hm), F32)
    c_m = jnp.concatenate([pre1, cos_m, cos_m, post1], axis=1)
    s1_m = jnp.concatenate([pre0, -sin_m, zm, post0], axis=1)
    s2_m = jnp.concatenate([pre0, zm, sin_m, post0], axis=1)
    return tabs_p, (c_m, s1_m, s2_m)


def _prep_in_weights(w_in):
    pts = np.cumsum((0,) + SPLIT_SIZES)
    seg = lambda k: w_in[:, pts[k]:pts[k + 1]]
    (a_q, a_k, a_v, m_z, m_xbc, m_dt, c_q, c_k, c_v, c_qi, c_ki, c_wi, d_cq, d_ckv, d_kpe) = [seg(k) for k in range(15)]
    scale = HEAD_DIM ** -0.5 * LOG2E
    zeros = lambda n: jnp.zeros((D_MODEL, n), w_in.dtype)
    cols = [a_q * scale, a_k, a_v, c_q * scale, c_k, c_v, c_qi, c_ki, c_ki, c_ki, c_ki,
            m_z, m_xbc, d_ckv, d_cq, d_kpe, m_dt, c_wi * IDX_SCALE]
    small = jnp.concatenate(cols, axis=1)
    small = jnp.concatenate([small, zeros(N_SMALL - small.shape[1])], axis=1).astype(BF16)
    return small


def _prep_mla_weights(w_uq, w_ukv):
    dq = MLA_NOPE + MLA_ROPE
    wq = w_uq.reshape(MLA_Q_RANK, N_HEADS, dq) * (dq ** -0.5 * LOG2E)
    wq = jnp.pad(wq, ((0, 0), (0, 0), (0, LANES - dq))).reshape(MLA_Q_RANK, N_HEADS * LANES)
    wkv = w_ukv.reshape(MLA_KV_RANK, N_HEADS, MLA_NOPE + MLA_V)
    wk = jnp.pad(wkv[:, :, :MLA_NOPE], ((0, 0), (0, 0), (0, LANES - MLA_NOPE))).reshape(MLA_KV_RANK, N_HEADS * LANES)
    wv = wkv[:, :, MLA_NOPE:].reshape(MLA_KV_RANK, N_HEADS * MLA_V)
    return wq.astype(BF16), wk.astype(BF16), wv.astype(BF16)


def _misc_row(v):
    return jnp.zeros((1, LANES), F32).at[0, MISC_DT:MISC_DT + SSM_HEADS].set(v.astype(F32))


def _route(top_idx, t):
    tk = t * TOP_K
    flat_e = top_idx.reshape(-1)
    onehot = (flat_e[:, None] == jnp.arange(N_EXPERTS)[None, :]).astype(I32)
    seen = jnp.cumsum(onehot, axis=0)
    counts = seen[-1]
    padded = ((counts + MOE_ROWS - 1) // MOE_ROWS) * MOE_ROWS
    pad_end = jnp.cumsum(padded)
    pad_start = pad_end - padded
    rank = jnp.sum(onehot * seen, axis=1) - 1
    pos = (pad_start[flat_e] + rank).astype(I32)
    n_blocks = tk // MOE_ROWS + N_EXPERTS
    n_rows = n_blocks * MOE_ROWS
    row_tok = jnp.zeros((n_rows,), I32).at[pos].set(jnp.arange(tk, dtype=I32) // TOP_K)
    block_start = jnp.arange(n_blocks) * MOE_ROWS
    block_e = jnp.minimum(jnp.sum(pad_end[None, :] <= block_start[:, None], axis=1), N_EXPERTS - 1).astype(I32)
    n_used = (pad_end[-1] // MOE_ROWS).astype(I32).reshape(1)
    return row_tok, pos, block_e, n_used


def _layer(x, x_bf, tabs_p, tabs_m, flags, expand, p, layer, big):
    t = x.shape[0]
    w_small = _prep_in_weights(p["w_in"])
    proj = matmul_bf16(x_bf, w_small)
    cast, cast_t = rope_cast(proj, flags, tabs_p)

    o_a = moba_attention(cast, cast_t)
    o_b = ssd_mixer(proj, p["conv_w"], p["conv_b"].reshape(1, -1), _misc_row(p["dt_bias"]),
                    _misc_row(-jnp.exp(p["a_log"].astype(F32))),
                    jnp.repeat(p["d_skip"].astype(F32), SSM_HEAD_DIM).reshape(1, -1),
                    p["ssm_norm_w"].reshape(1, -1), expand)
    wq, wk, wv = _prep_mla_weights(p["w_uq"], p["w_ukv"])
    qt_m, k_m, vt_m, misc_t = mla_prep(proj, p["q_norm_w"].reshape(1, -1), p["kv_norm_w"].reshape(1, -1), wq, wk, wv,
                                       tabs_m)
    o_c = dsa_attention(cast, cast_t, dsa_select(cast, cast_t, misc_t))
    o_d = mla_attention(qt_m, k_m, vt_m)

    merged = gated_merge(x_bf, layer, big["w_in"], p["b_gate"], (o_a, o_b, o_c, o_d), p["w_branch"].astype(BF16))
    rw = jnp.pad(p["router_w"], ((0, 0), (0, LANES - N_EXPERTS))).astype(BF16)
    rb = jnp.concatenate([p["router_b"].astype(F32), jnp.full((LANES - N_EXPERTS,), -jnp.inf, F32)]).reshape(1, -1)
    x1, x1_bf, top_idx, gates = outproj_ln_router(merged, p["w_out"].astype(BF16), x, p["ln1_g"].reshape(1, -1),
                                                  p["ln1_b"].reshape(1, -1), rw, rb)

    row_tok, pos, block_e, n_used = _route(top_idx[:, :TOP_K], t)
    out = expert_ffn(block_e, n_used, row_tok.reshape(-1, 1, MOE_ROWS), x1, layer, big["w_gate_up"],
                     p["b_gate_up"][:, None, :], big["w_down"], p["b_down"][:, None, :])
    pos_km = pos.reshape(t // COMBINE_TM, COMBINE_TM, TOP_K).transpose(0, 2, 1).reshape(-1, 1, TOP_K * COMBINE_TM)
    return combine_ln(pos_km, out, gates, x1, p["ln2_g"].reshape(1, -1), p["ln2_b"].reshape(1, -1))


def kernel(x, positions, w_in, b_gate, conv_w, conv_b, dt_bias, a_log, d_skip, ssm_norm_w, q_norm_w, w_uq,
           kv_norm_w, w_ukv, w_branch, w_out, ln1_g, ln1_b, router_w, router_b, w_gate_up, b_gate_up, w_down,
           b_down, ln2_g, ln2_b):
    b, s, d = x.shape
    assert b == 1 and d == D_MODEL
    names = ("w_in", "b_gate", "conv_w", "conv_b", "dt_bias", "a_log", "d_skip", "ssm_norm_w", "q_norm_w", "w_uq",
             "kv_norm_w", "w_ukv", "w_branch", "w_out", "ln1_g", "ln1_b", "router_w", "router_b", "w_gate_up",
             "b_gate_up", "w_down", "b_down", "ln2_g", "ln2_b")
    stacked = (w_in, b_gate, conv_w, conv_b, dt_bias, a_log, d_skip, ssm_norm_w, q_norm_w, w_uq, kv_norm_w, w_ukv,
               w_branch, w_out, ln1_g, ln1_b, router_w, router_b, w_gate_up, b_gate_up, w_down, b_down, ln2_g, ln2_b)
    tabs_p, tabs_m = _rope_tables(positions)
    flags = jnp.array([1, 1, 0, 1, 1, 0, 1], I32)
    head_of_lane = jnp.arange(SSM_INNER) // SSM_HEAD_DIM
    expand = (jnp.arange(LANES)[:, None] == (MISC_DT + head_of_lane)[None, :]).astype(F32)

    big = {"w_in": w_in, "w_gate_up": w_gate_up, "w_down": w_down}
    xt = x.reshape(s, d)
    xt_bf = xt.astype(BF16)
    for l in range(DEPTH):
        p = {n: a[l] for n, a in zip(names, stacked)}
        xt, xt_bf = _layer(xt, xt_bf, tabs_p, tabs_m, flags, expand, p, l, big)
    return xt.reshape(b, s, d)
```

```python
import functools
import math

import jax
import jax.numpy as jnp
import numpy as np
from jax import lax
from jax.experimental import pallas as pl
from jax.experimental.pallas import tpu as pltpu

F32 = jnp.float32
BF16 = jnp.bfloat16
I32 = jnp.int32
I16 = jnp.int16

D_MODEL = 2048
DEPTH = 4
HEAD_DIM = 64
ROT_DIM = HEAD_DIM // 4
ROPE_THETA = 500000.0
N_HEADS = 8
ATT_W = N_HEADS * HEAD_DIM
MOBA_BLOCK = 256
MOBA_TOPK = 3
SSM_HEADS = 8
SSM_HEAD_DIM = 64
SSM_INNER = 512
SSM_STATE = 128
SSM_GROUPS = 2
SSM_CONV = 4
SSM_CONV_CH = 1024
SSM_CHUNK = 256
IDX_HEADS = 4
IDX_DIM = 64
DSA_TOPK = 256
IDX_SCALE = (IDX_HEADS * IDX_DIM) ** -0.5
MLA_Q_RANK = 384
MLA_KV_RANK = 256
MLA_NOPE = 64
MLA_ROPE = 32
MLA_V = 64
MLA_THETA = 10000.0
N_BRANCH = 4
BRANCH_W = 512
SPLIT_SIZES = (512, 512, 512, 512, 1024, 8, 512, 512, 512, 256, 64, 4, 384, 256, 32, N_BRANCH * D_MODEL)
N_EXPERTS = 32
TOP_K = 4
D_FF = 768
SWIGLU_ALPHA = 1.702
SWIGLU_LIMIT = 7.0
DN_ALPHA = (2 * DEPTH) ** 0.25
LN_EPS = 1e-5

LANES = 128
NEG = -1e30
VMEM_LIMIT = 56 * 1024 * 1024

COL_AQ, COL_AK, COL_AV = 0, 512, 1024
COL_CQ, COL_CK, COL_CV = 1536, 2048, 2560
COL_CQI, COL_KI4 = 3072, 3328
N_CAST = 3584
COL_XBC = 0
COL_MZ = 1024
COL_CQL = 1536
COL_MISC = 1920
COL_CKV = 2048
N_REST = 2304
N_GATE0 = sum(SPLIT_SIZES[:-1])
MISC_DT = 32
MISC_WI = 40
MOE_ROWS = 256


def _cparams(sem, vmem=None):
    return pltpu.CompilerParams(dimension_semantics=sem, vmem_limit_bytes=vmem)


def _dot(a, b):
    return jnp.dot(a, b, preferred_element_type=F32)


def _dot_nt(a, b):
    return lax.dot_general(a, b, (((1,), (1,)), ((), ())), preferred_element_type=F32)


def _dot_hi(a, b):
    return jnp.dot(a, b, preferred_element_type=F32, precision=lax.Precision.HIGHEST)


def _mm_kernel(x_ref, w_ref, o_ref):
    o_ref[...] = _dot(x_ref[...], w_ref[...])


def matmul_bf16(x, w, tm=512, tn=1024):
    m, k = x.shape
    n = w.shape[1]
    return pl.pallas_call(
        _mm_kernel,
        grid=(n // tn, m // tm),
        in_specs=[pl.BlockSpec((tm, k), lambda j, i: (i, 0)),
                  pl.BlockSpec((k, tn), lambda j, i: (0, j))],
        out_specs=pl.BlockSpec((tm, tn), lambda j, i: (i, j)),
        out_shape=jax.ShapeDtypeStruct((m, n), F32),
        compiler_params=_cparams(("arbitrary", "arbitrary"), VMEM_LIMIT),
        name="in_proj",
    )(x, w)


def _rot_apply(x, c, s1, s2, shift):
    return x * c + pltpu.roll(x, LANES - shift, 1) * s1 + pltpu.roll(x, shift, 1) * s2


def _proj_attn_kernel(flag_ref, x_ref, w_ref, c_ref, s1_ref, s2_ref, o_ref, ot_ref):
    j = pl.program_id(0)
    tm, width = o_ref.shape
    tile = ot_ref.shape[2]
    acc = _dot(x_ref[...], w_ref[...])

    def emit(g, y):
        sl = slice(g * LANES, (g + 1) * LANES)
        o_ref[:, sl] = y.astype(BF16)
        for s in range(tm // tile):
            ot_ref[s, sl, :] = y[s * tile:(s + 1) * tile, :].T.astype(BF16)

    @pl.when(flag_ref[j] == 1)
    def _():
        c, s1, s2 = c_ref[...], s1_ref[...], s2_ref[...]
        for g in range(width // LANES):
            emit(g, _rot_apply(acc[:, g * LANES:(g + 1) * LANES], c, s1, s2, ROT_DIM // 2))

    @pl.when(flag_ref[j] == 0)
    def _():
        for g in range(width // LANES):
            emit(g, acc[:, g * LANES:(g + 1) * LANES])


def proj_attention(x_bf, w, flags, tabs, tm=512, tw=512):
    t, k = x_bf.shape
    sub = tm // ATT_T
    grid_spec = pltpu.PrefetchScalarGridSpec(
        num_scalar_prefetch=1,
        grid=(N_CAST // tw, t // tm),
        in_specs=[pl.BlockSpec((tm, k), lambda j, i, f: (i, 0)),
                  pl.BlockSpec((k, tw), lambda j, i, f: (0, j)),
                  pl.BlockSpec((tm, LANES), lambda j, i, f: (i, 0)),
                  pl.BlockSpec((tm, LANES), lambda j, i, f: (i, 0)),
                  pl.BlockSpec((tm, LANES), lambda j, i, f: (i, 0))],
        out_specs=[pl.BlockSpec((tm, tw), lambda j, i, f: (i, j)),
                   pl.BlockSpec((sub, tw, ATT_T), lambda j, i, f: (i, j, 0))],
    )
    return pl.pallas_call(
        _proj_attn_kernel,
        grid_spec=grid_spec,
        out_shape=[jax.ShapeDtypeStruct((t, N_CAST), BF16),
                   jax.ShapeDtypeStruct((t // ATT_T, N_CAST, ATT_T), BF16)],
        compiler_params=_cparams(("arbitrary", "arbitrary"), VMEM_LIMIT),
        name="proj_attention",
    )(flags, x_bf, w, *tabs)


ATT_T = 256
ATT_HEADS = 4


LOG2E = 1.4426950408889634
ACC_ROWS = HEAD_DIM + 16


def _softmax_step(s, smax, vt, state, keep=None):
    m, acc = state
    if keep is not None:
        smax = jnp.where(keep, smax, NEG)
    m_new = jnp.maximum(m, smax)
    shift = m_new if keep is None else jnp.where(keep, m_new, -NEG)
    p = jnp.exp2(s - shift).astype(BF16)
    vt_ext = jnp.concatenate([vt, jnp.ones((ACC_ROWS - HEAD_DIM, vt.shape[1]), vt.dtype)], axis=0)
    acc_new = jnp.exp2(m - m_new) * acc + _dot(vt_ext, p)
    return m_new, acc_new


def _init_state(tq):
    return (jnp.full((1, tq), NEG, F32), jnp.zeros((ACC_ROWS, tq), F32))


def _causal_t(tk, tq):
    return lax.broadcasted_iota(I32, (tk, tq), 0) <= lax.broadcasted_iota(I32, (tk, tq), 1)


def _head_rows(qt):
    row = lax.broadcasted_iota(I32, (LANES, 1), 0)
    out = []
    for p in range(qt.shape[0] // LANES):
        blk = qt[p * LANES:(p + 1) * LANES, :]
        zero = jnp.zeros_like(blk)
        out += [jnp.where(row < HEAD_DIM, blk, zero), jnp.where(row >= HEAD_DIM, blk, zero)]
    return out


def _pair_cols(h):
    return slice((h // 2) * LANES, (h // 2 + 1) * LANES)


def _attention_tile(i, score_fn, vt_fn, s_ref, tq, last_mask, keep_fn=None):
    nh = s_ref.shape[1]

    def park(scores, buf):
        for h in range(nh):
            s_ref[buf, h] = scores[h]
        return [jnp.max(sc, axis=0, keepdims=True) for sc in scores]

    def step(n, carry, buf, nxt_buf):
        nxt = score_fn(n + 1)
        states = []
        for h in range(nh):
            smax, m, acc = carry[3 * h:3 * h + 3]
            keep = None if keep_fn is None else keep_fn(n, h)
            states.append(_softmax_step(s_ref[buf, h], smax, vt_fn(n, h), (m, acc), keep))
        smax_next = park(nxt, nxt_buf)
        out = []
        for h in range(nh):
            out.extend((smax_next[h],) + tuple(states[h]))
        return tuple(out)

    def pair(mm, carry):
        return step(2 * mm + 1, step(2 * mm, carry, 0, 1), 1, 0)

    init = []
    for smax in park(score_fn(0), 0):
        init.extend((smax,) + _init_state(tq))
    carry = lax.fori_loop(0, i // 2, pair, tuple(init))
    carry = lax.cond(i % 2 == 1, lambda c: step(i - 1, c, 0, 0), lambda c: c, carry)
    fin = []
    for h in range(nh):
        smax, m, acc = carry[3 * h:3 * h + 3]
        s = s_ref[0, h]
        if last_mask is not None:
            s = jnp.where(last_mask, s, NEG)
            smax = jnp.max(s, axis=0, keepdims=True)
        _, acc = _softmax_step(s, smax, vt_fn(i, h), (m, acc))
        fin.append(acc[0:HEAD_DIM] / acc[HEAD_DIM:HEAD_DIM + 1])
    return jnp.concatenate(fin, axis=0).T


def _top_n_rows(g, valid, n):
    row = lax.broadcasted_iota(I32, g.shape, 0)
    g = jnp.where(valid, g, -jnp.inf)
    sel = jnp.zeros(g.shape, jnp.bool_)
    for _ in range(n):
        m = jnp.max(g, axis=0, keepdims=True)
        idx = jnp.min(jnp.where(g == m, row, g.shape[0]), axis=0, keepdims=True)
        pick = row == idx
        sel = jnp.logical_or(sel, pick)
        g = jnp.where(pick, -jnp.inf, g)
    return jnp.logical_and(sel, valid)


def _moba_kernel(qt_ref, k_ref, vt_ref, o_ref, kmean_ref, sel_ref, s_ref):
    i = pl.program_id(1)
    tq = qt_ref.shape[1]
    nb = k_ref.shape[0] // MOBA_BLOCK

    @pl.when(i == 0)
    def _():
        kmean_ref[...] = jnp.zeros(kmean_ref.shape, F32)

        def mean_body(n, c):
            kb = k_ref[pl.ds(pl.multiple_of(n * MOBA_BLOCK, MOBA_BLOCK), MOBA_BLOCK), :].astype(F32)
            kmean_ref[pl.ds(n, 1), :] = jnp.sum(kb, axis=0, keepdims=True) * (1.0 / MOBA_BLOCK)
            return c

        lax.fori_loop(0, nb, mean_body, 0)

    nh = s_ref.shape[1]
    qh = _head_rows(qt_ref[...])
    kmean = kmean_ref[...].astype(BF16)
    row_n = lax.broadcasted_iota(I32, (kmean.shape[0], tq), 0)
    for h in range(nh):
        gate = _dot(kmean[:, _pair_cols(h)], qh[h])
        sel_ref[h] = jnp.where(_top_n_rows(gate, row_n < i, MOBA_TOPK), 1.0, 0.0)

    def score_fn(n):
        kb = k_ref[pl.ds(pl.multiple_of(n * MOBA_BLOCK, MOBA_BLOCK), MOBA_BLOCK), :]
        return [_dot(kb[:, _pair_cols(h)], qh[h]) for h in range(nh)]

    def vt_fn(n, h):
        return vt_ref[n, h * HEAD_DIM:(h + 1) * HEAD_DIM, :]

    def keep_fn(n, h):
        return sel_ref[h, pl.ds(n, 1), :] > 0.5

    o_ref[...] = _attention_tile(i, score_fn, vt_fn, s_ref, tq, _causal_t(MOBA_BLOCK, tq), keep_fn).astype(BF16)


def moba_attention(cast, cast_t):
    t = cast.shape[0]
    tq = MOBA_BLOCK
    w = ATT_HEADS * HEAD_DIM
    nbp = max(8, t // MOBA_BLOCK)
    return pl.pallas_call(
        _moba_kernel,
        grid=(N_HEADS // ATT_HEADS, t // tq),
        in_specs=[pl.BlockSpec((None, w, tq), lambda g, i: (i, COL_AQ // w + g, 0)),
                  pl.BlockSpec((t, w), lambda g, i: (0, COL_AK // w + g)),
                  pl.BlockSpec((t // tq, w, tq), lambda g, i: (0, COL_AV // w + g, 0))],
        out_specs=pl.BlockSpec((tq, w), lambda g, i: (i, g)),
        out_shape=jax.ShapeDtypeStruct((t, ATT_W), BF16),
        scratch_shapes=[pltpu.VMEM((nbp, w), F32), pltpu.VMEM((ATT_HEADS, nbp, tq), F32),
                        pltpu.VMEM((2, ATT_HEADS, MOBA_BLOCK, tq), F32)],
        compiler_params=_cparams(("arbitrary", "arbitrary"), VMEM_LIMIT),
        name="moba_attention",
    )(cast_t, cast, cast_t)


HALF16 = 1 << 15


def _float_key(x):
    b = lax.bitcast_convert_type(x, I32)
    return jnp.where(b < 0, b ^ jnp.int32(0x7FFFFFFF), b)


def _dsa_select_kernel(qit_ref, ki_ref, misct_ref, bias_ref, key_ref, hi_ref, lo_ref):
    i = pl.program_id(0)
    tq = qit_ref.shape[1]
    tk = ATT_T
    n_chunks = ki_ref.shape[0] // tk
    n_live = i + 1

    qit = qit_ref[...]
    row = lax.broadcasted_iota(I32, (IDX_HEADS * IDX_DIM, 1), 0)
    zero = jnp.zeros_like(qit)
    qh = [jnp.where((row >= h * IDX_DIM) & (row < (h + 1) * IDX_DIM), qit, zero) for h in range(IDX_HEADS)]
    wrow = [misct_ref[MISC_WI + h:MISC_WI + h + 1, :] for h in range(IDX_HEADS)]
    visible = _causal_t(tk, tq)

    def score_body(j, c):
        kic = ki_ref[pl.ds(pl.multiple_of(j * tk, tk), tk), :]
        acc = jnp.zeros((tk, tq), F32)
        for h in range(IDX_HEADS):
            acc = acc + wrow[h] * jnp.maximum(_dot(kic, qh[h]), 0.0)
        acc = jnp.where(jnp.logical_or(j < i, visible), acc, -jnp.inf)
        key = _float_key(acc)
        key_ref[j] = key
        hi_ref[j] = (key >> 16).astype(I16)
        lo_ref[j] = ((key & 0xFFFF) - HALF16).astype(I16)
        return c

    lax.fori_loop(0, n_live, score_body, 0)

    def count_ge(cand):
        def body(j, cnt):
            for r in range(tk // 8):
                cnt = cnt + jnp.where(key_ref[j, r * 8:(r + 1) * 8, :] >= cand, 1.0, 0.0)
            return cnt
        cnt = lax.fori_loop(0, n_live, body, jnp.zeros((8, tq), F32))
        return jnp.sum(cnt, axis=0, keepdims=True)

    def count16(ref, cand):
        c16 = cand.astype(I16)

        def body(j, cnt):
            for r in range(tk // 16):
                cnt = cnt + jnp.where(ref[j, r * 16:(r + 1) * 16, :] >= c16, jnp.int16(1), jnp.int16(0))
            return cnt
        cnt = lax.fori_loop(0, n_live, body, jnp.zeros((16, tq), I16))
        return jnp.sum(cnt.astype(I32), axis=0, keepdims=True).astype(F32)

    def digit_select(ref, want):
        base = jnp.where(count16(ref, jnp.zeros((1, tq), I32)) >= want, 0, -HALF16)

        def bit_body(b, d):
            cand = d | (jnp.int32(1) << (14 - b))
            return jnp.where(count16(ref, cand) >= want, cand, d)

        return lax.fori_loop(0, 15, bit_body, base)

    topk = float(DSA_TOPK)
    t_hi = digit_select(hi_ref, topk)
    want_lo = topk - count16(hi_ref, t_hi + 1)
    t_hi16 = t_hi.astype(I16)

    def mask_body(j, c):
        lo_ref[j] = jnp.where(hi_ref[j] == t_hi16, lo_ref[j], jnp.int16(-HALF16))
        return c

    lax.fori_loop(0, n_live, mask_body, 0)
    t_lo = digit_select(lo_ref, want_lo)
    thr = (t_hi << 16) | (t_lo + HALF16)

    need = topk - count_ge(thr + 1)
    lower = jnp.where(lax.broadcasted_iota(I32, (tk, tk), 0) >= lax.broadcasted_iota(I32, (tk, tk), 1),
                      1.0, 0.0).astype(BF16)

    def bias_body(j, seen):
        kk = key_ref[j]
        eq = kk == thr
        rank = _dot(lower, jnp.where(eq, 1.0, 0.0).astype(BF16)) + seen
        keep = ((kk > thr) | (eq & (rank <= need))) & jnp.logical_or(j < i, visible)
        bias_ref[pl.ds(pl.multiple_of(j * tk, tk), tk), :] = jnp.where(keep, 0.0, NEG).astype(BF16)
        return rank[tk - 1:tk, :]

    lax.fori_loop(0, n_live, bias_body, jnp.zeros((1, tq), F32))

    def fill_body(j, c):
        bias_ref[pl.ds(pl.multiple_of(j * tk, tk), tk), :] = jnp.full((tk, tq), NEG, BF16)
        return c

    lax.fori_loop(n_live, n_chunks, fill_body, 0)


def dsa_select(cast, cast_t, misc_t):
    t = cast.shape[0]
    tq = ATT_T
    w = IDX_HEADS * IDX_DIM
    return pl.pallas_call(
        _dsa_select_kernel,
        grid=(t // tq,),
        in_specs=[pl.BlockSpec((None, w, tq), lambda i: (i, COL_CQI // w, 0)),
                  pl.BlockSpec((t, w), lambda i: (0, COL_KI4 // w)),
                  pl.BlockSpec((None, LANES, tq), lambda i: (i, 0, 0))],
        out_specs=pl.BlockSpec((None, t, tq), lambda i: (i, 0, 0)),
        out_shape=jax.ShapeDtypeStruct((t // tq, t, tq), BF16),
        scratch_shapes=[pltpu.VMEM((t // ATT_T, ATT_T, tq), I32), pltpu.VMEM((t // ATT_T, ATT_T, tq), I16),
                        pltpu.VMEM((t // ATT_T, ATT_T, tq), I16)],
        compiler_params=_cparams(("arbitrary",), VMEM_LIMIT),
        name="dsa_select",
    )(cast_t, cast, misc_t)


def _dsa_attn_kernel(qt_ref, k_ref, vt_ref, bias_ref, o_ref, s_ref):
    i = pl.program_id(1)
    tq = qt_ref.shape[1]
    tk = ATT_T
    nh = s_ref.shape[1]
    qh = _head_rows(qt_ref[...])

    def score_fn(n):
        off = pl.multiple_of(n * tk, tk)
        kb = k_ref[pl.ds(off, tk), :]
        bias = bias_ref[pl.ds(off, tk), :].astype(F32)
        return [_dot(kb[:, _pair_cols(h)], qh[h]) + bias for h in range(nh)]

    def vt_fn(n, h):
        return vt_ref[n, h * HEAD_DIM:(h + 1) * HEAD_DIM, :]

    o_ref[...] = _attention_tile(i, score_fn, vt_fn, s_ref, tq, None).astype(BF16)


def dsa_attention(cast, cast_t, bias):
    t = cast.shape[0]
    tq = ATT_T
    w = ATT_HEADS * HEAD_DIM
    return pl.pallas_call(
        _dsa_attn_kernel,
        grid=(N_HEADS // ATT_HEADS, t // tq),
        in_specs=[pl.BlockSpec((None, w, tq), lambda g, i: (i, COL_CQ // w + g, 0)),
                  pl.BlockSpec((t, w), lambda g, i: (0, COL_CK // w + g)),
                  pl.BlockSpec((t // tq, w, tq), lambda g, i: (0, COL_CV // w + g, 0)),
                  pl.BlockSpec((None, t, tq), lambda g, i: (i, 0, 0))],
        out_specs=pl.BlockSpec((tq, w), lambda g, i: (i, g)),
        out_shape=jax.ShapeDtypeStruct((t, ATT_W), BF16),
        scratch_shapes=[pltpu.VMEM((2, ATT_HEADS, ATT_T, tq), F32)],
        compiler_params=_cparams(("arbitrary", "arbitrary"), VMEM_LIMIT),
        name="dsa_attention",
    )(cast_t, cast, cast_t, bias)


def _rms(x, w):
    return x * lax.rsqrt(jnp.mean(x * x, axis=-1, keepdims=True) + LN_EPS) * w


def _mla_prep_kernel(cq_ref, ckv_ref, misc_ref, qn_ref, kvn_ref, wq_ref, wk_ref, wv_ref,
                     c_ref, s1_ref, s2_ref, qt_ref, k_ref, vt_ref, misct_ref):
    c, s1, s2 = c_ref[...], s1_ref[...], s2_ref[...]
    cq = _rms(cq_ref[...], qn_ref[...]).astype(BF16)
    ckv = _rms(ckv_ref[...], kvn_ref[...]).astype(BF16)
    misc = misc_ref[...]
    misct_ref[...] = misc.T
    lane = lax.broadcasted_iota(I32, (1, LANES), 1)
    kpe = pltpu.roll(jnp.where(lane < MLA_ROPE, misc, 0.0), MLA_NOPE, 1)
    kpe = _rot_apply(kpe, c, s1, s2, MLA_ROPE // 2)
    q = _dot(cq, wq_ref[...])
    k = _dot(ckv, wk_ref[...])
    for h in range(N_HEADS):
        sl = slice(h * LANES, (h + 1) * LANES)
        qt_ref[sl, :] = _rot_apply(q[:, sl], c, s1, s2, MLA_ROPE // 2).T.astype(BF16)
        k_ref[:, sl] = (k[:, sl] + kpe).astype(BF16)
    v = _dot(ckv, wv_ref[...])
    for g in range(ATT_W // LANES):
        sl = slice(g * LANES, (g + 1) * LANES)
        vt_ref[sl, :] = v[:, sl].T.astype(BF16)


def mla_prep(proj, qn, kvn, wq, wk, wv, tabs):
    t = proj.shape[0]
    tm = ATT_T
    full = lambda a: pl.BlockSpec(a.shape, lambda i: (0,) * a.ndim)
    row = lambda w, col: pl.BlockSpec((tm, w), lambda i: (i, col // w))
    tile_t = lambda n: pl.BlockSpec((None, n, tm), lambda i: (i, 0, 0))
    return pl.pallas_call(
        _mla_prep_kernel,
        grid=(t // tm,),
        in_specs=[row(MLA_Q_RANK, COL_CQL), row(MLA_KV_RANK, COL_CKV), row(LANES, COL_MISC),
                  full(qn), full(kvn), full(wq), full(wk), full(wv),
                  row(LANES, 0), row(LANES, 0), row(LANES, 0)],
        out_specs=[tile_t(N_HEADS * LANES),
                   pl.BlockSpec((tm, N_HEADS * LANES), lambda i: (i, 0)),
                   tile_t(ATT_W), tile_t(LANES)],
        out_shape=[jax.ShapeDtypeStruct((t // tm, N_HEADS * LANES, tm), BF16),
                   jax.ShapeDtypeStruct((t, N_HEADS * LANES), BF16),
                   jax.ShapeDtypeStruct((t // tm, ATT_W, tm), BF16),
                   jax.ShapeDtypeStruct((t // tm, LANES, tm), F32)],
        compiler_params=_cparams(("arbitrary",), VMEM_LIMIT),
        name="mla_prep",
    )(proj, proj, proj, qn, kvn, wq, wk, wv, *tabs)


def _mla_attn_kernel(qt_ref, k_ref, vt_ref, o_ref, s_ref):
    i = pl.program_id(1)
    tq = qt_ref.shape[1]
    tk = ATT_T
    nh = s_ref.shape[1]
    qh = [qt_ref[h * LANES:(h + 1) * LANES, :] for h in range(nh)]

    def score_fn(n):
        off = pl.multiple_of(n * tk, tk)
        return [_dot(k_ref[pl.ds(off, tk), h * LANES:(h + 1) * LANES], qh[h]) for h in range(nh)]

    def vt_fn(n, h):
        return vt_ref[n, h * HEAD_DIM:(h + 1) * HEAD_DIM, :]

    o_ref[...] = _attention_tile(i, score_fn, vt_fn, s_ref, tq, _causal_t(tk, tq)).astype(BF16)


def mla_attention(qt, k, vt):
    t = k.shape[0]
    tq = ATT_T
    wqk = ATT_HEADS * LANES
    wv = ATT_HEADS * HEAD_DIM
    return pl.pallas_call(
        _mla_attn_kernel,
        grid=(N_HEADS // ATT_HEADS, t // tq),
        in_specs=[pl.BlockSpec((None, wqk, tq), lambda g, i: (i, g, 0)),
                  pl.BlockSpec((t, wqk), lambda g, i: (0, g)),
                  pl.BlockSpec((t // tq, wv, tq), lambda g, i: (0, g, 0))],
        out_specs=pl.BlockSpec((tq, wv), lambda g, i: (i, g)),
        out_shape=jax.ShapeDtypeStruct((t, ATT_W), BF16),
        scratch_shapes=[pltpu.VMEM((2, ATT_HEADS, ATT_T, tq), F32)],
        compiler_params=_cparams(("arbitrary", "arbitrary"), VMEM_LIMIT),
        name="mla_attention",
    )(qt, k, vt)


def _silu(x):
    return x / (1.0 + jnp.exp(-x))


def _softplus(x):
    return jnp.maximum(x, 0.0) + jnp.log1p(jnp.exp(-jnp.abs(x)))


def _ssd_kernel(z_ref, xbc_ref, misc_ref, cw_ref, cb_ref, dtb_ref, a_ref, dsk_ref, nw_ref, exp_ref,
                o_ref, ext_ref, state_ref):
    c = pl.program_id(0)
    L = SSM_CHUNK
    half = SSM_INNER // SSM_GROUPS

    @pl.when(c == 0)
    def _():
        ext_ref[0:8, :] = jnp.zeros((8, SSM_CONV_CH), F32)
        state_ref[...] = jnp.zeros(state_ref.shape, F32)

    ext_ref[8:8 + L, :] = xbc_ref[...]
    conv = jnp.zeros((L, SSM_CONV_CH), F32) + cb_ref[...]
    for kk in range(SSM_CONV):
        conv = conv + cw_ref[kk:kk + 1, :] * ext_ref[pl.ds(8 - (SSM_CONV - 1) + kk, L), :]
    tail = ext_ref[L:L + 8, :]
    ext_ref[0:8, :] = tail
    xbc = _silu(conv)
    xs = xbc[:, 0:SSM_INNER]
    bm = xbc[:, SSM_INNER:SSM_INNER + SSM_GROUPS * SSM_STATE]
    cm = xbc[:, SSM_INNER + SSM_GROUPS * SSM_STATE:]

    dt = _softplus(misc_ref[...] + dtb_ref[...])
    da = dt * a_ref[...]
    r = lax.broadcasted_iota(I32, (L, L), 0)
    cc = lax.broadcasted_iota(I32, (L, L), 1)
    tri = r >= cc
    a_cum = _dot_hi(jnp.where(tri, 1.0, 0.0), da)
    a_cum_t = a_cum.T
    expand = exp_ref[...]
    dt_full = _dot_hi(dt, expand)
    acum_full = _dot_hi(a_cum, expand)
    a_last = acum_full[L - 1:L, :]
    xdt = xs * dt_full
    xdt_b = xdt.astype(BF16)

    ys = []
    for g in range(SSM_GROUPS):
        bg = bm[:, g * SSM_STATE:(g + 1) * SSM_STATE].astype(BF16)
        cg = cm[:, g * SSM_STATE:(g + 1) * SSM_STATE].astype(BF16)
        cb = _dot_nt(cg, bg)
        hpg = SSM_HEADS // SSM_GROUPS
        cols = []
        for hh in range(hpg):
            h = g * hpg + hh
            diff = a_cum[:, MISC_DT + h:MISC_DT + h + 1] - a_cum_t[MISC_DT + h:MISC_DT + h + 1, :]
            seg = jnp.where(tri, jnp.exp(jnp.where(tri, diff, 0.0)), 0.0)
            sc = (cb * seg).astype(BF16)
            pair = (h // 2) * LANES
            yp = _dot(sc, xdt_b[:, pair:pair + LANES])
            cols.append(yp[:, (h % 2) * SSM_HEAD_DIM:(h % 2 + 1) * SSM_HEAD_DIM])
        y_diag = jnp.concatenate(cols, axis=1)
        gs = slice(g * half, (g + 1) * half)
        st = state_ref[:, gs]
        y_off = _dot(cg, st.astype(BF16)) * jnp.exp(acum_full[:, gs])
        ys.append(y_diag + y_off)
        xw = (xdt[:, gs] * jnp.exp(a_last[:, gs] - acum_full[:, gs])).astype(BF16)
        new = _dot(bm[:, g * SSM_STATE:(g + 1) * SSM_STATE].T.astype(BF16), xw)
        state_ref[:, gs] = st * jnp.exp(a_last[:, gs]) + new
    y = jnp.concatenate(ys, axis=1) + xs * dsk_ref[...]
    y = y * _silu(z_ref[...])
    o_ref[...] = _rms(y, nw_ref[...]).astype(BF16)


def ssd_mixer(proj, conv_w, conv_b, dtb_row, a_row, dsk_row, norm_w, expand):
    t = proj.shape[0]
    L = SSM_CHUNK
    full = lambda a: pl.BlockSpec(a.shape, lambda c: (0,) * a.ndim)
    return pl.pallas_call(
        _ssd_kernel,
        grid=(t // L,),
        in_specs=[pl.BlockSpec((L, SSM_INNER), lambda c: (c, COL_MZ // SSM_INNER)),
                  pl.BlockSpec((L, SSM_CONV_CH), lambda c: (c, COL_XBC // SSM_CONV_CH)),
                  pl.BlockSpec((L, LANES), lambda c: (c, COL_MISC // LANES)),
                  full(conv_w), full(conv_b), full(dtb_row), full(a_row), full(dsk_row), full(norm_w),
                  full(expand)],
        out_specs=pl.BlockSpec((L, SSM_INNER), lambda c: (c, 0)),
        out_shape=jax.ShapeDtypeStruct((t, SSM_INNER), BF16),
        scratch_shapes=[pltpu.VMEM((L + 8, SSM_CONV_CH), F32),
                        pltpu.VMEM((SSM_STATE, SSM_INNER), F32)],
        compiler_params=_cparams(("arbitrary",), VMEM_LIMIT),
        name="ssd_mixer",
    )(proj, proj, proj, conv_w, conv_b, dtb_row, a_row, dsk_row, norm_w, expand)


def _merge_kernel(x_ref, wg0_ref, wg1_ref, wg2_ref, wg3_ref, bg_ref, oa_ref, ob_ref, oc_ref, od_ref, wb_ref, o_ref):
    x = x_ref[...]
    acc = None
    wg_refs = (wg0_ref, wg1_ref, wg2_ref, wg3_ref)
    for g, br in enumerate((oa_ref, ob_ref, oc_ref, od_ref)):
        logit = _dot(x, wg_refs[g][...]) + bg_ref[g:g + 1, :]
        gate = 1.0 / (1.0 + jnp.exp(-logit))
        term = gate * _dot(br[...], wb_ref[g])
        acc = term if acc is None else acc + term
    o_ref[...] = acc.astype(BF16)


def gated_merge(x_bf, wg, bg, branches, wb, tm=1024, tn=256):
    t = x_bf.shape[0]
    br_spec = pl.BlockSpec((tm, BRANCH_W), lambda j, i: (i, 0))
    nj = D_MODEL // tn
    wg_specs = [pl.BlockSpec((D_MODEL, tn), functools.partial(lambda j, i, g: (0, g * nj + j), g=g))
                for g in range(N_BRANCH)]
    return pl.pallas_call(
        _merge_kernel,
        grid=(nj, t // tm),
        in_specs=[pl.BlockSpec((tm, D_MODEL), lambda j, i: (i, 0)),
                  *wg_specs,
                  pl.BlockSpec((N_BRANCH, tn), lambda j, i: (0, j)),
                  br_spec, br_spec, br_spec, br_spec,
                  pl.BlockSpec((N_BRANCH, BRANCH_W, tn), lambda j, i: (0, 0, j))],
        out_specs=pl.BlockSpec((tm, tn), lambda j, i: (i, j)),
        out_shape=jax.ShapeDtypeStruct((t, D_MODEL), BF16),
        compiler_params=_cparams(("arbitrary", "arbitrary"), VMEM_LIMIT),
        name="gated_merge",
    )(x_bf, wg, wg, wg, wg, bg, *branches, wb)


def _layer_norm(v, g, b):
    mu = jnp.mean(v, axis=-1, keepdims=True)
    d = v - mu
    var = jnp.mean(d * d, axis=-1, keepdims=True)
    return d * lax.rsqrt(var + LN_EPS) * g + b


def _outproj_kernel(m_ref, w_ref, x_ref, g_ref, b_ref, rw_ref, rb_ref, x1_ref, x1b_ref, idx_ref, gate_ref):
    mixed = _dot(m_ref[...], w_ref[...])
    x1 = _layer_norm(DN_ALPHA * x_ref[...] + mixed, g_ref[...], b_ref[...])
    x1_ref[...] = x1
    x1b = x1.astype(BF16)
    x1b_ref[...] = x1b
    logits = _dot(x1b, rw_ref[...]) + rb_ref[...]
    lane = lax.broadcasted_iota(I32, logits.shape, 1)
    g = logits
    idx_out = jnp.zeros(logits.shape, I32)
    val_out = jnp.zeros(logits.shape, F32)
    vals = []
    for kk in range(TOP_K):
        m = jnp.max(g, axis=1, keepdims=True)
        idx = jnp.min(jnp.where(g == m, lane, LANES), axis=1, keepdims=True)
        g = jnp.where(lane == idx, -jnp.inf, g)
        idx_out = jnp.where(lane == kk, idx, idx_out)
        vals.append(m)
    es = [jnp.exp(v - vals[0]) for v in vals]
    den = es[0] + es[1] + es[2] + es[3]
    for kk in range(TOP_K):
        val_out = jnp.where(lane == kk, es[kk] / den, val_out)
    idx_ref[...] = idx_out
    gate_ref[...] = val_out


def outproj_ln_router(merged, w_out, x, g, b, rw, rb, tm=256):
    t = x.shape[0]
    full = lambda a: pl.BlockSpec(a.shape, lambda i: (0,) * a.ndim)
    row = lambda w: pl.BlockSpec((tm, w), lambda i: (i, 0))
    return pl.pallas_call(
        _outproj_kernel,
        grid=(t // tm,),
        in_specs=[row(D_MODEL), full(w_out), row(D_MODEL), full(g), full(b), full(rw), full(rb)],
        out_specs=[row(D_MODEL), row(D_MODEL), row(LANES), row(LANES)],
        out_shape=[jax.ShapeDtypeStruct((t, D_MODEL), F32), jax.ShapeDtypeStruct((t, D_MODEL), BF16),
                   jax.ShapeDtypeStruct((t, LANES), I32), jax.ShapeDtypeStruct((t, LANES), F32)],
        compiler_params=_cparams(("arbitrary",), VMEM_LIMIT),
        name="outproj_ln_router",
    )(merged, w_out, x, g, b, rw, rb)


def _start_row_gather(idx_ref, src_hbm, dst, sem, rows):
    for r in range(rows):
        pltpu.make_async_copy(src_hbm.at[pl.ds(idx_ref[0, r], 1)], dst.at[pl.ds(r, 1)], sem).start(priority=r % 2)


def _wait_row_gather(src_hbm, dst, sem, rows):
    pltpu.make_async_copy(src_hbm.at[pl.ds(0, rows)], dst, sem).wait()


def _expert_kernel(be_ref, nu_ref, tok_ref, tokn_ref, x_hbm, wgu_ref, bgu_ref, wd_ref, bd_ref, o_ref, xbuf, sem):
    blk = pl.program_id(0)
    n_used = nu_ref[0]
    slot = lax.rem(blk, 2)

    @pl.when(jnp.logical_and(blk == 0, n_used > 0))
    def _():
        _start_row_gather(tok_ref, x_hbm, xbuf.at[0], sem.at[0], MOE_ROWS)

    @pl.when(blk + 1 < n_used)
    def _():
        _start_row_gather(tokn_ref, x_hbm, xbuf.at[1 - slot], sem.at[1 - slot], MOE_ROWS)

    @pl.when(blk < n_used)
    def _():
        _wait_row_gather(x_hbm, xbuf.at[slot], sem.at[slot], MOE_ROWS)
        gu = _dot(xbuf[slot].astype(BF16), wgu_ref[0].astype(BF16)) + bgu_ref[0]
        glu = jnp.minimum(gu[:, :D_FF], SWIGLU_LIMIT)
        lin = jnp.clip(gu[:, D_FF:], -SWIGLU_LIMIT, SWIGLU_LIMIT)
        act = glu / (1.0 + jnp.exp(-SWIGLU_ALPHA * glu)) * (lin + 1.0)
        o_ref[...] = _dot(act.astype(BF16), wd_ref[0].astype(BF16)) + bd_ref[0]

    @pl.when(blk >= nu_ref[0])
    def _():
        o_ref[...] = jnp.zeros(o_ref.shape, F32)


def expert_ffn(block_e, n_used, row_tok, x, layer, wgu, bgu, wd, bd):
    n_blocks = row_tok.shape[0]
    n_rows = n_blocks * MOE_ROWS
    grid_spec = pltpu.PrefetchScalarGridSpec(
        num_scalar_prefetch=2,
        grid=(n_blocks,),
        in_specs=[pl.BlockSpec((None, 1, MOE_ROWS), lambda b, be, nu: (b, 0, 0), memory_space=pltpu.SMEM),
                  pl.BlockSpec((None, 1, MOE_ROWS), lambda b, be, nu: (jnp.minimum(b + 1, n_blocks - 1), 0, 0),
                               memory_space=pltpu.SMEM),
                  pl.BlockSpec(memory_space=pl.ANY),
                  pl.BlockSpec((None, 1, D_MODEL, 2 * D_FF), lambda b, be, nu: (layer, be[b], 0, 0)),
                  pl.BlockSpec((1, 1, 2 * D_FF), lambda b, be, nu: (be[b], 0, 0)),
                  pl.BlockSpec((None, 1, D_FF, D_MODEL), lambda b, be, nu: (layer, be[b], 0, 0)),
                  pl.BlockSpec((1, 1, D_MODEL), lambda b, be, nu: (be[b], 0, 0))],
        out_specs=pl.BlockSpec((MOE_ROWS, D_MODEL), lambda b, be, nu: (b, 0)),
        scratch_shapes=[pltpu.VMEM((2, MOE_ROWS, D_MODEL), F32), pltpu.SemaphoreType.DMA((2,))],
    )
    return pl.pallas_call(
        _expert_kernel,
        grid_spec=grid_spec,
        out_shape=jax.ShapeDtypeStruct((n_rows, D_MODEL), F32),
        compiler_params=_cparams(("arbitrary",), VMEM_LIMIT),
        name="expert_ffn",
    )(block_e, n_used, row_tok, row_tok, x, wgu, bgu, wd, bd)


COMBINE_TM = 128


def _combine_kernel(pos_ref, posn_ref, out_hbm, gate_ref, x_ref, g_ref, b_ref, o_ref, ob_ref, rbuf, sem):
    i = pl.program_id(0)
    tm = x_ref.shape[0]
    rows = TOP_K * tm
    slot = lax.rem(i, 2)

    @pl.when(i == 0)
    def _():
        _start_row_gather(pos_ref, out_hbm, rbuf.at[0], sem.at[0], rows)

    @pl.when(i + 1 < pl.num_programs(0))
    def _():
        _start_row_gather(posn_ref, out_hbm, rbuf.at[1 - slot], sem.at[1 - slot], rows)

    _wait_row_gather(out_hbm, rbuf.at[slot], sem.at[slot], rows)
    gate = gate_ref[...]
    y = jnp.zeros(x_ref.shape, F32)
    for kk in range(TOP_K):
        y = y + gate[:, kk:kk + 1] * rbuf[slot, kk * tm:(kk + 1) * tm, :]
    x2 = _layer_norm(DN_ALPHA * x_ref[...] + y, g_ref[...], b_ref[...])
    o_ref[...] = x2
    ob_ref[...] = x2.astype(BF16)


def combine_ln(pos_km, out, gates, x1, g, b):
    t = x1.shape[0]
    tm = COMBINE_TM
    n_tiles = t // tm
    full = lambda a: pl.BlockSpec(a.shape, lambda i: (0,) * a.ndim)
    row = lambda w: pl.BlockSpec((tm, w), lambda i: (i, 0))
    return pl.pallas_call(
        _combine_kernel,
        grid=(n_tiles,),
        in_specs=[pl.BlockSpec((None, 1, TOP_K * tm), lambda i: (i, 0, 0), memory_space=pltpu.SMEM),
                  pl.BlockSpec((None, 1, TOP_K * tm), lambda i: (jnp.minimum(i + 1, n_tiles - 1), 0, 0),
                               memory_space=pltpu.SMEM),
                  pl.BlockSpec(memory_space=pl.ANY), row(LANES), row(D_MODEL), full(g), full(b)],
        out_specs=[row(D_MODEL), row(D_MODEL)],
        out_shape=[jax.ShapeDtypeStruct((t, D_MODEL), F32), jax.ShapeDtypeStruct((t, D_MODEL), BF16)],
        scratch_shapes=[pltpu.VMEM((2, TOP_K * tm, D_MODEL), F32), pltpu.SemaphoreType.DMA((2,))],
        compiler_params=_cparams(("arbitrary",), VMEM_LIMIT),
        name="combine_ln",
    )(pos_km, pos_km, out, gates, x1, g, b)


def _rope_tables(positions):
    pos = positions.reshape(-1).astype(F32)

    def cs(dim, theta):
        inv = theta ** (-jnp.arange(0, dim, 2, dtype=F32) / dim)
        ang = pos[:, None] * inv
        return jnp.cos(ang), jnp.sin(ang)

    t = pos.shape[0]
    cos_p, sin_p = cs(ROT_DIM, ROPE_THETA)
    half = ROT_DIM // 2
    one = jnp.ones((t, HEAD_DIM - ROT_DIM), F32)
    zero = jnp.zeros((t, HEAD_DIM - ROT_DIM), F32)
    zh = jnp.zeros((t, half), F32)
    c64 = jnp.concatenate([cos_p, cos_p, one], axis=1)
    s1_64 = jnp.concatenate([-sin_p, zh, zero], axis=1)
    s2_64 = jnp.concatenate([zh, sin_p, zero], axis=1)
    tabs_p = tuple(jnp.tile(a, (1, 2)) for a in (c64, s1_64, s2_64))

    cos_m, sin_m = cs(MLA_ROPE, MLA_THETA)
    hm = MLA_ROPE // 2
    pre1 = jnp.ones((t, MLA_NOPE), F32)
    pre0 = jnp.zeros((t, MLA_NOPE), F32)
    post1 = jnp.ones((t, LANES - MLA_NOPE - MLA_ROPE), F32)
    post0 = jnp.zeros((t, LANES - MLA_NOPE - MLA_ROPE), F32)
    zm = jnp.zeros((t, hm), F32)
    c_m = jnp.concatenate([pre1, cos_m, cos_m, post1], axis=1)
    s1_m = jnp.concatenate([pre0, -sin_m, zm, post0], axis=1)
    s2_m = jnp.concatenate([pre0, zm, sin_m, post0], axis=1)
    return tabs_p, (c_m, s1_m, s2_m)


def _prep_in_weights(w_in):
    pts = np.cumsum((0,) + SPLIT_SIZES)
    seg = lambda k: w_in[:, pts[k]:pts[k + 1]]
    (a_q, a_k, a_v, m_z, m_xbc, m_dt, c_q, c_k, c_v, c_qi, c_ki, c_wi, d_cq, d_ckv, d_kpe) = [seg(k) for k in range(15)]
    scale = HEAD_DIM ** -0.5 * LOG2E
    zeros = lambda n: jnp.zeros((D_MODEL, n), w_in.dtype)
    attn = jnp.concatenate([a_q * scale, a_k, a_v, c_q * scale, c_k, c_v, c_qi, c_ki, c_ki, c_ki, c_ki], axis=1)
    misc_pad = zeros(LANES - MLA_ROPE - SSM_HEADS - IDX_HEADS)
    rest = jnp.concatenate([m_xbc, m_z, d_cq, d_kpe, m_dt, c_wi * IDX_SCALE, misc_pad, d_ckv], axis=1)
    assert attn.shape[1] == N_CAST and rest.shape[1] == N_REST
    return attn.astype(BF16), rest.astype(BF16)


def _prep_mla_weights(w_uq, w_ukv):
    dq = MLA_NOPE + MLA_ROPE
    wq = w_uq.reshape(MLA_Q_RANK, N_HEADS, dq) * (dq ** -0.5 * LOG2E)
    wq = jnp.pad(wq, ((0, 0), (0, 0), (0, LANES - dq))).reshape(MLA_Q_RANK, N_HEADS * LANES)
    wkv = w_ukv.reshape(MLA_KV_RANK, N_HEADS, MLA_NOPE + MLA_V)
    wk = jnp.pad(wkv[:, :, :MLA_NOPE], ((0, 0), (0, 0), (0, LANES - MLA_NOPE))).reshape(MLA_KV_RANK, N_HEADS * LANES)
    wv = wkv[:, :, MLA_NOPE:].reshape(MLA_KV_RANK, N_HEADS * MLA_V)
    return wq.astype(BF16), wk.astype(BF16), wv.astype(BF16)


def _misc_row(v):
    return jnp.zeros((1, LANES), F32).at[0, MISC_DT:MISC_DT + SSM_HEADS].set(v.astype(F32))


def _route(top_idx, t):
    tk = t * TOP_K
    flat_e = top_idx.reshape(-1)
    onehot = (flat_e[:, None] == jnp.arange(N_EXPERTS)[None, :]).astype(I32)
    seen = jnp.cumsum(onehot, axis=0)
    counts = seen[-1]
    padded = ((counts + MOE_ROWS - 1) // MOE_ROWS) * MOE_ROWS
    pad_end = jnp.cumsum(padded)
    pad_start = pad_end - padded
    rank = jnp.sum(onehot * seen, axis=1) - 1
    pos = (pad_start[flat_e] + rank).astype(I32)
    n_blocks = tk // MOE_ROWS + N_EXPERTS
    n_rows = n_blocks * MOE_ROWS
    row_tok = jnp.zeros((n_rows,), I32).at[pos].set(jnp.arange(tk, dtype=I32) // TOP_K)
    block_start = jnp.arange(n_blocks) * MOE_ROWS
    block_e = jnp.minimum(jnp.sum(pad_end[None, :] <= block_start[:, None], axis=1), N_EXPERTS - 1).astype(I32)
    n_used = (pad_end[-1] // MOE_ROWS).astype(I32).reshape(1)
    return row_tok, pos, block_e, n_used


def _layer(x, x_bf, tabs_p, tabs_m, flags, expand, p, layer, big):
    t = x.shape[0]
    w_attn, w_rest = _prep_in_weights(p["w_in"])
    cast, cast_t = proj_attention(x_bf, w_attn, flags, tabs_p)
    proj = matmul_bf16(x_bf, w_rest, tn=N_REST // 2)

    o_a = moba_attention(cast, cast_t)
    o_b = ssd_mixer(proj, p["conv_w"], p["conv_b"].reshape(1, -1), _misc_row(p["dt_bias"]),
                    _misc_row(-jnp.exp(p["a_log"].astype(F32))),
                    jnp.repeat(p["d_skip"].astype(F32), SSM_HEAD_DIM).reshape(1, -1),
                    p["ssm_norm_w"].reshape(1, -1), expand)
    wq, wk, wv = _prep_mla_weights(p["w_uq"], p["w_ukv"])
    qt_m, k_m, vt_m, misc_t = mla_prep(proj, p["q_norm_w"].reshape(1, -1), p["kv_norm_w"].reshape(1, -1), wq, wk, wv,
                                       tabs_m)
    o_c = dsa_attention(cast, cast_t, dsa_select(cast, cast_t, misc_t))
    o_d = mla_attention(qt_m, k_m, vt_m)

    wg = p["w_in"][:, N_GATE0:].astype(BF16)
    merged = gated_merge(x_bf, wg, p["b_gate"], (o_a, o_b, o_c, o_d), p["w_branch"].astype(BF16))
    rw = jnp.pad(p["router_w"], ((0, 0), (0, LANES - N_EXPERTS))).astype(BF16)
    rb = jnp.concatenate([p["router_b"].astype(F32), jnp.full((LANES - N_EXPERTS,), -jnp.inf, F32)]).reshape(1, -1)
    x1, x1_bf, top_idx, gates = outproj_ln_router(merged, p["w_out"].astype(BF16), x, p["ln1_g"].reshape(1, -1),
                                                  p["ln1_b"].reshape(1, -1), rw, rb)

    row_tok, pos, block_e, n_used = _route(top_idx[:, :TOP_K], t)
    out = expert_ffn(block_e, n_used, row_tok.reshape(-1, 1, MOE_ROWS), x1, layer, big["w_gate_up"],
                     p["b_gate_up"][:, None, :], big["w_down"], p["b_down"][:, None, :])
    pos_km = pos.reshape(t // COMBINE_TM, COMBINE_TM, TOP_K).transpose(0, 2, 1).reshape(-1, 1, TOP_K * COMBINE_TM)
    return combine_ln(pos_km, out, gates, x1, p["ln2_g"].reshape(1, -1), p["ln2_b"].reshape(1, -1))


def kernel(x, positions, w_in, b_gate, conv_w, conv_b, dt_bias, a_log, d_skip, ssm_norm_w, q_norm_w, w_uq,
           kv_norm_w, w_ukv, w_branch, w_out, ln1_g, ln1_b, router_w, router_b, w_gate_up, b_gate_up, w_down,
           b_down, ln2_g, ln2_b):
    b, s, d = x.shape
    assert b == 1 and d == D_MODEL
    names = ("w_in", "b_gate", "conv_w", "conv_b", "dt_bias", "a_log", "d_skip", "ssm_norm_w", "q_norm_w", "w_uq",
             "kv_norm_w", "w_ukv", "w_branch", "w_out", "ln1_g", "ln1_b", "router_w", "router_b", "w_gate_up",
             "b_gate_up", "w_down", "b_down", "ln2_g", "ln2_b")
    stacked = (w_in, b_gate, conv_w, conv_b, dt_bias, a_log, d_skip, ssm_norm_w, q_norm_w, w_uq, kv_norm_w, w_ukv,
               w_branch, w_out, ln1_g, ln1_b, router_w, router_b, w_gate_up, b_gate_up, w_down, b_down, ln2_g, ln2_b)
    tabs_p, tabs_m = _rope_tables(positions)
    flags = jnp.array([1, 1, 0, 1, 1, 0, 1], I32)
    head_of_lane = jnp.arange(SSM_INNER) // SSM_HEAD_DIM
    expand = (jnp.arange(LANES)[:, None] == (MISC_DT + head_of_lane)[None, :]).astype(F32)

    big = {"w_gate_up": w_gate_up, "w_down": w_down}
    xt = x.reshape(s, d)
    xt_bf = xt.astype(BF16)
    for l in range(DEPTH):
        p = {n: a[l] for n, a in zip(names, stacked)}
        xt, xt_bf = _layer(xt, xt_bf, tabs_p, tabs_m, flags, expand, p, l, big)
    return xt.reshape(b, s, d)
```

```python
import functools
import math

import jax
import jax.numpy as jnp
import numpy as np
from jax import lax
from jax.experimental import pallas as pl
from jax.experimental.pallas import tpu as pltpu

F32 = jnp.float32
BF16 = jnp.bfloat16
I32 = jnp.int32
I16 = jnp.int16

D_MODEL = 2048
DEPTH = 4
HEAD_DIM = 64
ROT_DIM = HEAD_DIM // 4
ROPE_THETA = 500000.0
N_HEADS = 8
ATT_W = N_HEADS * HEAD_DIM
MOBA_BLOCK = 256
MOBA_TOPK = 3
SSM_HEADS = 8
SSM_HEAD_DIM = 64
SSM_INNER = 512
SSM_STATE = 128
SSM_GROUPS = 2
SSM_CONV = 4
SSM_CONV_CH = 1024
SSM_CHUNK = 256
IDX_HEADS = 4
IDX_DIM = 64
DSA_TOPK = 256
IDX_SCALE = (IDX_HEADS * IDX_DIM) ** -0.5
MLA_Q_RANK = 384
MLA_KV_RANK = 256
MLA_NOPE = 64
MLA_ROPE = 32
MLA_V = 64
MLA_THETA = 10000.0
N_BRANCH = 4
BRANCH_W = 512
SPLIT_SIZES = (512, 512, 512, 512, 1024, 8, 512, 512, 512, 256, 64, 4, 384, 256, 32, N_BRANCH * D_MODEL)
N_EXPERTS = 32
TOP_K = 4
D_FF = 768
SWIGLU_ALPHA = 1.702
SWIGLU_LIMIT = 7.0
DN_ALPHA = (2 * DEPTH) ** 0.25
LN_EPS = 1e-5

LANES = 128
NEG = -1e30
VMEM_LIMIT = 56 * 1024 * 1024

COL_AQ, COL_AK, COL_AV = 0, 512, 1024
COL_CQ, COL_CK, COL_CV = 1536, 2048, 2560
COL_CQI, COL_KI4 = 3072, 3328
N_CAST = 3584
COL_XBC = 0
COL_MZ = 1024
COL_CQL = 1536
COL_MISC = 1920
COL_CKV = 2048
N_REST = 2304
N_GATE0 = sum(SPLIT_SIZES[:-1])
MISC_DT = 32
MISC_WI = 40
MOE_ROWS = 256


def _cparams(sem, vmem=None):
    return pltpu.CompilerParams(dimension_semantics=sem, vmem_limit_bytes=vmem)


def _dot(a, b):
    return jnp.dot(a, b, preferred_element_type=F32)


def _dot_nt(a, b):
    return lax.dot_general(a, b, (((1,), (1,)), ((), ())), preferred_element_type=F32)


def _dot_hi(a, b):
    return jnp.dot(a, b, preferred_element_type=F32, precision=lax.Precision.HIGHEST)


def _mm_kernel(x_ref, w_ref, o_ref):
    o_ref[...] = _dot(x_ref[...], w_ref[...])


def matmul_bf16(x, w, tm=512, tn=1024):
    m, k = x.shape
    n = w.shape[1]
    return pl.pallas_call(
        _mm_kernel,
        grid=(n // tn, m // tm),
        in_specs=[pl.BlockSpec((tm, k), lambda j, i: (i, 0)),
                  pl.BlockSpec((k, tn), lambda j, i: (0, j))],
        out_specs=pl.BlockSpec((tm, tn), lambda j, i: (i, j)),
        out_shape=jax.ShapeDtypeStruct((m, n), F32),
        compiler_params=_cparams(("arbitrary", "arbitrary"), VMEM_LIMIT),
        name="in_proj",
    )(x, w)


def _rot_apply(x, c, s1, s2, shift):
    return x * c + pltpu.roll(x, LANES - shift, 1) * s1 + pltpu.roll(x, shift, 1) * s2


def _proj_attn_kernel(flag_ref, x_ref, w_ref, c_ref, s1_ref, s2_ref, o_ref, ot_ref):
    j = pl.program_id(0)
    tm, width = o_ref.shape
    tile = ot_ref.shape[2]
    acc = _dot(x_ref[...], w_ref[...])

    def emit(g, y):
        sl = slice(g * LANES, (g + 1) * LANES)
        o_ref[:, sl] = y.astype(BF16)
        for s in range(tm // tile):
            ot_ref[s, sl, :] = y[s * tile:(s + 1) * tile, :].T.astype(BF16)

    @pl.when(flag_ref[j] == 1)
    def _():
        c, s1, s2 = c_ref[...], s1_ref[...], s2_ref[...]
        for g in range(width // LANES):
            emit(g, _rot_apply(acc[:, g * LANES:(g + 1) * LANES], c, s1, s2, ROT_DIM // 2))

    @pl.when(flag_ref[j] == 0)
    def _():
        for g in range(width // LANES):
            emit(g, acc[:, g * LANES:(g + 1) * LANES])


def proj_attention(x_bf, w, flags, tabs, tm=512, tw=512):
    t, k = x_bf.shape
    sub = tm // ATT_T
    grid_spec = pltpu.PrefetchScalarGridSpec(
        num_scalar_prefetch=1,
        grid=(N_CAST // tw, t // tm),
        in_specs=[pl.BlockSpec((tm, k), lambda j, i, f: (i, 0)),
                  pl.BlockSpec((k, tw), lambda j, i, f: (0, j)),
                  pl.BlockSpec((tm, LANES), lambda j, i, f: (i, 0)),
                  pl.BlockSpec((tm, LANES), lambda j, i, f: (i, 0)),
                  pl.BlockSpec((tm, LANES), lambda j, i, f: (i, 0))],
        out_specs=[pl.BlockSpec((tm, tw), lambda j, i, f: (i, j)),
                   pl.BlockSpec((sub, tw, ATT_T), lambda j, i, f: (i, j, 0))],
    )
    return pl.pallas_call(
        _proj_attn_kernel,
        grid_spec=grid_spec,
        out_shape=[jax.ShapeDtypeStruct((t, N_CAST), BF16),
                   jax.ShapeDtypeStruct((t // ATT_T, N_CAST, ATT_T), BF16)],
        compiler_params=_cparams(("arbitrary", "arbitrary"), VMEM_LIMIT),
        name="proj_attention",
    )(flags, x_bf, w, *tabs)


ATT_T = 256
ATT_HEADS = 4


LOG2E = 1.4426950408889634
ACC_ROWS = HEAD_DIM + 16


def _softmax_step(s, smax, vt, state, keep=None):
    m, acc = state
    if keep is not None:
        smax = jnp.where(keep, smax, NEG)
    m_new = jnp.maximum(m, smax)
    shift = m_new if keep is None else jnp.where(keep, m_new, -NEG)
    p = jnp.exp2(s - shift).astype(BF16)
    vt_ext = jnp.concatenate([vt, jnp.ones((ACC_ROWS - HEAD_DIM, vt.shape[1]), vt.dtype)], axis=0)
    acc_new = jnp.exp2(m - m_new) * acc + _dot(vt_ext, p)
    return m_new, acc_new


def _init_state(tq):
    return (jnp.full((1, tq), NEG, F32), jnp.zeros((ACC_ROWS, tq), F32))


def _causal_t(tk, tq):
    return lax.broadcasted_iota(I32, (tk, tq), 0) <= lax.broadcasted_iota(I32, (tk, tq), 1)


def _head_rows(qt):
    row = lax.broadcasted_iota(I32, (LANES, 1), 0)
    out = []
    for p in range(qt.shape[0] // LANES):
        blk = qt[p * LANES:(p + 1) * LANES, :]
        zero = jnp.zeros_like(blk)
        out += [jnp.where(row < HEAD_DIM, blk, zero), jnp.where(row >= HEAD_DIM, blk, zero)]
    return out


def _pair_cols(h):
    return slice((h // 2) * LANES, (h // 2 + 1) * LANES)


def _attention_tile(i, score_fn, vt_fn, s_ref, tq, last_mask, keep_fn=None):
    nh = s_ref.shape[1]

    def park(scores, buf):
        for h in range(nh):
            s_ref[buf, h] = scores[h]
        return [jnp.max(sc, axis=0, keepdims=True) for sc in scores]

    def step(n, carry, buf, nxt_buf):
        nxt = score_fn(n + 1)
        states = []
        for h in range(nh):
            smax, m, acc = carry[3 * h:3 * h + 3]
            keep = None if keep_fn is None else keep_fn(n, h)
            states.append(_softmax_step(s_ref[buf, h], smax, vt_fn(n, h), (m, acc), keep))
        smax_next = park(nxt, nxt_buf)
        out = []
        for h in range(nh):
            out.extend((smax_next[h],) + tuple(states[h]))
        return tuple(out)

    def pair(mm, carry):
        return step(2 * mm + 1, step(2 * mm, carry, 0, 1), 1, 0)

    init = []
    for smax in park(score_fn(0), 0):
        init.extend((smax,) + _init_state(tq))
    carry = lax.fori_loop(0, i // 2, pair, tuple(init))
    carry = lax.cond(i % 2 == 1, lambda c: step(i - 1, c, 0, 0), lambda c: c, carry)
    fin = []
    for h in range(nh):
        smax, m, acc = carry[3 * h:3 * h + 3]
        s = s_ref[0, h]
        if last_mask is not None:
            s = jnp.where(last_mask, s, NEG)
            smax = jnp.max(s, axis=0, keepdims=True)
        _, acc = _softmax_step(s, smax, vt_fn(i, h), (m, acc))
        fin.append(acc[0:HEAD_DIM] / acc[HEAD_DIM:HEAD_DIM + 1])
    return jnp.concatenate(fin, axis=0).T


def _top_n_rows(g, valid, n):
    row = lax.broadcasted_iota(I32, g.shape, 0)
    g = jnp.where(valid, g, -jnp.inf)
    sel = jnp.zeros(g.shape, jnp.bool_)
    for _ in range(n):
        m = jnp.max(g, axis=0, keepdims=True)
        idx = jnp.min(jnp.where(g == m, row, g.shape[0]), axis=0, keepdims=True)
        pick = row == idx
        sel = jnp.logical_or(sel, pick)
        g = jnp.where(pick, -jnp.inf, g)
    return jnp.logical_and(sel, valid)


def _moba_kernel(qt_ref, k_ref, vt_ref, o_ref, kmean_ref, sel_ref, s_ref):
    i = pl.program_id(1)
    tq = qt_ref.shape[1]
    nb = k_ref.shape[0] // MOBA_BLOCK

    @pl.when(i == 0)
    def _():
        kmean_ref[...] = jnp.zeros(kmean_ref.shape, F32)

        def mean_body(n, c):
            kb = k_ref[pl.ds(pl.multiple_of(n * MOBA_BLOCK, MOBA_BLOCK), MOBA_BLOCK), :].astype(F32)
            kmean_ref[pl.ds(n, 1), :] = jnp.sum(kb, axis=0, keepdims=True) * (1.0 / MOBA_BLOCK)
            return c

        lax.fori_loop(0, nb, mean_body, 0)

    nh = s_ref.shape[1]
    qh = _head_rows(qt_ref[...])
    kmean = kmean_ref[...].astype(BF16)
    row_n = lax.broadcasted_iota(I32, (kmean.shape[0], tq), 0)
    for h in range(nh):
        gate = _dot(kmean[:, _pair_cols(h)], qh[h])
        sel_ref[h] = jnp.where(_top_n_rows(gate, row_n < i, MOBA_TOPK), 1.0, 0.0)

    def score_fn(n):
        kb = k_ref[pl.ds(pl.multiple_of(n * MOBA_BLOCK, MOBA_BLOCK), MOBA_BLOCK), :]
        return [_dot(kb[:, _pair_cols(h)], qh[h]) for h in range(nh)]

    def vt_fn(n, h):
        return vt_ref[n, h * HEAD_DIM:(h + 1) * HEAD_DIM, :]

    def keep_fn(n, h):
        return sel_ref[h, pl.ds(n, 1), :] > 0.5

    o_ref[...] = _attention_tile(i, score_fn, vt_fn, s_ref, tq, _causal_t(MOBA_BLOCK, tq), keep_fn).astype(BF16)


def moba_attention(cast, cast_t):
    t = cast.shape[0]
    tq = MOBA_BLOCK
    w = ATT_HEADS * HEAD_DIM
    nbp = max(8, t // MOBA_BLOCK)
    return pl.pallas_call(
        _moba_kernel,
        grid=(N_HEADS // ATT_HEADS, t // tq),
        in_specs=[pl.BlockSpec((None, w, tq), lambda g, i: (i, COL_AQ // w + g, 0)),
                  pl.BlockSpec((t, w), lambda g, i: (0, COL_AK // w + g)),
                  pl.BlockSpec((t // tq, w, tq), lambda g, i: (0, COL_AV // w + g, 0))],
        out_specs=pl.BlockSpec((tq, w), lambda g, i: (i, g)),
        out_shape=jax.ShapeDtypeStruct((t, ATT_W), BF16),
        scratch_shapes=[pltpu.VMEM((nbp, w), F32), pltpu.VMEM((ATT_HEADS, nbp, tq), F32),
                        pltpu.VMEM((2, ATT_HEADS, MOBA_BLOCK, tq), F32)],
        compiler_params=_cparams(("arbitrary", "arbitrary"), VMEM_LIMIT),
        name="moba_attention",
    )(cast_t, cast, cast_t)


HALF16 = 1 << 15


def _float_key(x):
    b = lax.bitcast_convert_type(x, I32)
    return jnp.where(b < 0, b ^ jnp.int32(0x7FFFFFFF), b)


def _dsa_select_kernel(qit_ref, ki_ref, misct_ref, bias_ref, key_ref, hi_ref, lo_ref):
    i = pl.program_id(0)
    tq = qit_ref.shape[1]
    tk = ATT_T
    n_chunks = ki_ref.shape[0] // tk
    n_live = i + 1

    qit = qit_ref[...]
    row = lax.broadcasted_iota(I32, (IDX_HEADS * IDX_DIM, 1), 0)
    zero = jnp.zeros_like(qit)
    qh = [jnp.where((row >= h * IDX_DIM) & (row < (h + 1) * IDX_DIM), qit, zero) for h in range(IDX_HEADS)]
    wrow = [misct_ref[MISC_WI + h:MISC_WI + h + 1, :] for h in range(IDX_HEADS)]
    visible = _causal_t(tk, tq)

    def score_body(j, c):
        kic = ki_ref[pl.ds(pl.multiple_of(j * tk, tk), tk), :]
        acc = jnp.zeros((tk, tq), F32)
        for h in range(IDX_HEADS):
            acc = acc + wrow[h] * jnp.maximum(_dot(kic, qh[h]), 0.0)
        acc = jnp.where(jnp.logical_or(j < i, visible), acc, -jnp.inf)
        key = _float_key(acc)
        key_ref[j] = key
        hi_ref[j] = (key >> 16).astype(I16)
        lo_ref[j] = ((key & 0xFFFF) - HALF16).astype(I16)
        return c

    lax.fori_loop(0, n_live, score_body, 0)

    def count_ge(cand):
        def body(j, cnt):
            for r in range(tk // 8):
                cnt = cnt + jnp.where(key_ref[j, r * 8:(r + 1) * 8, :] >= cand, 1.0, 0.0)
            return cnt
        cnt = lax.fori_loop(0, n_live, body, jnp.zeros((8, tq), F32))
        return jnp.sum(cnt, axis=0, keepdims=True)

    def count16(ref, cand):
        c16 = cand.astype(I16)

        def body(j, cnt):
            for r in range(tk // 16):
                cnt = cnt + jnp.where(ref[j, r * 16:(r + 1) * 16, :] >= c16, jnp.int16(1), jnp.int16(0))
            return cnt
        cnt = lax.fori_loop(0, n_live, body, jnp.zeros((16, tq), I16))
        return jnp.sum(cnt.astype(I32), axis=0, keepdims=True).astype(F32)

    def digit_select(ref, want):
        base = jnp.where(count16(ref, jnp.zeros((1, tq), I32)) >= want, 0, -HALF16)

        def bit_body(b, d):
            cand = d | (jnp.int32(1) << (14 - b))
            return jnp.where(count16(ref, cand) >= want, cand, d)

        return lax.fori_loop(0, 15, bit_body, base)

    topk = float(DSA_TOPK)
    t_hi = digit_select(hi_ref, topk)
    want_lo = topk - count16(hi_ref, t_hi + 1)
    t_hi16 = t_hi.astype(I16)

    def mask_body(j, c):
        lo_ref[j] = jnp.where(hi_ref[j] == t_hi16, lo_ref[j], jnp.int16(-HALF16))
        return c

    lax.fori_loop(0, n_live, mask_body, 0)
    t_lo = digit_select(lo_ref, want_lo)
    thr = (t_hi << 16) | (t_lo + HALF16)

    need = topk - count_ge(thr + 1)
    lower = jnp.where(lax.broadcasted_iota(I32, (tk, tk), 0) >= lax.broadcasted_iota(I32, (tk, tk), 1),
                      1.0, 0.0).astype(BF16)

    def bias_body(j, seen):
        kk = key_ref[j]
        eq = kk == thr
        rank = _dot(lower, jnp.where(eq, 1.0, 0.0).astype(BF16)) + seen
        keep = ((kk > thr) | (eq & (rank <= need))) & jnp.logical_or(j < i, visible)
        bias_ref[pl.ds(pl.multiple_of(j * tk, tk), tk), :] = jnp.where(keep, 0.0, NEG).astype(BF16)
        return rank[tk - 1:tk, :]

    lax.fori_loop(0, n_live, bias_body, jnp.zeros((1, tq), F32))

    def fill_body(j, c):
        bias_ref[pl.ds(pl.multiple_of(j * tk, tk), tk), :] = jnp.full((tk, tq), NEG, BF16)
        return c

    lax.fori_loop(n_live, n_chunks, fill_body, 0)


def dsa_select(cast, cast_t, misc_t):
    t = cast.shape[0]
    tq = ATT_T
    w = IDX_HEADS * IDX_DIM
    return pl.pallas_call(
        _dsa_select_kernel,
        grid=(t // tq,),
        in_specs=[pl.BlockSpec((None, w, tq), lambda i: (i, COL_CQI // w, 0)),
                  pl.BlockSpec((t, w), lambda i: (0, COL_KI4 // w)),
                  pl.BlockSpec((None, LANES, tq), lambda i: (i, 0, 0))],
        out_specs=pl.BlockSpec((None, t, tq), lambda i: (i, 0, 0)),
        out_shape=jax.ShapeDtypeStruct((t // tq, t, tq), BF16),
        scratch_shapes=[pltpu.VMEM((t // ATT_T, ATT_T, tq), I32), pltpu.VMEM((t // ATT_T, ATT_T, tq), I16),
                        pltpu.VMEM((t // ATT_T, ATT_T, tq), I16)],
        compiler_params=_cparams(("arbitrary",), VMEM_LIMIT),
        name="dsa_select",
    )(cast_t, cast, misc_t)


def _dsa_attn_kernel(qt_ref, k_ref, vt_ref, bias_ref, o_ref, s_ref):
    i = pl.program_id(1)
    tq = qt_ref.shape[1]
    tk = ATT_T
    nh = s_ref.shape[1]
    qh = _head_rows(qt_ref[...])

    def score_fn(n):
        off = pl.multiple_of(n * tk, tk)
        kb = k_ref[pl.ds(off, tk), :]
        bias = bias_ref[pl.ds(off, tk), :].astype(F32)
        return [_dot(kb[:, _pair_cols(h)], qh[h]) + bias for h in range(nh)]

    def vt_fn(n, h):
        return vt_ref[n, h * HEAD_DIM:(h + 1) * HEAD_DIM, :]

    o_ref[...] = _attention_tile(i, score_fn, vt_fn, s_ref, tq, None).astype(BF16)


def dsa_attention(cast, cast_t, bias):
    t = cast.shape[0]
    tq = ATT_T
    w = ATT_HEADS * HEAD_DIM
    return pl.pallas_call(
        _dsa_attn_kernel,
        grid=(N_HEADS // ATT_HEADS, t // tq),
        in_specs=[pl.BlockSpec((None, w, tq), lambda g, i: (i, COL_CQ // w + g, 0)),
                  pl.BlockSpec((t, w), lambda g, i: (0, COL_CK // w + g)),
                  pl.BlockSpec((t // tq, w, tq), lambda g, i: (0, COL_CV // w + g, 0)),
                  pl.BlockSpec((None, t, tq), lambda g, i: (i, 0, 0))],
        out_specs=pl.BlockSpec((tq, w), lambda g, i: (i, g)),
        out_shape=jax.ShapeDtypeStruct((t, ATT_W), BF16),
        scratch_shapes=[pltpu.VMEM((2, ATT_HEADS, ATT_T, tq), F32)],
        compiler_params=_cparams(("arbitrary", "arbitrary"), VMEM_LIMIT),
        name="dsa_attention",
    )(cast_t, cast, cast_t, bias)


def _rms(x, w):
    return x * lax.rsqrt(jnp.mean(x * x, axis=-1, keepdims=True) + LN_EPS) * w


def _mla_prep_kernel(cq_ref, ckv_ref, misc_ref, qn_ref, kvn_ref, wq_ref, wk_ref, wv_ref,
                     c_ref, s1_ref, s2_ref, qt_ref, k_ref, vt_ref, misct_ref):
    c, s1, s2 = c_ref[...], s1_ref[...], s2_ref[...]
    cq = _rms(cq_ref[...], qn_ref[...]).astype(BF16)
    ckv = _rms(ckv_ref[...], kvn_ref[...]).astype(BF16)
    misc = misc_ref[...]
    misct_ref[...] = misc.T
    lane = lax.broadcasted_iota(I32, (1, LANES), 1)
    kpe = pltpu.roll(jnp.where(lane < MLA_ROPE, misc, 0.0), MLA_NOPE, 1)
    kpe = _rot_apply(kpe, c, s1, s2, MLA_ROPE // 2)
    q = _dot(cq, wq_ref[...])
    k = _dot(ckv, wk_ref[...])
    for h in range(N_HEADS):
        sl = slice(h * LANES, (h + 1) * LANES)
        qt_ref[sl, :] = _rot_apply(q[:, sl], c, s1, s2, MLA_ROPE // 2).T.astype(BF16)
        k_ref[:, sl] = (k[:, sl] + kpe).astype(BF16)
    v = _dot(ckv, wv_ref[...])
    for g in range(ATT_W // LANES):
        sl = slice(g * LANES, (g + 1) * LANES)
        vt_ref[sl, :] = v[:, sl].T.astype(BF16)


def mla_prep(proj, qn, kvn, wq, wk, wv, tabs):
    t = proj.shape[0]
    tm = ATT_T
    full = lambda a: pl.BlockSpec(a.shape, lambda i: (0,) * a.ndim)
    row = lambda w, col: pl.BlockSpec((tm, w), lambda i: (i, col // w))
    tile_t = lambda n: pl.BlockSpec((None, n, tm), lambda i: (i, 0, 0))
    return pl.pallas_call(
        _mla_prep_kernel,
        grid=(t // tm,),
        in_specs=[row(MLA_Q_RANK, COL_CQL), row(MLA_KV_RANK, COL_CKV), row(LANES, COL_MISC),
                  full(qn), full(kvn), full(wq), full(wk), full(wv),
                  row(LANES, 0), row(LANES, 0), row(LANES, 0)],
        out_specs=[tile_t(N_HEADS * LANES),
                   pl.BlockSpec((tm, N_HEADS * LANES), lambda i: (i, 0)),
                   tile_t(ATT_W), tile_t(LANES)],
        out_shape=[jax.ShapeDtypeStruct((t // tm, N_HEADS * LANES, tm), BF16),
                   jax.ShapeDtypeStruct((t, N_HEADS * LANES), BF16),
                   jax.ShapeDtypeStruct((t // tm, ATT_W, tm), BF16),
                   jax.ShapeDtypeStruct((t // tm, LANES, tm), F32)],
        compiler_params=_cparams(("arbitrary",), VMEM_LIMIT),
        name="mla_prep",
    )(proj, proj, proj, qn, kvn, wq, wk, wv, *tabs)


def _mla_attn_kernel(qt_ref, k_ref, vt_ref, o_ref, s_ref):
    i = pl.program_id(1)
    tq = qt_ref.shape[1]
    tk = ATT_T
    nh = s_ref.shape[1]
    qh = [qt_ref[h * LANES:(h + 1) * LANES, :] for h in range(nh)]

    def score_fn(n):
        off = pl.multiple_of(n * tk, tk)
        return [_dot(k_ref[pl.ds(off, tk), h * LANES:(h + 1) * LANES], qh[h]) for h in range(nh)]

    def vt_fn(n, h):
        return vt_ref[n, h * HEAD_DIM:(h + 1) * HEAD_DIM, :]

    o_ref[...] = _attention_tile(i, score_fn, vt_fn, s_ref, tq, _causal_t(tk, tq)).astype(BF16)


def mla_attention(qt, k, vt):
    t = k.shape[0]
    tq = ATT_T
    wqk = ATT_HEADS * LANES
    wv = ATT_HEADS * HEAD_DIM
    return pl.pallas_call(
        _mla_attn_kernel,
        grid=(N_HEADS // ATT_HEADS, t // tq),
        in_specs=[pl.BlockSpec((None, wqk, tq), lambda g, i: (i, g, 0)),
                  pl.BlockSpec((t, wqk), lambda g, i: (0, g)),
                  pl.BlockSpec((t // tq, wv, tq), lambda g, i: (0, g, 0))],
        out_specs=pl.BlockSpec((tq, wv), lambda g, i: (i, g)),
        out_shape=jax.ShapeDtypeStruct((t, ATT_W), BF16),
        scratch_shapes=[pltpu.VMEM((2, ATT_HEADS, ATT_T, tq), F32)],
        compiler_params=_cparams(("arbitrary", "arbitrary"), VMEM_LIMIT),
        name="mla_attention",
    )(qt, k, vt)


def _silu(x):
    return x / (1.0 + jnp.exp(-x))


def _softplus(x):
    return jnp.maximum(x, 0.0) + jnp.log1p(jnp.exp(-jnp.abs(x)))


def _ssd_kernel(z_ref, xbc_ref, misc_ref, cw_ref, cb_ref, dtb_ref, a_ref, dsk_ref, nw_ref, exp_ref,
                o_ref, ext_ref, state_ref):
    c = pl.program_id(0)
    L = SSM_CHUNK
    half = SSM_INNER // SSM_GROUPS

    @pl.when(c == 0)
    def _():
        ext_ref[0:8, :] = jnp.zeros((8, SSM_CONV_CH), F32)
        state_ref[...] = jnp.zeros(state_ref.shape, F32)

    ext_ref[8:8 + L, :] = xbc_ref[...]
    conv = jnp.zeros((L, SSM_CONV_CH), F32) + cb_ref[...]
    for kk in range(SSM_CONV):
        conv = conv + cw_ref[kk:kk + 1, :] * ext_ref[pl.ds(8 - (SSM_CONV - 1) + kk, L), :]
    tail = ext_ref[L:L + 8, :]
    ext_ref[0:8, :] = tail
    xbc = _silu(conv)
    xs = xbc[:, 0:SSM_INNER]
    bm = xbc[:, SSM_INNER:SSM_INNER + SSM_GROUPS * SSM_STATE]
    cm = xbc[:, SSM_INNER + SSM_GROUPS * SSM_STATE:]

    dt = _softplus(misc_ref[...] + dtb_ref[...])
    da = dt * a_ref[...]
    r = lax.broadcasted_iota(I32, (L, L), 0)
    cc = lax.broadcasted_iota(I32, (L, L), 1)
    tri = r >= cc
    a_cum = _dot_hi(jnp.where(tri, 1.0, 0.0), da)
    a_cum_t = a_cum.T
    expand = exp_ref[...]
    dt_full = _dot_hi(dt, expand)
    acum_full = _dot_hi(a_cum, expand)
    a_last = acum_full[L - 1:L, :]
    xdt = xs * dt_full
    xdt_b = xdt.astype(BF16)

    ys = []
    for g in range(SSM_GROUPS):
        bg = bm[:, g * SSM_STATE:(g + 1) * SSM_STATE].astype(BF16)
        cg = cm[:, g * SSM_STATE:(g + 1) * SSM_STATE].astype(BF16)
        cb = _dot_nt(cg, bg)
        hpg = SSM_HEADS // SSM_GROUPS
        cols = []
        for hh in range(hpg):
            h = g * hpg + hh
            diff = a_cum[:, MISC_DT + h:MISC_DT + h + 1] - a_cum_t[MISC_DT + h:MISC_DT + h + 1, :]
            seg = jnp.where(tri, jnp.exp(jnp.where(tri, diff, 0.0)), 0.0)
            sc = (cb * seg).astype(BF16)
            pair = (h // 2) * LANES
            yp = _dot(sc, xdt_b[:, pair:pair + LANES])
            cols.append(yp[:, (h % 2) * SSM_HEAD_DIM:(h % 2 + 1) * SSM_HEAD_DIM])
        y_diag = jnp.concatenate(cols, axis=1)
        gs = slice(g * half, (g + 1) * half)
        st = state_ref[:, gs]
        y_off = _dot(cg, st.astype(BF16)) * jnp.exp(acum_full[:, gs])
        ys.append(y_diag + y_off)
        xw = (xdt[:, gs] * jnp.exp(a_last[:, gs] - acum_full[:, gs])).astype(BF16)
        new = _dot(bm[:, g * SSM_STATE:(g + 1) * SSM_STATE].T.astype(BF16), xw)
        state_ref[:, gs] = st * jnp.exp(a_last[:, gs]) + new
    y = jnp.concatenate(ys, axis=1) + xs * dsk_ref[...]
    y = y * _silu(z_ref[...])
    o_ref[...] = _rms(y, nw_ref[...]).astype(BF16)


def ssd_mixer(proj, conv_w, conv_b, dtb_row, a_row, dsk_row, norm_w, expand):
    t = proj.shape[0]
    L = SSM_CHUNK
    full = lambda a: pl.BlockSpec(a.shape, lambda c: (0,) * a.ndim)
    return pl.pallas_call(
        _ssd_kernel,
        grid=(t // L,),
        in_specs=[pl.BlockSpec((L, SSM_INNER), lambda c: (c, COL_MZ // SSM_INNER)),
                  pl.BlockSpec((L, SSM_CONV_CH), lambda c: (c, COL_XBC // SSM_CONV_CH)),
                  pl.BlockSpec((L, LANES), lambda c: (c, COL_MISC // LANES)),
                  full(conv_w), full(conv_b), full(dtb_row), full(a_row), full(dsk_row), full(norm_w),
                  full(expand)],
        out_specs=pl.BlockSpec((L, SSM_INNER), lambda c: (c, 0)),
        out_shape=jax.ShapeDtypeStruct((t, SSM_INNER), BF16),
        scratch_shapes=[pltpu.VMEM((L + 8, SSM_CONV_CH), F32),
                        pltpu.VMEM((SSM_STATE, SSM_INNER), F32)],
        compiler_params=_cparams(("arbitrary",), VMEM_LIMIT),
        name="ssd_mixer",
    )(proj, proj, proj, conv_w, conv_b, dtb_row, a_row, dsk_row, norm_w, expand)


def _merge_kernel(x_ref, wg0_ref, wg1_ref, wg2_ref, wg3_ref, bg_ref, oa_ref, ob_ref, oc_ref, od_ref, wb_ref, o_ref):
    x = x_ref[...]
    acc = None
    wg_refs = (wg0_ref, wg1_ref, wg2_ref, wg3_ref)
    for g, br in enumerate((oa_ref, ob_ref, oc_ref, od_ref)):
        logit = _dot(x, wg_refs[g][...]) + bg_ref[g:g + 1, :]
        gate = 1.0 / (1.0 + jnp.exp(-logit))
        term = gate * _dot(br[...], wb_ref[g])
        acc = term if acc is None else acc + term
    o_ref[...] = acc.astype(BF16)


def gated_merge(x_bf, wg, bg, branches, wb, tm=1024, tn=256):
    t = x_bf.shape[0]
    br_spec = pl.BlockSpec((tm, BRANCH_W), lambda j, i: (i, 0))
    nj = D_MODEL // tn
    wg_specs = [pl.BlockSpec((D_MODEL, tn), functools.partial(lambda j, i, g: (0, g * nj + j), g=g))
                for g in range(N_BRANCH)]
    return pl.pallas_call(
        _merge_kernel,
        grid=(nj, t // tm),
        in_specs=[pl.BlockSpec((tm, D_MODEL), lambda j, i: (i, 0)),
                  *wg_specs,
                  pl.BlockSpec((N_BRANCH, tn), lambda j, i: (0, j)),
                  br_spec, br_spec, br_spec, br_spec,
                  pl.BlockSpec((N_BRANCH, BRANCH_W, tn), lambda j, i: (0, 0, j))],
        out_specs=pl.BlockSpec((tm, tn), lambda j, i: (i, j)),
        out_shape=jax.ShapeDtypeStruct((t, D_MODEL), BF16),
        compiler_params=_cparams(("arbitrary", "arbitrary"), VMEM_LIMIT),
        name="gated_merge",
    )(x_bf, wg, wg, wg, wg, bg, *branches, wb)


def _layer_norm(v, g, b):
    mu = jnp.mean(v, axis=-1, keepdims=True)
    d = v - mu
    var = jnp.mean(d * d, axis=-1, keepdims=True)
    return d * lax.rsqrt(var + LN_EPS) * g + b


def _outproj_kernel(m_ref, w_ref, x_ref, g_ref, b_ref, rw_ref, rb_ref, x1_ref, x1b_ref, idx_ref, gate_ref):
    mixed = _dot(m_ref[...], w_ref[...])
    x1 = _layer_norm(DN_ALPHA * x_ref[...] + mixed, g_ref[...], b_ref[...])
    x1_ref[...] = x1
    x1b = x1.astype(BF16)
    x1b_ref[...] = x1b
    logits = _dot(x1b, rw_ref[...]) + rb_ref[...]
    lane = lax.broadcasted_iota(I32, logits.shape, 1)
    g = logits
    idx_out = jnp.zeros(logits.shape, I32)
    val_out = jnp.zeros(logits.shape, F32)
    vals = []
    for kk in range(TOP_K):
        m = jnp.max(g, axis=1, keepdims=True)
        idx = jnp.min(jnp.where(g == m, lane, LANES), axis=1, keepdims=True)
        g = jnp.where(lane == idx, -jnp.inf, g)
        idx_out = jnp.where(lane == kk, idx, idx_out)
        vals.append(m)
    es = [jnp.exp(v - vals[0]) for v in vals]
    den = es[0] + es[1] + es[2] + es[3]
    for kk in range(TOP_K):
        val_out = jnp.where(lane == kk, es[kk] / den, val_out)
    idx_ref[...] = idx_out
    gate_ref[...] = val_out


def outproj_ln_router(merged, w_out, x, g, b, rw, rb, tm=256):
    t = x.shape[0]
    full = lambda a: pl.BlockSpec(a.shape, lambda i: (0,) * a.ndim)
    row = lambda w: pl.BlockSpec((tm, w), lambda i: (i, 0))
    return pl.pallas_call(
        _outproj_kernel,
        grid=(t // tm,),
        in_specs=[row(D_MODEL), full(w_out), row(D_MODEL), full(g), full(b), full(rw), full(rb)],
        out_specs=[row(D_MODEL), row(D_MODEL), row(LANES), row(LANES)],
        out_shape=[jax.ShapeDtypeStruct((t, D_MODEL), F32), jax.ShapeDtypeStruct((t, D_MODEL), BF16),
                   jax.ShapeDtypeStruct((t, LANES), I32), jax.ShapeDtypeStruct((t, LANES), F32)],
        compiler_params=_cparams(("arbitrary",), VMEM_LIMIT),
        name="outproj_ln_router",
    )(merged, w_out, x, g, b, rw, rb)


def _start_row_gather(idx_ref, src_hbm, dst, sem, rows):
    for r in range(rows):
        pltpu.make_async_copy(src_hbm.at[pl.ds(idx_ref[0, r], 1)], dst.at[pl.ds(r, 1)], sem).start(priority=r % 2)


def _wait_row_gather(src_hbm, dst, sem, rows):
    pltpu.make_async_copy(src_hbm.at[pl.ds(0, rows)], dst, sem).wait()


def _expert_kernel(be_ref, nu_ref, tok_ref, tokn_ref, x_hbm, wgu_ref, bgu_ref, wd_ref, bd_ref, o_ref, xbuf, sem):
    blk = pl.program_id(0)
    n_used = nu_ref[0]
    slot = lax.rem(blk, 2)

    @pl.when(jnp.logical_and(blk == 0, n_used > 0))
    def _():
        _start_row_gather(tok_ref, x_hbm, xbuf.at[0], sem.at[0], MOE_ROWS)

    @pl.when(blk + 1 < n_used)
    def _():
        _start_row_gather(tokn_ref, x_hbm, xbuf.at[1 - slot], sem.at[1 - slot], MOE_ROWS)

    @pl.when(blk < n_used)
    def _():
        _wait_row_gather(x_hbm, xbuf.at[slot], sem.at[slot], MOE_ROWS)
        gu = _dot(xbuf[slot].astype(BF16), wgu_ref[0].astype(BF16)) + bgu_ref[0]
        glu = jnp.minimum(gu[:, :D_FF], SWIGLU_LIMIT)
        lin = jnp.clip(gu[:, D_FF:], -SWIGLU_LIMIT, SWIGLU_LIMIT)
        act = glu / (1.0 + jnp.exp(-SWIGLU_ALPHA * glu)) * (lin + 1.0)
        o_ref[...] = _dot(act.astype(BF16), wd_ref[0].astype(BF16)) + bd_ref[0]

    @pl.when(blk >= nu_ref[0])
    def _():
        o_ref[...] = jnp.zeros(o_ref.shape, F32)


def expert_ffn(block_e, n_used, row_tok, x, layer, wgu, bgu, wd, bd):
    n_blocks = row_tok.shape[0]
    n_rows = n_blocks * MOE_ROWS
    grid_spec = pltpu.PrefetchScalarGridSpec(
        num_scalar_prefetch=2,
        grid=(n_blocks,),
        in_specs=[pl.BlockSpec((None, 1, MOE_ROWS), lambda b, be, nu: (b, 0, 0), memory_space=pltpu.SMEM),
                  pl.BlockSpec((None, 1, MOE_ROWS), lambda b, be, nu: (jnp.minimum(b + 1, n_blocks - 1), 0, 0),
                               memory_space=pltpu.SMEM),
                  pl.BlockSpec(memory_space=pl.ANY),
                  pl.BlockSpec((None, 1, D_MODEL, 2 * D_FF), lambda b, be, nu: (layer, be[b], 0, 0)),
                  pl.BlockSpec((1, 1, 2 * D_FF), lambda b, be, nu: (be[b], 0, 0)),
                  pl.BlockSpec((None, 1, D_FF, D_MODEL), lambda b, be, nu: (layer, be[b], 0, 0)),
                  pl.BlockSpec((1, 1, D_MODEL), lambda b, be, nu: (be[b], 0, 0))],
        out_specs=pl.BlockSpec((MOE_ROWS, D_MODEL), lambda b, be, nu: (b, 0)),
        scratch_shapes=[pltpu.VMEM((2, MOE_ROWS, D_MODEL), F32), pltpu.SemaphoreType.DMA((2,))],
    )
    return pl.pallas_call(
        _expert_kernel,
        grid_spec=grid_spec,
        out_shape=jax.ShapeDtypeStruct((n_rows, D_MODEL), F32),
        compiler_params=_cparams(("arbitrary",), VMEM_LIMIT),
        name="expert_ffn",
    )(block_e, n_used, row_tok, row_tok, x, wgu, bgu, wd, bd)


COMBINE_TM = 128


def _combine_kernel(pos_ref, posn_ref, out_hbm, gate_ref, x_ref, g_ref, b_ref, o_ref, ob_ref, rbuf, sem):
    i = pl.program_id(0)
    tm = x_ref.shape[0]
    rows = TOP_K * tm
    slot = lax.rem(i, 2)

    @pl.when(i == 0)
    def _():
        _start_row_gather(pos_ref, out_hbm, rbuf.at[0], sem.at[0], rows)

    @pl.when(i + 1 < pl.num_programs(0))
    def _():
        _start_row_gather(posn_ref, out_hbm, rbuf.at[1 - slot], sem.at[1 - slot], rows)

    _wait_row_gather(out_hbm, rbuf.at[slot], sem.at[slot], rows)
    gate = gate_ref[...]
    y = jnp.zeros(x_ref.shape, F32)
    for kk in range(TOP_K):
        y = y + gate[:, kk:kk + 1] * rbuf[slot, kk * tm:(kk + 1) * tm, :]
    x2 = _layer_norm(DN_ALPHA * x_ref[...] + y, g_ref[...], b_ref[...])
    o_ref[...] = x2
    ob_ref[...] = x2.astype(BF16)


def combine_ln(pos_km, out, gates, x1, g, b):
    t = x1.shape[0]
    tm = COMBINE_TM
    n_tiles = t // tm
    full = lambda a: pl.BlockSpec(a.shape, lambda i: (0,) * a.ndim)
    row = lambda w: pl.BlockSpec((tm, w), lambda i: (i, 0))
    return pl.pallas_call(
        _combine_kernel,
        grid=(n_tiles,),
        in_specs=[pl.BlockSpec((None, 1, TOP_K * tm), lambda i: (i, 0, 0), memory_space=pltpu.SMEM),
                  pl.BlockSpec((None, 1, TOP_K * tm), lambda i: (jnp.minimum(i + 1, n_tiles - 1), 0, 0),
                               memory_space=pltpu.SMEM),
                  pl.BlockSpec(memory_space=pl.ANY), row(LANES), row(D_MODEL), full(g), full(b)],
        out_specs=[row(D_MODEL), row(D_MODEL)],
        out_shape=[jax.ShapeDtypeStruct((t, D_MODEL), F32), jax.ShapeDtypeStruct((t, D_MODEL), BF16)],
        scratch_shapes=[pltpu.VMEM((2, TOP_K * tm, D_MODEL), F32), pltpu.SemaphoreType.DMA((2,))],
        compiler_params=_cparams(("arbitrary",), VMEM_LIMIT),
        name="combine_ln",
    )(pos_km, pos_km, out, gates, x1, g, b)


def _rope_tables(positions):
    pos = positions.reshape(-1).astype(F32)

    def cs(dim, theta):
        inv = theta ** (-jnp.arange(0, dim, 2, dtype=F32) / dim)
        ang = pos[:, None] * inv
        return jnp.cos(ang), jnp.sin(ang)

    t = pos.shape[0]
    cos_p, sin_p = cs(ROT_DIM, ROPE_THETA)
    half = ROT_DIM // 2
    one = jnp.ones((t, HEAD_DIM - ROT_DIM), F32)
    zero = jnp.zeros((t, HEAD_DIM - ROT_DIM), F32)
    zh = jnp.zeros((t, half), F32)
    c64 = jnp.concatenate([cos_p, cos_p, one], axis=1)
    s1_64 = jnp.concatenate([-sin_p, zh, zero], axis=1)
    s2_64 = jnp.concatenate([zh, sin_p, zero], axis=1)
    tabs_p = tuple(jnp.tile(a, (1, 2)) for a in (c64, s1_64, s2_64))

    cos_m, sin_m = cs(MLA_ROPE, MLA_THETA)
    hm = MLA_ROPE // 2
    pre1 = jnp.ones((t, MLA_NOPE), F32)
    pre0 = jnp.zeros((t, MLA_NOPE), F32)
    post1 = jnp.ones((t, LANES - MLA_NOPE - MLA_ROPE), F32)
    post0 = jnp.zeros((t, LANES - MLA_NOPE - MLA_ROPE), F32)
    zm = jnp.zeros((t, hm), F32)
    c_m = jnp.concatenate([pre1, cos_m, cos_m, post1], axis=1)
    s1_m = jnp.concatenate([pre0, -sin_m, zm, post0], axis=1)
    s2_m = jnp.concatenate([pre0, zm, sin_m, post0], axis=1)
    return tabs_p, (c_m, s1_m, s2_m)


def _prep_in_weights(w_in):
    pts = np.cumsum((0,) + SPLIT_SIZES)
    seg = lambda k: w_in[:, pts[k]:pts[k + 1]]
    (a_q, a_k, a_v, m_z, m_xbc, m_dt, c_q, c_k, c_v, c_qi, c_ki, c_wi, d_cq, d_ckv, d_kpe) = [seg(k) for k in range(15)]
    scale = HEAD_DIM ** -0.5 * LOG2E
    zeros = lambda n: jnp.zeros((D_MODEL, n), w_in.dtype)
    attn = jnp.concatenate([a_q * scale, a_k, a_v, c_q * scale, c_k, c_v, c_qi, c_ki, c_ki, c_ki, c_ki], axis=1)
    misc_pad = zeros(LANES - MLA_ROPE - SSM_HEADS - IDX_HEADS)
    rest = jnp.concatenate([m_xbc, m_z, d_cq, d_kpe, m_dt, c_wi * IDX_SCALE, misc_pad, d_ckv], axis=1)
    assert attn.shape[1] == N_CAST and rest.shape[1] == N_REST
    return attn.astype(BF16), rest.astype(BF16)


def _prep_mla_weights(w_uq, w_ukv):
    dq = MLA_NOPE + MLA_ROPE
    wq = w_uq.reshape(MLA_Q_RANK, N_HEADS, dq) * (dq ** -0.5 * LOG2E)
    wq = jnp.pad(wq, ((0, 0), (0, 0), (0, LANES - dq))).reshape(MLA_Q_RANK, N_HEADS * LANES)
    wkv = w_ukv.reshape(MLA_KV_RANK, N_HEADS, MLA_NOPE + MLA_V)
    wk = jnp.pad(wkv[:, :, :MLA_NOPE], ((0, 0), (0, 0), (0, LANES - MLA_NOPE))).reshape(MLA_KV_RANK, N_HEADS * LANES)
    wv = wkv[:, :, MLA_NOPE:].reshape(MLA_KV_RANK, N_HEADS * MLA_V)
    return wq.astype(BF16), wk.astype(BF16), wv.astype(BF16)


def _misc_row(v):
    return jnp.zeros((1, LANES), F32).at[0, MISC_DT:MISC_DT + SSM_HEADS].set(v.astype(F32))


def _route(top_idx, t):
    tk = t * TOP_K
    flat_e = top_idx.reshape(-1)
    onehot = (flat_e[:, None] == jnp.arange(N_EXPERTS)[None, :]).astype(I32)
    seen = jnp.cumsum(onehot, axis=0)
    counts = seen[-1]
    padded = ((counts + MOE_ROWS - 1) // MOE_ROWS) * MOE_ROWS
    pad_end = jnp.cumsum(padded)
    pad_start = pad_end - padded
    rank = jnp.sum(onehot * seen, axis=1) - 1
    pos = (pad_start[flat_e] + rank).astype(I32)
    n_blocks = tk // MOE_ROWS + N_EXPERTS
    n_rows = n_blocks * MOE_ROWS
    block_start = jnp.arange(n_blocks) * MOE_ROWS
    block_e = jnp.minimum(jnp.sum(pad_end[None, :] <= block_start[:, None], axis=1), N_EXPERTS - 1).astype(I32)
    order = jnp.argsort(flat_e, stable=True).astype(I32)
    row_e = jnp.repeat(block_e, MOE_ROWS)
    start = jnp.cumsum(counts) - counts
    src = start[row_e] + (jnp.arange(n_rows, dtype=I32) - pad_start[row_e])
    row_tok = order[jnp.clip(src, 0, tk - 1)] // TOP_K
    n_used = (pad_end[-1] // MOE_ROWS).astype(I32).reshape(1)
    return row_tok, pos, block_e, n_used


def _layer(x, x_bf, tabs_p, tabs_m, flags, expand, p, layer, big):
    t = x.shape[0]
    w_attn, w_rest = _prep_in_weights(p["w_in"])
    cast, cast_t = proj_attention(x_bf, w_attn, flags, tabs_p)
    proj = matmul_bf16(x_bf, w_rest, tn=N_REST // 2)

    o_a = moba_attention(cast, cast_t)
    o_b = ssd_mixer(proj, p["conv_w"], p["conv_b"].reshape(1, -1), _misc_row(p["dt_bias"]),
                    _misc_row(-jnp.exp(p["a_log"].astype(F32))),
                    jnp.repeat(p["d_skip"].astype(F32), SSM_HEAD_DIM).reshape(1, -1),
                    p["ssm_norm_w"].reshape(1, -1), expand)
    wq, wk, wv = _prep_mla_weights(p["w_uq"], p["w_ukv"])
    qt_m, k_m, vt_m, misc_t = mla_prep(proj, p["q_norm_w"].reshape(1, -1), p["kv_norm_w"].reshape(1, -1), wq, wk, wv,
                                       tabs_m)
    o_c = dsa_attention(cast, cast_t, dsa_select(cast, cast_t, misc_t))
    o_d = mla_attention(qt_m, k_m, vt_m)

    wg = p["w_in"][:, N_GATE0:].astype(BF16)
    merged = gated_merge(x_bf, wg, p["b_gate"], (o_a, o_b, o_c, o_d), p["w_branch"].astype(BF16))
    rw = jnp.pad(p["router_w"], ((0, 0), (0, LANES - N_EXPERTS))).astype(BF16)
    rb = jnp.concatenate([p["router_b"].astype(F32), jnp.full((LANES - N_EXPERTS,), -jnp.inf, F32)]).reshape(1, -1)
    x1, x1_bf, top_idx, gates = outproj_ln_router(merged, p["w_out"].astype(BF16), x, p["ln1_g"].reshape(1, -1),
                                                  p["ln1_b"].reshape(1, -1), rw, rb)

    row_tok, pos, block_e, n_used = _route(top_idx[:, :TOP_K], t)
    out = expert_ffn(block_e, n_used, row_tok.reshape(-1, 1, MOE_ROWS), x1, layer, big["w_gate_up"],
                     p["b_gate_up"][:, None, :], big["w_down"], p["b_down"][:, None, :])
    pos_km = pos.reshape(t // COMBINE_TM, COMBINE_TM, TOP_K).transpose(0, 2, 1).reshape(-1, 1, TOP_K * COMBINE_TM)
    return combine_ln(pos_km, out, gates, x1, p["ln2_g"].reshape(1, -1), p["ln2_b"].reshape(1, -1))


def kernel(x, positions, w_in, b_gate, conv_w, conv_b, dt_bias, a_log, d_skip, ssm_norm_w, q_norm_w, w_uq,
           kv_norm_w, w_ukv, w_branch, w_out, ln1_g, ln1_b, router_w, router_b, w_gate_up, b_gate_up, w_down,
           b_down, ln2_g, ln2_b):
    b, s, d = x.shape
    assert b == 1 and d == D_MODEL
    names = ("w_in", "b_gate", "conv_w", "conv_b", "dt_bias", "a_log", "d_skip", "ssm_norm_w", "q_norm_w", "w_uq",
             "kv_norm_w", "w_ukv", "w_branch", "w_out", "ln1_g", "ln1_b", "router_w", "router_b", "w_gate_up",
             "b_gate_up", "w_down", "b_down", "ln2_g", "ln2_b")
    stacked = (w_in, b_gate, conv_w, conv_b, dt_bias, a_log, d_skip, ssm_norm_w, q_norm_w, w_uq, kv_norm_w, w_ukv,
               w_branch, w_out, ln1_g, ln1_b, router_w, router_b, w_gate_up, b_gate_up, w_down, b_down, ln2_g, ln2_b)
    tabs_p, tabs_m = _rope_tables(positions)
    flags = jnp.array([1, 1, 0, 1, 1, 0, 1], I32)
    head_of_lane = jnp.arange(SSM_INNER) // SSM_HEAD_DIM
    expand = (jnp.arange(LANES)[:, None] == (MISC_DT + head_of_lane)[None, :]).astype(F32)

    big = {"w_gate_up": w_gate_up, "w_down": w_down}
    xt = x.reshape(s, d)
    xt_bf = xt.astype(BF16)
    for l in range(DEPTH):
        p = {n: a[l] for n, a in zip(names, stacked)}
        xt, xt_bf = _layer(xt, xt_bf, tabs_p, tabs_m, flags, expand, p, l, big)
    return xt.reshape(b, s, d)
```
